```python
import math
import jax, jax.numpy as jnp
from jax import lax
import numpy as np

D_MODEL = 1024
BATCH = 4
SEQ = 8192
DEPTH = 2

CHUNK = 64
Q_BLOCK = 128
N_MIXERS = 2
N_ATTN_LAYERS = (DEPTH + 1) // 2
N_RET_LAYERS = DEPTH // 2

DA_HEADS = 8
DA_HEAD_DIM = D_MODEL // (2 * DA_HEADS)
DA_V_DIM = 2 * DA_HEAD_DIM
DA_QK_WIDTH = 2 * DA_HEADS * DA_HEAD_DIM
DA_V_WIDTH = DA_HEADS * DA_V_DIM
DA_IN_WIDTH = 2 * DA_QK_WIDTH + 2 * DA_V_WIDTH

RET_HEADS = 4
RET_QK_DIM = D_MODEL // RET_HEADS
RET_V_DIM = 2 * RET_QK_DIM
RET_QK_WIDTH = RET_HEADS * RET_QK_DIM
RET_V_WIDTH = RET_HEADS * RET_V_DIM
RET_IN_WIDTH = 2 * RET_QK_WIDTH + 2 * RET_V_WIDTH

NORM_EPS = 1e-6
HEAD_NORM_EPS = 1e-5

kernel_name = "hybrid_diffattn_retention_streaming_block"


def rms_norm(x, gain, eps=NORM_EPS):
    xf = x.astype(jnp.float32)
    y = xf * lax.rsqrt(jnp.mean(xf * xf, axis=-1, keepdims=True) + eps)
    return (y * gain.astype(jnp.float32)).astype(x.dtype)


def head_rms_norm(x, eps=HEAD_NORM_EPS):
    xf = x.astype(jnp.float32)
    return xf * lax.rsqrt(jnp.mean(xf * xf, axis=-1, keepdims=True) + eps)


def alibi_slopes(n_heads):
    return jnp.asarray([2.0 ** (-8.0 * (h + 1) / n_heads) for h in range(n_heads)], dtype=jnp.float32)


def retention_log_gammas(n_heads):
    gammas = 1.0 - 2.0 ** (-5.0 - jnp.arange(n_heads, dtype=jnp.float32))
    return jnp.log(gammas)


def diff_attention(h, w_in, w_out, lq1, lk1, lq2, lk2, subln_gain, lam_init):
    B, S, _ = h.shape
    f32 = jnp.float32
    proj = jnp.einsum("bsd,de->bse", h, w_in)
    q, k, v, g = jnp.split(proj, [DA_QK_WIDTH, 2 * DA_QK_WIDTH, 2 * DA_QK_WIDTH + DA_V_WIDTH], axis=-1)
    q = q.reshape(B, S, DA_HEADS, 2, DA_HEAD_DIM).astype(f32) * (DA_HEAD_DIM ** -0.5)
    k = k.reshape(B, S, DA_HEADS, 2, DA_HEAD_DIM).astype(f32)
    v = v.reshape(B, S, DA_HEADS, DA_V_DIM).astype(f32)
    lam = (jnp.exp(jnp.sum(lq1.astype(f32) * lk1.astype(f32)))
           - jnp.exp(jnp.sum(lq2.astype(f32) * lk2.astype(f32))) + lam_init)
    slopes = alibi_slopes(DA_HEADS)
    k_pos = jnp.arange(S)
    n_blocks = S // Q_BLOCK
    q_blocks = jnp.moveaxis(q.reshape(B, n_blocks, Q_BLOCK, DA_HEADS, 2, DA_HEAD_DIM), 1, 0)

    def attend(args):
        q_blk, blk = args
        q_pos = blk * Q_BLOCK + jnp.arange(Q_BLOCK)
        scores = jnp.einsum("bqhcd,bkhcd->bhcqk", q_blk, k)
        dist = jnp.abs(q_pos[:, None] - k_pos[None, :]).astype(f32)
        bias = -slopes[:, None, None, None] * dist
        allowed = (k_pos[None, :] // CHUNK) <= (q_pos[:, None] // CHUNK)
        scores = jnp.where(allowed, scores + bias, -jnp.inf)
        probs = jax.nn.softmax(scores, axis=-1)
        diff = probs[:, :, 0] - lam * probs[:, :, 1]
        return jnp.einsum("bhqk,bkhe->bqhe", diff, v)

    out = lax.map(attend, (q_blocks, jnp.arange(n_blocks)))
    out = jnp.moveaxis(out, 0, 1).reshape(B, S, DA_HEADS, DA_V_DIM)
    out = head_rms_norm(out) * subln_gain.astype(f32) * (1.0 - lam_init)
    y = jax.nn.silu(g.astype(f32)) * out.reshape(B, S, DA_V_WIDTH)
    return jnp.einsum("bse,ed->bsd", y.astype(h.dtype), w_out)


def retention(h, w_in, w_out):
    B, S, _ = h.shape
    f32 = jnp.float32
    n_chunks = S // CHUNK
    proj = jnp.einsum("bsd,de->bse", h, w_in)
    q, k, v, g = jnp.split(proj, [RET_QK_WIDTH, 2 * RET_QK_WIDTH, 2 * RET_QK_WIDTH + RET_V_WIDTH], axis=-1)
    q = q.reshape(B, n_chunks, CHUNK, RET_HEADS, RET_QK_DIM).astype(f32)
    k = k.reshape(B, n_chunks, CHUNK, RET_HEADS, RET_QK_DIM).astype(f32) * (RET_QK_DIM ** -0.5)
    v = v.reshape(B, n_chunks, CHUNK, RET_HEADS, RET_V_DIM).astype(f32)

    log_gamma = retention_log_gammas(RET_HEADS)
    idx = jnp.arange(CHUNK, dtype=f32)
    intra_decay = jnp.exp(log_gamma[:, None, None] * jnp.abs(idx[:, None] - idx[None, :]))
    query_decay = jnp.exp(idx[:, None] * log_gamma[None, :])
    key_decay = jnp.exp((CHUNK - idx)[:, None] * log_gamma[None, :])
    chunk_decay = jnp.exp(CHUNK * log_gamma)

    inner = jnp.einsum("bnihd,bnjhd->bnhij", q, k) * intra_decay
    inner_out = jnp.einsum("bnhij,bnjhe->bnihe", inner, v)

    def step(state, qkv):
        q_c, k_c, v_c = qkv
        cross = jnp.einsum("bihd,bhde->bihe", q_c * query_decay[None, :, :, None], state)
        state = (chunk_decay[None, :, None, None] * state
                 + jnp.einsum("bjhd,bjhe->bhde", k_c * key_decay[None, :, :, None], v_c))
        return state, cross

    state0 = jnp.zeros((B, RET_HEADS, RET_QK_DIM, RET_V_DIM), f32)
    _, cross = lax.scan(step, state0, (jnp.moveaxis(q, 1, 0), jnp.moveaxis(k, 1, 0), jnp.moveaxis(v, 1, 0)))
    out = inner_out + jnp.moveaxis(cross, 0, 1)
    out = head_rms_norm(out).reshape(B, S, RET_V_WIDTH)
    y = jax.nn.silu(g.astype(f32)) * out
    return jnp.einsum("bse,ed->bsd", y.astype(h.dtype), w_out)


def setup_inputs(seed: int = 0) -> dict:
    key = jax.random.key(seed)
    ks = jax.random.split(key, 16)
    D = D_MODEL
    nrm = jax.random.normal
    return {
        "x": nrm(ks[0], (BATCH, SEQ, D), jnp.float32),
        "c": nrm(ks[1], (BATCH, D), jnp.float32),
        "ada_w": nrm(ks[2], (DEPTH, D, 3 * D), jnp.float32) * (0.5 * D ** -0.5),
        "ada_b": nrm(ks[3], (DEPTH, 3 * D), jnp.float32) * 0.01,
        "pre_gain": 1.0 + 0.05 * nrm(ks[4], (DEPTH, D), jnp.float32),
        "post_gain": 1.0 + 0.05 * nrm(ks[5], (DEPTH, D), jnp.float32),
        "da_w_in": nrm(ks[6], (N_ATTN_LAYERS, D, DA_IN_WIDTH), jnp.float32) * D ** -0.5,
        "da_w_out": nrm(ks[7], (N_ATTN_LAYERS, DA_V_WIDTH, D), jnp.float32) * DA_V_WIDTH ** -0.5,
        "da_lambda_q1": 0.1 * nrm(ks[8], (N_ATTN_LAYERS, DA_HEAD_DIM), jnp.float32),
        "da_lambda_k1": 0.1 * nrm(ks[9], (N_ATTN_LAYERS, DA_HEAD_DIM), jnp.float32),
        "da_lambda_q2": 0.1 * nrm(ks[10], (N_ATTN_LAYERS, DA_HEAD_DIM), jnp.float32),
        "da_lambda_k2": 0.1 * nrm(ks[11], (N_ATTN_LAYERS, DA_HEAD_DIM), jnp.float32),
        "da_subln_gain": 1.0 + 0.05 * nrm(ks[12], (N_ATTN_LAYERS, DA_V_DIM), jnp.float32),
        "ret_w_in": nrm(ks[13], (N_RET_LAYERS, D, RET_IN_WIDTH), jnp.float32) * D ** -0.5,
        "ret_w_out": nrm(ks[14], (N_RET_LAYERS, RET_V_WIDTH, D), jnp.float32) * RET_V_WIDTH ** -0.5,
    }


def reference(x, c, ada_w, ada_b, pre_gain, post_gain, da_w_in, da_w_out, da_lambda_q1, da_lambda_k1,
              da_lambda_q2, da_lambda_k2, da_subln_gain, ret_w_in, ret_w_out):
    cond = jax.nn.silu(c)
    for layer in range(DEPTH):
        mod = jnp.einsum("bd,de->be", cond, ada_w[layer]) + ada_b[layer]
        shift, scale, gate = jnp.split(mod, 3, axis=-1)
        h = rms_norm(x, pre_gain[layer]) * (1.0 + scale[:, None, :]) + shift[:, None, :]
        j = layer // N_MIXERS
        if layer % N_MIXERS == 0:
            lam_init = 0.8 - 0.6 * math.exp(-0.3 * layer)
            y = diff_attention(h, da_w_in[j], da_w_out[j], da_lambda_q1[j], da_lambda_k1[j],
                               da_lambda_q2[j], da_lambda_k2[j], da_subln_gain[j], lam_init)
        else:
            y = retention(h, ret_w_in[j], ret_w_out[j])
        x = x + gate[:, None, :] * rms_norm(y, post_gain[layer])
    return x
```

```python
import functools
import math

import jax
import jax.numpy as jnp
from jax import lax
from jax.experimental import pallas as pl
from jax.experimental.pallas import tpu as pltpu

CHUNK = 64
DA_HEADS = 8
DA_HEAD_DIM = 64
DA_V_DIM = 128
RET_HEADS = 4
RET_QK_DIM = 256
RET_V_DIM = 512
NORM_EPS = 1e-6
HEAD_NORM_EPS = 1e-5
MASK_VALUE = -1e30

V7X_VMEM_LIMIT_BYTES = 56 * 1024 * 1024
LANES = 128

BF16 = jnp.bfloat16
F32 = jnp.float32


def _cparams(sem, vmem=V7X_VMEM_LIMIT_BYTES):
    return pltpu.CompilerParams(dimension_semantics=sem, vmem_limit_bytes=vmem)


def _mod_kernel(c_ref, w_ref, b_ref, o_ref):
    c = c_ref[...]
    cond = c * jax.nn.sigmoid(c)
    o_ref[0] = jnp.dot(cond, w_ref[0], preferred_element_type=F32) + b_ref[0]


def _modulation(c, ada_w, ada_b):
    depth, d, d3 = ada_w.shape
    b = c.shape[0]
    n_tiles = d3 // d
    return pl.pallas_call(
        _mod_kernel,
        grid=(depth, n_tiles),
        in_specs=[
            pl.BlockSpec((b, d), lambda l, j: (0, 0)),
            pl.BlockSpec((1, d, d), lambda l, j: (l, 0, j)),
            pl.BlockSpec((1, 1, d), lambda l, j: (l, 0, j)),
        ],
        out_specs=pl.BlockSpec((1, b, d), lambda l, j: (l, 0, j)),
        out_shape=jax.ShapeDtypeStruct((depth, b, d3), F32),
        compiler_params=_cparams(("arbitrary", "arbitrary")),
        name="modulation",
    )(c, ada_w, ada_b.reshape(depth, 1, d3))


def _inproj_kernel(x_ref, mod_ref, gain_ref, w_ref, *refs, d_model, plan, slab):
    out_refs = refs[:len(plan)]
    hb_ref = refs[len(plan)]
    x = x_ref[0]
    ms = jnp.mean(x * x, axis=-1, keepdims=True)
    hn = x * lax.rsqrt(ms + NORM_EPS) * gain_ref[...]
    shift = mod_ref[0, :, 0:d_model]
    scale = mod_ref[0, :, d_model:2 * d_model]
    hb_ref[...] = (hn * (1.0 + scale) + shift).astype(BF16)
    for (c0, n_heads, width), o_ref in zip(plan, out_refs):
        total = width if n_heads is None else n_heads * width
        for s0 in range(0, total, slab):
            r = jnp.dot(hb_ref[...], w_ref[:, c0 + s0:c0 + s0 + slab],
                        preferred_element_type=F32).astype(BF16)
            if n_heads is None:
                o_ref[0, :, s0:s0 + slab] = r
            elif width >= slab:
                hh, off = divmod(s0, width)
                o_ref[0, hh, :, off:off + slab] = r
            else:
                for p in range(slab // width):
                    o_ref[0, s0 // width + p] = r[:, p * width:(p + 1) * width]


def _in_projection(x, mod, gain, w_bf16, plan, tm, slab):
    b, s, d = x.shape
    n = w_bf16.shape[1]
    out_shapes, out_specs = [], []
    for (c0, n_heads, width) in plan:
        if n_heads is None:
            out_shapes.append(jax.ShapeDtypeStruct((b, s, width), BF16))
            out_specs.append(pl.BlockSpec((1, tm, width), lambda bi, i: (bi, i, 0)))
        else:
            out_shapes.append(jax.ShapeDtypeStruct((b, n_heads, s, width), BF16))
            out_specs.append(pl.BlockSpec((1, n_heads, tm, width), lambda bi, i: (bi, 0, i, 0)))
    return pl.pallas_call(
        functools.partial(_inproj_kernel, d_model=d, plan=plan, slab=slab),
        grid=(b, s // tm),
        in_specs=[
            pl.BlockSpec((1, tm, d), lambda bi, i: (bi, i, 0)),
            pl.BlockSpec((1, 1, 3 * d), lambda bi, i: (bi, 0, 0)),
            pl.BlockSpec((1, d), lambda bi, i: (0, 0)),
            pl.BlockSpec((d, n), lambda bi, i: (0, 0)),
        ],
        out_specs=out_specs,
        out_shape=out_shapes,
        scratch_shapes=[pltpu.VMEM((tm, d), BF16)],
        compiler_params=_cparams(("arbitrary", "arbitrary")),
        name="in_projection",
    )(x, mod, gain, w_bf16)


def _outproj_kernel(y_ref, w_ref, x_ref, mod_ref, gain_ref, o_ref, *, d_model):
    t = jnp.dot(y_ref[0], w_ref[...], preferred_element_type=F32)
    ms = jnp.mean(t * t, axis=-1, keepdims=True)
    n = t * lax.rsqrt(ms + NORM_EPS) * gain_ref[...]
    gate = mod_ref[0, :, 2 * d_model:3 * d_model]
    o_ref[0] = x_ref[0] + gate * n


def _out_projection(y, w_bf16, x, mod, gain, tm):
    b, s, d = x.shape
    k = y.shape[-1]
    return pl.pallas_call(
        functools.partial(_outproj_kernel, d_model=d),
        grid=(b, s // tm),
        in_specs=[
            pl.BlockSpec((1, tm, k), lambda bi, i: (bi, i, 0)),
            pl.BlockSpec((k, d), lambda bi, i: (0, 0)),
            pl.BlockSpec((1, tm, d), lambda bi, i: (bi, i, 0)),
            pl.BlockSpec((1, 1, 3 * d), lambda bi, i: (bi, 0, 0)),
            pl.BlockSpec((1, d), lambda bi, i: (0, 0)),
        ],
        out_specs=pl.BlockSpec((1, tm, d), lambda bi, i: (bi, i, 0)),
        out_shape=jax.ShapeDtypeStruct((b, s, d), F32),
        compiler_params=_cparams(("arbitrary", "arbitrary")),
        name="out_projection",
    )(y, w_bf16, x, mod, gain)


def _attn_kernel(slope_ref, q_ref, k_ref, v_ref, g_ref, lamv_ref, subln_ref, y_ref,
                 qs_ref, m_ref, l_ref, acc_ref, boff_ref, bdiag_ref, *, tq, lam_init):
    h = pl.program_id(1)
    qi = pl.program_id(2)
    tk = tq
    slope = slope_ref[h]

    q = q_ref[0, 0] * jnp.asarray(DA_HEAD_DIM ** -0.5, BF16)
    lane = lax.broadcasted_iota(jnp.int32, (tq, 2 * DA_HEAD_DIM), 1)
    zero = jnp.zeros_like(q)
    qs_ref[0:tq] = jnp.where(lane < DA_HEAD_DIM, q, zero)
    qs_ref[tq:2 * tq] = jnp.where(lane >= DA_HEAD_DIM, q, zero)

    ii = lax.broadcasted_iota(jnp.int32, (2 * tq, tk), 0)
    ii = jnp.where(ii >= tq, ii - tq, ii)
    jj = lax.broadcasted_iota(jnp.int32, (2 * tq, tk), 1)
    rel = (jj - ii).astype(F32)
    boff_ref[...] = slope * rel
    allowed = (jj >> 6) <= (ii >> 6)
    bdiag_ref[...] = jnp.where(allowed, -slope * jnp.abs(rel), MASK_VALUE)

    m_ref[...] = jnp.full(m_ref.shape, MASK_VALUE, F32)
    l_ref[...] = jnp.zeros(l_ref.shape, F32)
    acc_ref[...] = jnp.zeros(acc_ref.shape, F32)

    def block(j, bias_ref, shift):
        k = k_ref[0, 0, pl.ds(pl.multiple_of(j * tk, tk), tk), :]
        v = v_ref[0, 0, pl.ds(pl.multiple_of(j * tk, tk), tk), :]
        s = lax.dot_general(qs_ref[...], k, (((1,), (1,)), ((), ())),
                            preferred_element_type=F32)
        s = s + bias_ref[...]
        m_prev = m_ref[...]
        m_curr = jnp.max(s, axis=1, keepdims=True) + shift
        m_next = jnp.maximum(m_prev, m_curr)
        mm = m_next - shift
        p = jnp.exp(s - jnp.concatenate([mm] * (tk // LANES), axis=1))
        alpha = jnp.exp(m_prev - m_next)
        l_ref[...] = alpha * l_ref[...] + jnp.sum(p, axis=1, keepdims=True)
        m_ref[...] = m_next
        pv = jnp.dot(p.astype(BF16), v, preferred_element_type=F32)
        acc_ref[...] = alpha * acc_ref[...] + pv

    def off_diag(j, carry):
        shift = slope * ((j - qi) * tk).astype(F32)
        block(j, boff_ref, shift)
        return carry

    lax.fori_loop(0, qi, off_diag, 0)
    block(qi, bdiag_ref, jnp.float32(0.0))

    lv = lamv_ref[...]
    lam = (jnp.exp(jnp.sum(lv[0:1] * lv[1:2], axis=-1, keepdims=True))
           - jnp.exp(jnp.sum(lv[2:3] * lv[3:4], axis=-1, keepdims=True)) + lam_init)
    o = acc_ref[...] / l_ref[...]
    out = o[0:tq] - lam * o[tq:2 * tq]
    ms = jnp.mean(out * out, axis=-1, keepdims=True)
    out = out * lax.rsqrt(ms + HEAD_NORM_EPS) * subln_ref[...] * (1.0 - lam_init)
    g = g_ref[0].astype(F32)
    y_ref[0] = (g * jax.nn.sigmoid(g) * out).astype(BF16)


def _diff_attention(q, k, v, g, slopes, lamv, subln, lam_init, tq):
    b, nh, s, dk = q.shape
    kernel = functools.partial(_attn_kernel, tq=tq, lam_init=lam_init)
    return pl.pallas_call(
        kernel,
        grid=(b, nh, s // tq),
        in_specs=[
            pl.BlockSpec(memory_space=pltpu.SMEM),
            pl.BlockSpec((1, 1, tq, dk), lambda bi, h, i: (bi, h, i, 0)),
            pl.BlockSpec((1, 1, s, dk), lambda bi, h, i: (bi, h, 0, 0)),
            pl.BlockSpec((1, 1, s, DA_V_DIM), lambda bi, h, i: (bi, h, 0, 0)),
            pl.BlockSpec((1, tq, DA_V_DIM), lambda bi, h, i: (bi, i, h)),
            pl.BlockSpec((4, DA_HEAD_DIM), lambda bi, h, i: (0, 0)),
            pl.BlockSpec((1, DA_V_DIM), lambda bi, h, i: (0, 0)),
        ],
        out_specs=pl.BlockSpec((1, tq, DA_V_DIM), lambda bi, h, i: (bi, i, h)),
        out_shape=jax.ShapeDtypeStruct((b, s, nh * DA_V_DIM), BF16),
        scratch_shapes=[
            pltpu.VMEM((2 * tq, dk), BF16),
            pltpu.VMEM((2 * tq, LANES), F32),
            pltpu.VMEM((2 * tq, LANES), F32),
            pltpu.VMEM((2 * tq, DA_V_DIM), F32),
            pltpu.VMEM((2 * tq, tq), F32),
            pltpu.VMEM((2 * tq, tq), F32),
        ],
        compiler_params=_cparams(("arbitrary", "arbitrary", "arbitrary")),
        name="diff_attention",
    )(slopes, q, k, v, g, lamv, subln)


def _ret_kernel(lg_ref, q_ref, k_ref, v_ref, g_ref, y_ref, state_ref, decay_ref, *, blk):
    h = pl.program_id(1)
    n = pl.program_id(2)
    lg = lg_ref[h]

    @pl.when(n == 0)
    def _():
        state_ref[...] = jnp.zeros(state_ref.shape, F32)
        ti = lax.broadcasted_iota(jnp.int32, (blk, blk), 0)
        ui = lax.broadcasted_iota(jnp.int32, (blk, blk), 1)
        dist = jnp.abs(ti - ui).astype(F32)
        allowed = (ui >> 6) <= (ti >> 6)
        decay_ref[...] = jnp.where(allowed, jnp.exp(lg * dist), 0.0)

    q = q_ref[0, 0]
    k = k_ref[0, 0] * jnp.asarray(RET_QK_DIM ** -0.5, BF16)
    v = v_ref[0, 0]
    t = lax.broadcasted_iota(jnp.int32, (blk, 1), 0).astype(F32)
    qd = (q.astype(F32) * jnp.exp(lg * t)).astype(BF16)
    kd = (k.astype(F32) * jnp.exp(lg * (blk - t))).astype(BF16)

    a = lax.dot_general(q, k, (((1,), (1,)), ((), ())), preferred_element_type=F32)
    a = (a * decay_ref[...]).astype(BF16)
    out = jnp.dot(a, v, preferred_element_type=F32)
    out = out + jnp.dot(qd, state_ref[...].astype(BF16), preferred_element_type=F32)
    kv = lax.dot_general(kd, v, (((0,), (0,)), ((), ())), preferred_element_type=F32)
    block_decay = jnp.exp(lg * jnp.full((1, 1), float(blk), F32))
    state_ref[...] = block_decay * state_ref[...] + kv

    ms = jnp.mean(out * out, axis=-1, keepdims=True)
    out = out * lax.rsqrt(ms + HEAD_NORM_EPS)
    g = g_ref[0].astype(F32)
    y_ref[0] = (g * jax.nn.sigmoid(g) * out).astype(BF16)


def _retention(q, k, v, g, log_gammas, blk):
    b, nh, s, dk = q.shape
    dv = v.shape[-1]
    return pl.pallas_call(
        functools.partial(_ret_kernel, blk=blk),
        grid=(b, nh, s // blk),
        in_specs=[
            pl.BlockSpec(memory_space=pltpu.SMEM),
            pl.BlockSpec((1, 1, blk, dk), lambda bi, h, i: (bi, h, i, 0)),
            pl.BlockSpec((1, 1, blk, dk), lambda bi, h, i: (bi, h, i, 0)),
            pl.BlockSpec((1, 1, blk, dv), lambda bi, h, i: (bi, h, i, 0)),
            pl.BlockSpec((1, blk, dv), lambda bi, h, i: (bi, i, h)),
        ],
        out_specs=pl.BlockSpec((1, blk, dv), lambda bi, h, i: (bi, i, h)),
        out_shape=jax.ShapeDtypeStruct((b, s, nh * dv), BF16),
        scratch_shapes=[
            pltpu.VMEM((dk, dv), F32),
            pltpu.VMEM((blk, blk), F32),
        ],
        compiler_params=_cparams(("arbitrary", "arbitrary", "arbitrary")),
        name="retention",
    )(log_gammas, q, k, v, g)


def kernel(x, c, ada_w, ada_b, pre_gain, post_gain, da_w_in, da_w_out, da_lambda_q1, da_lambda_k1,
           da_lambda_q2, da_lambda_k2, da_subln_gain, ret_w_in, ret_w_out):
    b, s, d = x.shape
    mod = _modulation(c, ada_w, ada_b)

    qk_w = 2 * DA_HEADS * DA_HEAD_DIM
    v_w = DA_HEADS * DA_V_DIM
    da_plan = ((0, DA_HEADS, 2 * DA_HEAD_DIM), (qk_w, DA_HEADS, 2 * DA_HEAD_DIM),
               (2 * qk_w, DA_HEADS, DA_V_DIM), (2 * qk_w + v_w, None, v_w))
    mod0 = mod[0].reshape(b, 1, 3 * d)
    q, k, v, g = _in_projection(x, mod0, pre_gain[0:1], da_w_in[0].astype(BF16), da_plan,
                                tm=512, slab=256)
    lam_init = 0.8 - 0.6 * math.exp(-0.3 * 0)
    slopes = jnp.asarray([2.0 ** (-8.0 * (hh + 1) / DA_HEADS) for hh in range(DA_HEADS)], F32)
    lamv = jnp.stack([da_lambda_q1[0], da_lambda_k1[0], da_lambda_q2[0], da_lambda_k2[0]])
    y = _diff_attention(q, k, v, g, slopes, lamv, da_subln_gain[0:1], lam_init, tq=256)
    x = _out_projection(y, da_w_out[0].astype(BF16), x, mod0, post_gain[0:1], tm=512)

    rqk = RET_HEADS * RET_QK_DIM
    rv = RET_HEADS * RET_V_DIM
    ret_plan = ((0, RET_HEADS, RET_QK_DIM), (rqk, RET_HEADS, RET_QK_DIM),
                (2 * rqk, RET_HEADS, RET_V_DIM), (2 * rqk + rv, None, rv))
    mod1 = mod[1].reshape(b, 1, 3 * d)
    q, k, v, g = _in_projection(x, mod1, pre_gain[1:2], ret_w_in[0].astype(BF16), ret_plan,
                                tm=256, slab=256)
    gammas = 1.0 - 2.0 ** (-5.0 - jnp.arange(RET_HEADS, dtype=F32))
    y = _retention(q, k, v, g, jnp.log(gammas), blk=256)
    return _out_projection(y, ret_w_out[0].astype(BF16), x, mod1, post_gain[1:2], tm=512)
```

```python
import functools
import math

import jax
import jax.numpy as jnp
from jax import lax
from jax.experimental import pallas as pl
from jax.experimental.pallas import tpu as pltpu

CHUNK = 64
CHUNK_SHIFT = 6
DA_HEADS = 8
DA_HEAD_DIM = 64
DA_V_DIM = 128
RET_HEADS = 4
RET_QK_DIM = 256
RET_V_DIM = 512
NORM_EPS = 1e-6
HEAD_NORM_EPS = 1e-5
MASK_VALUE = -1e30
LOG2E = 1.4426950408889634

V7X_VMEM_LIMIT_BYTES = 56 * 1024 * 1024
LANES = 128
SUBLANES = 8
BF16_EXACT_INT = 256

BF16 = jnp.bfloat16
F32 = jnp.float32


def _cparams(sem, vmem=V7X_VMEM_LIMIT_BYTES):
    return pltpu.CompilerParams(dimension_semantics=sem, vmem_limit_bytes=vmem)


def _mod_kernel(c_ref, w_ref, b_ref, o_ref):
    c = c_ref[...]
    cond = c * jax.nn.sigmoid(c)
    o_ref[0] = jnp.dot(cond, w_ref[0], preferred_element_type=F32) + b_ref[0]


def _modulation(c, ada_w, ada_b):
    depth, d, d3 = ada_w.shape
    b = c.shape[0]
    n_tiles = d3 // d
    return pl.pallas_call(
        _mod_kernel,
        grid=(depth, n_tiles),
        in_specs=[
            pl.BlockSpec((b, d), lambda l, j: (0, 0)),
            pl.BlockSpec((1, d, d), lambda l, j: (l, 0, j)),
            pl.BlockSpec((1, 1, d), lambda l, j: (l, 0, j)),
        ],
        out_specs=pl.BlockSpec((1, b, d), lambda l, j: (l, 0, j)),
        out_shape=jax.ShapeDtypeStruct((depth, b, d3), F32),
        compiler_params=_cparams(("arbitrary", "arbitrary")),
        name="modulation",
    )(c, ada_w, ada_b.reshape(depth, 1, d3))


def _inproj_kernel(x_ref, mod_ref, gain_ref, w_ref, *refs, d_model, plan, slab):
    out_refs = refs[:len(plan)]
    hb_ref = refs[len(plan)]
    x = x_ref[0]
    ms = jnp.mean(x * x, axis=-1, keepdims=True)
    hn = x * lax.rsqrt(ms + NORM_EPS) * gain_ref[...]
    shift = mod_ref[0, :, 0:d_model]
    scale = mod_ref[0, :, d_model:2 * d_model]
    hb_ref[...] = (hn * (1.0 + scale) + shift).astype(BF16)
    for (kind, c0, n_heads, width, out_scale), o_ref in zip(plan, out_refs):
        total = width if kind == "tok" else n_heads * width
        for s0 in range(0, total, slab):
            r = jnp.dot(hb_ref[...], w_ref[:, c0 + s0:c0 + s0 + slab],
                        preferred_element_type=F32)
            if out_scale != 1.0:
                r = r * out_scale
            if kind == "tok":
                o_ref[0, :, s0:s0 + slab] = r.astype(BF16)
            elif kind == "head" and width >= slab:
                hh, off = divmod(s0, width)
                o_ref[0, hh, :, off:off + slab] = r.astype(BF16)
            elif kind == "head":
                for p in range(slab // width):
                    o_ref[0, s0 // width + p] = r[:, p * width:(p + 1) * width].astype(BF16)
            else:
                for p in range(slab // width):
                    o_ref[0, s0 // width + p, 0] = r[:, p * width:(p + 1) * width].T.astype(BF16)


def _in_projection(x, mod, gain, w_bf16, plan, tm, slab):
    b, s, d = x.shape
    n = w_bf16.shape[1]
    out_shapes, out_specs = [], []
    for (kind, c0, n_heads, width, _) in plan:
        if kind == "tok":
            out_shapes.append(jax.ShapeDtypeStruct((b, s, width), BF16))
            out_specs.append(pl.BlockSpec((1, tm, width), lambda bi, i: (bi, i, 0)))
        elif kind == "head":
            out_shapes.append(jax.ShapeDtypeStruct((b, n_heads, s, width), BF16))
            out_specs.append(pl.BlockSpec((1, n_heads, tm, width), lambda bi, i: (bi, 0, i, 0)))
        else:
            assert slab % width == 0
            out_shapes.append(jax.ShapeDtypeStruct((b, n_heads, s // tm, width, tm), BF16))
            out_specs.append(pl.BlockSpec((1, n_heads, 1, width, tm),
                                          lambda bi, i: (bi, 0, i, 0, 0)))
    return pl.pallas_call(
        functools.partial(_inproj_kernel, d_model=d, plan=plan, slab=slab),
        grid=(b, s // tm),
        in_specs=[
            pl.BlockSpec((1, tm, d), lambda bi, i: (bi, i, 0)),
            pl.BlockSpec((1, 1, 3 * d), lambda bi, i: (bi, 0, 0)),
            pl.BlockSpec((1, d), lambda bi, i: (0, 0)),
            pl.BlockSpec((d, n), lambda bi, i: (0, 0)),
        ],
        out_specs=out_specs,
        out_shape=out_shapes,
        scratch_shapes=[pltpu.VMEM((tm, d), BF16)],
        compiler_params=_cparams(("arbitrary", "arbitrary")),
        name="in_projection",
    )(x, mod, gain, w_bf16)


def _outproj_kernel(y_ref, w_ref, x_ref, mod_ref, gain_ref, o_ref, *, d_model):
    t = jnp.dot(y_ref[0], w_ref[...], preferred_element_type=F32)
    ms = jnp.mean(t * t, axis=-1, keepdims=True)
    n = t * lax.rsqrt(ms + NORM_EPS) * gain_ref[...]
    gate = mod_ref[0, :, 2 * d_model:3 * d_model]
    o_ref[0] = x_ref[0] + gate * n


def _out_projection(y, w_bf16, x, mod, gain, tm):
    b, s, d = x.shape
    k = y.shape[-1]
    return pl.pallas_call(
        functools.partial(_outproj_kernel, d_model=d),
        grid=(b, s // tm),
        in_specs=[
            pl.BlockSpec((1, tm, k), lambda bi, i: (bi, i, 0)),
            pl.BlockSpec((k, d), lambda bi, i: (0, 0)),
            pl.BlockSpec((1, tm, d), lambda bi, i: (bi, i, 0)),
            pl.BlockSpec((1, 1, 3 * d), lambda bi, i: (bi, 0, 0)),
            pl.BlockSpec((1, d), lambda bi, i: (0, 0)),
        ],
        out_specs=pl.BlockSpec((1, tm, d), lambda bi, i: (bi, i, 0)),
        out_shape=jax.ShapeDtypeStruct((b, s, d), F32),
        compiler_params=_cparams(("arbitrary", "arbitrary")),
        name="out_projection",
    )(y, w_bf16, x, mod, gain)


N_ALIBI_FEATURES = 6
ATTN_ROW_CHUNK = 64


def _alibi_key_features(s):
    pos = jnp.arange(s, dtype=jnp.int32)[:, None]
    col = jnp.arange(LANES, dtype=jnp.int32)[None, :]
    lo = (pos % BF16_EXACT_INT).astype(F32)
    hi = (pos // BF16_EXACT_INT).astype(F32)
    feat = jnp.where(col < 3, lo, jnp.where(col < N_ALIBI_FEATURES, hi, 0.0))
    return feat.astype(BF16)


def _attn_kernel(slope_ref, qT_ref, k_ref, kf_ref, vT_ref, g_ref, lamv_ref, subln_ref, y_ref,
                 qs_ref, s_ref, p_ref, m_ref, l_ref, acc_ref, corr_ref, *, tq, lam_init):
    h = pl.program_id(1)
    qi = pl.program_id(2)
    tk = tq
    w = 2 * tq
    rc = ATTN_ROW_CHUNK
    n_chunks = tk // rc

    @pl.when(qi == 0)
    def _():
        slope2 = slope_ref[h] * LOG2E
        fr = lax.broadcasted_iota(jnp.int32, (LANES, w), 0)
        x = jnp.full((LANES, w), slope2, F32)
        hi = x.astype(BF16).astype(F32)
        mid = (x - hi).astype(BF16).astype(F32)
        lo = x - hi - mid
        part = jnp.where((fr == 0) | (fr == 3), hi, jnp.where((fr == 1) | (fr == 4), mid, lo))
        feat = jnp.where(fr < 3, part,
                         jnp.where(fr < N_ALIBI_FEATURES, part * BF16_EXACT_INT, 0.0))
        qs_ref[LANES:2 * LANES, :] = feat.astype(BF16)
        kk = lax.broadcasted_iota(jnp.int32, (tk, w), 0)
        qq = lax.broadcasted_iota(jnp.int32, (tk, w), 1)
        qq = jnp.where(qq >= tq, qq - tq, qq)
        ahead = (kk - qq).astype(F32)
        same_chunk = (kk >> CHUNK_SHIFT) == (qq >> CHUNK_SHIFT)
        corr_ref[...] = jnp.where(kk <= qq, 0.0,
                                  jnp.where(same_chunk, -2.0 * slope2 * ahead, MASK_VALUE))

    qT = qT_ref[0, 0, 0]
    row = lax.broadcasted_iota(jnp.int32, qT.shape, 0)
    zero = jnp.zeros_like(qT)
    qs_ref[0:LANES, 0:tq] = jnp.where(row < DA_HEAD_DIM, qT, zero)
    qs_ref[0:LANES, tq:w] = jnp.where(row >= DA_HEAD_DIM, qT, zero)

    m_ref[...] = jnp.full(m_ref.shape, MASK_VALUE, F32)
    l_ref[...] = jnp.zeros(l_ref.shape, F32)
    acc_ref[...] = jnp.zeros(acc_ref.shape, F32)

    def scores(j, slot):
        start = pl.multiple_of(j * tk, tk)
        ka = jnp.concatenate([k_ref[0, 0, pl.ds(start, tk), :], kf_ref[pl.ds(start, tk), :]],
                             axis=1)
        s_ref[slot] = jnp.dot(ka, qs_ref[...], preferred_element_type=F32)

    def process(j, slot, diag):
        def load(c):
            blk = s_ref[slot, c * rc:(c + 1) * rc, :]
            if diag:
                blk = blk + corr_ref[c * rc:(c + 1) * rc, :]
            return blk

        def fold(parts, op):
            while len(parts) > 1:
                parts = [op(parts[i], parts[i + 1]) for i in range(0, len(parts), 2)]
            return parts[0]

        def rows8(blk):
            return [blk[r * SUBLANES:(r + 1) * SUBLANES] for r in range(rc // SUBLANES)]

        mx = fold([fold(rows8(load(c)), jnp.maximum) for c in range(n_chunks)], jnp.maximum)
        m_prev = m_ref[...]
        m_next = jnp.maximum(m_prev, jnp.max(mx, axis=0, keepdims=True))
        alpha = jnp.exp2(m_prev - m_next)
        sums = []
        for c in range(n_chunks):
            p = jnp.exp2(load(c) - m_next)
            sums.append(fold(rows8(p), jnp.add))
            p_ref[c * rc:(c + 1) * rc, :] = p.astype(BF16)
        ps = fold(sums, jnp.add)
        l_ref[...] = alpha * l_ref[...] + jnp.sum(ps, axis=0, keepdims=True)
        m_ref[...] = m_next
        pv = jnp.dot(vT_ref[0, 0, j], p_ref[...], preferred_element_type=F32)
        acc_ref[...] = alpha * acc_ref[...] + pv

    scores(0, 0)

    def off_diag_pair(i, carry):
        scores(2 * i + 1, 1)
        process(2 * i, 0, False)
        scores(2 * i + 2, 0)
        process(2 * i + 1, 1, False)
        return carry

    lax.fori_loop(0, qi >> 1, off_diag_pair, 0)

    @pl.when((qi & 1) == 0)
    def _():
        process(qi, 0, True)

    @pl.when((qi & 1) == 1)
    def _():
        scores(qi, 1)
        process(qi - 1, 0, False)
        process(qi, 1, True)

    lv = lamv_ref[...]
    lam = (jnp.exp(jnp.sum(lv[0:1] * lv[1:2], axis=-1, keepdims=True))
           - jnp.exp(jnp.sum(lv[2:3] * lv[3:4], axis=-1, keepdims=True)) + lam_init)
    o = acc_ref[...] / l_ref[...]
    out = o[:, 0:tq] - lam * o[:, tq:w]
    ms = jnp.mean(out * out, axis=0, keepdims=True)
    out = out * lax.rsqrt(ms + HEAD_NORM_EPS) * subln_ref[...] * (1.0 - lam_init)
    g = g_ref[0].astype(F32)
    y_ref[0] = (g * jax.nn.sigmoid(g) * out.T).astype(BF16)


def _diff_attention(qT, k, vT, g, slopes, lamv, subln, lam_init):
    b, nh, nq, dk, tq = qT.shape
    s = k.shape[2]
    w = 2 * tq
    kernel = functools.partial(_attn_kernel, tq=tq, lam_init=lam_init)
    return pl.pallas_call(
        kernel,
        grid=(b, nh, nq),
        in_specs=[
            pl.BlockSpec(memory_space=pltpu.SMEM),
            pl.BlockSpec((1, 1, 1, dk, tq), lambda bi, h, i: (bi, h, i, 0, 0)),
            pl.BlockSpec((1, 1, s, dk), lambda bi, h, i: (bi, h, 0, 0)),
            pl.BlockSpec((s, LANES), lambda bi, h, i: (0, 0)),
            pl.BlockSpec((1, 1, nq, DA_V_DIM, tq), lambda bi, h, i: (bi, h, 0, 0, 0)),
            pl.BlockSpec((1, tq, DA_V_DIM), lambda bi, h, i: (bi, i, h)),
            pl.BlockSpec((4, DA_HEAD_DIM), lambda bi, h, i: (0, 0)),
            pl.BlockSpec((DA_V_DIM, 1), lambda bi, h, i: (0, 0)),
        ],
        out_specs=pl.BlockSpec((1, tq, DA_V_DIM), lambda bi, h, i: (bi, i, h)),
        out_shape=jax.ShapeDtypeStruct((b, s, nh * DA_V_DIM), BF16),
        scratch_shapes=[
            pltpu.VMEM((2 * LANES, w), BF16),
            pltpu.VMEM((2, tq, w), F32),
            pltpu.VMEM((tq, w), BF16),
            pltpu.VMEM((1, w), F32),
            pltpu.VMEM((1, w), F32),
            pltpu.VMEM((DA_V_DIM, w), F32),
            pltpu.VMEM((tq, w), F32),
        ],
        compiler_params=_cparams(("arbitrary", "arbitrary", "arbitrary")),
        name="diff_attention",
    )(slopes, qT, k, _alibi_key_features(s), vT, g, lamv, subln.reshape(DA_V_DIM, 1))


def _ret_kernel(lg_ref, q_ref, k_ref, v_ref, g_ref, y_ref, state_ref, decay_ref, *, blk):
    h = pl.program_id(1)
    n = pl.program_id(2)
    lg = lg_ref[h]

    @pl.when(n == 0)
    def _():
        state_ref[...] = jnp.zeros(state_ref.shape, F32)
        ti = lax.broadcasted_iota(jnp.int32, (blk, blk), 0)
        ui = lax.broadcasted_iota(jnp.int32, (blk, blk), 1)
        dist = jnp.abs(ti - ui).astype(F32)
        allowed = (ui >> CHUNK_SHIFT) <= (ti >> CHUNK_SHIFT)
        decay_ref[...] = jnp.where(allowed, jnp.exp(lg * dist), 0.0)

    q = q_ref[0, 0]
    k = k_ref[0, 0] * jnp.asarray(RET_QK_DIM ** -0.5, BF16)
    v = v_ref[0, 0]
    t = lax.broadcasted_iota(jnp.int32, (blk, 1), 0).astype(F32)
    qd = (q.astype(F32) * jnp.exp(lg * t)).astype(BF16)
    kd = (k.astype(F32) * jnp.exp(lg * (blk - t))).astype(BF16)

    a = lax.dot_general(q, k, (((1,), (1,)), ((), ())), preferred_element_type=F32)
    a = (a * decay_ref[...]).astype(BF16)
    out = jnp.dot(a, v, preferred_element_type=F32)
    out = out + jnp.dot(qd, state_ref[...].astype(BF16), preferred_element_type=F32)
    kv = lax.dot_general(kd, v, (((0,), (0,)), ((), ())), preferred_element_type=F32)
    block_decay = jnp.exp(lg * jnp.full((1, 1), float(blk), F32))
    state_ref[...] = block_decay * state_ref[...] + kv

    ms = jnp.mean(out * out, axis=-1, keepdims=True)
    out = out * lax.rsqrt(ms + HEAD_NORM_EPS)
    g = g_ref[0].astype(F32)
    y_ref[0] = (g * jax.nn.sigmoid(g) * out).astype(BF16)


def _retention(q, k, v, g, log_gammas, blk):
    b, nh, s, dk = q.shape
    dv = v.shape[-1]
    return pl.pallas_call(
        functools.partial(_ret_kernel, blk=blk),
        grid=(b, nh, s // blk),
        in_specs=[
            pl.BlockSpec(memory_space=pltpu.SMEM),
            pl.BlockSpec((1, 1, blk, dk), lambda bi, h, i: (bi, h, i, 0)),
            pl.BlockSpec((1, 1, blk, dk), lambda bi, h, i: (bi, h, i, 0)),
            pl.BlockSpec((1, 1, blk, dv), lambda bi, h, i: (bi, h, i, 0)),
            pl.BlockSpec((1, blk, dv), lambda bi, h, i: (bi, i, h)),
        ],
        out_specs=pl.BlockSpec((1, blk, dv), lambda bi, h, i: (bi, i, h)),
        out_shape=jax.ShapeDtypeStruct((b, s, nh * dv), BF16),
        scratch_shapes=[
            pltpu.VMEM((dk, dv), F32),
            pltpu.VMEM((blk, blk), F32),
        ],
        compiler_params=_cparams(("arbitrary", "arbitrary", "arbitrary")),
        name="retention",
    )(log_gammas, q, k, v, g)


def kernel(x, c, ada_w, ada_b, pre_gain, post_gain, da_w_in, da_w_out, da_lambda_q1, da_lambda_k1,
           da_lambda_q2, da_lambda_k2, da_subln_gain, ret_w_in, ret_w_out):
    b, s, d = x.shape
    mod = _modulation(c, ada_w, ada_b)

    attn_tile = 512
    qk_w = 2 * DA_HEADS * DA_HEAD_DIM
    v_w = DA_HEADS * DA_V_DIM
    q_scale = LOG2E * DA_HEAD_DIM ** -0.5
    da_plan = (("headT", 0, DA_HEADS, 2 * DA_HEAD_DIM, q_scale),
               ("head", qk_w, DA_HEADS, 2 * DA_HEAD_DIM, 1.0),
               ("headT", 2 * qk_w, DA_HEADS, DA_V_DIM, 1.0),
               ("tok", 2 * qk_w + v_w, None, v_w, 1.0))
    mod0 = mod[0].reshape(b, 1, 3 * d)
    qT, k, vT, g = _in_projection(x, mod0, pre_gain[0:1], da_w_in[0].astype(BF16), da_plan,
                                  tm=attn_tile, slab=256)
    lam_init = 0.8 - 0.6 * math.exp(-0.3 * 0)
    slopes = jnp.asarray([2.0 ** (-8.0 * (hh + 1) / DA_HEADS) for hh in range(DA_HEADS)], F32)
    lamv = jnp.stack([da_lambda_q1[0], da_lambda_k1[0], da_lambda_q2[0], da_lambda_k2[0]])
    y = _diff_attention(qT, k, vT, g, slopes, lamv, da_subln_gain[0], lam_init)
    x = _out_projection(y, da_w_out[0].astype(BF16), x, mod0, post_gain[0:1], tm=512)

    rqk = RET_HEADS * RET_QK_DIM
    rv = RET_HEADS * RET_V_DIM
    ret_plan = (("head", 0, RET_HEADS, RET_QK_DIM, 1.0), ("head", rqk, RET_HEADS, RET_QK_DIM, 1.0),
                ("head", 2 * rqk, RET_HEADS, RET_V_DIM, 1.0), ("tok", 2 * rqk + rv, None, rv, 1.0))
    mod1 = mod[1].reshape(b, 1, 3 * d)
    q, k, v, g = _in_projection(x, mod1, pre_gain[1:2], ret_w_in[0].astype(BF16), ret_plan,
                                tm=256, slab=256)
    gammas = 1.0 - 2.0 ** (-5.0 - jnp.arange(RET_HEADS, dtype=F32))
    y = _retention(q, k, v, g, jnp.log(gammas), blk=256)
    return _out_projection(y, ret_w_out[0].astype(BF16), x, mod1, post_gain[1:2], tm=512)
```

```python
import functools
import math

import jax
import jax.numpy as jnp
from jax import lax
from jax.experimental import pallas as pl
from jax.experimental.pallas import tpu as pltpu

CHUNK = 64
CHUNK_SHIFT = 6
DA_HEADS = 8
DA_HEAD_DIM = 64
DA_V_DIM = 128
RET_HEADS = 4
RET_QK_DIM = 256
RET_V_DIM = 512
NORM_EPS = 1e-6
HEAD_NORM_EPS = 1e-5
MASK_VALUE = -1e30
LOG2E = 1.4426950408889634

V7X_VMEM_LIMIT_BYTES = 56 * 1024 * 1024
LANES = 128
SUBLANES = 8
BF16_EXACT_INT = 256

BF16 = jnp.bfloat16
F32 = jnp.float32


def _cparams(sem, vmem=V7X_VMEM_LIMIT_BYTES):
    return pltpu.CompilerParams(dimension_semantics=sem, vmem_limit_bytes=vmem)


def _mod_kernel(c_ref, w_ref, b_ref, o_ref):
    c = c_ref[...]
    cond = c * jax.nn.sigmoid(c)
    o_ref[0] = jnp.dot(cond, w_ref[0], preferred_element_type=F32) + b_ref[0]


def _modulation(c, ada_w, ada_b):
    depth, d, d3 = ada_w.shape
    b = c.shape[0]
    n_tiles = d3 // d
    return pl.pallas_call(
        _mod_kernel,
        grid=(depth, n_tiles),
        in_specs=[
            pl.BlockSpec((b, d), lambda l, j: (0, 0)),
            pl.BlockSpec((1, d, d), lambda l, j: (l, 0, j)),
            pl.BlockSpec((1, 1, d), lambda l, j: (l, 0, j)),
        ],
        out_specs=pl.BlockSpec((1, b, d), lambda l, j: (l, 0, j)),
        out_shape=jax.ShapeDtypeStruct((depth, b, d3), F32),
        compiler_params=_cparams(("arbitrary", "arbitrary")),
        name="modulation",
    )(c, ada_w, ada_b.reshape(depth, 1, d3))


def _inproj_kernel(x_ref, mod_ref, gain_ref, w_ref, *refs, d_model, plan, slab):
    out_refs = refs[:len(plan)]
    hb_ref = refs[len(plan)]
    x = x_ref[0]
    ms = jnp.mean(x * x, axis=-1, keepdims=True)
    hn = x * lax.rsqrt(ms + NORM_EPS) * gain_ref[...]
    shift = mod_ref[0, :, 0:d_model]
    scale = mod_ref[0, :, d_model:2 * d_model]
    hb_ref[...] = (hn * (1.0 + scale) + shift).astype(BF16)
    for (kind, c0, n_heads, width, out_scale), o_ref in zip(plan, out_refs):
        total = width if kind == "tok" else n_heads * width
        for s0 in range(0, total, slab):
            r = jnp.dot(hb_ref[...], w_ref[:, c0 + s0:c0 + s0 + slab],
                        preferred_element_type=F32)
            if out_scale != 1.0:
                r = r * out_scale
            if kind == "tok":
                o_ref[0, :, s0:s0 + slab] = r.astype(BF16)
            elif kind == "head" and width >= slab:
                hh, off = divmod(s0, width)
                o_ref[0, hh, :, off:off + slab] = r.astype(BF16)
            elif kind == "head":
                for p in range(slab // width):
                    o_ref[0, s0 // width + p] = r[:, p * width:(p + 1) * width].astype(BF16)
            else:
                for p in range(slab // width):
                    o_ref[0, s0 // width + p, 0] = r[:, p * width:(p + 1) * width].T.astype(BF16)


def _in_projection(x, mod, gain, w_bf16, plan, tm, slab):
    b, s, d = x.shape
    n = w_bf16.shape[1]
    out_shapes, out_specs = [], []
    for (kind, c0, n_heads, width, _) in plan:
        if kind == "tok":
            out_shapes.append(jax.ShapeDtypeStruct((b, s, width), BF16))
            out_specs.append(pl.BlockSpec((1, tm, width), lambda bi, i: (bi, i, 0)))
        elif kind == "head":
            out_shapes.append(jax.ShapeDtypeStruct((b, n_heads, s, width), BF16))
            out_specs.append(pl.BlockSpec((1, n_heads, tm, width), lambda bi, i: (bi, 0, i, 0)))
        else:
            assert slab % width == 0
            out_shapes.append(jax.ShapeDtypeStruct((b, n_heads, s // tm, width, tm), BF16))
            out_specs.append(pl.BlockSpec((1, n_heads, 1, width, tm),
                                          lambda bi, i: (bi, 0, i, 0, 0)))
    return pl.pallas_call(
        functools.partial(_inproj_kernel, d_model=d, plan=plan, slab=slab),
        grid=(b, s // tm),
        in_specs=[
            pl.BlockSpec((1, tm, d), lambda bi, i: (bi, i, 0)),
            pl.BlockSpec((1, 1, 3 * d), lambda bi, i: (bi, 0, 0)),
            pl.BlockSpec((1, d), lambda bi, i: (0, 0)),
            pl.BlockSpec((d, n), lambda bi, i: (0, 0)),
        ],
        out_specs=out_specs,
        out_shape=out_shapes,
        scratch_shapes=[pltpu.VMEM((tm, d), BF16)],
        compiler_params=_cparams(("arbitrary", "arbitrary")),
        name="in_projection",
    )(x, mod, gain, w_bf16)


def _outproj_kernel(y_ref, w_ref, x_ref, mod_ref, gain_ref, o_ref, *, d_model):
    t = jnp.dot(y_ref[0], w_ref[...], preferred_element_type=F32)
    ms = jnp.mean(t * t, axis=-1, keepdims=True)
    n = t * lax.rsqrt(ms + NORM_EPS) * gain_ref[...]
    gate = mod_ref[0, :, 2 * d_model:3 * d_model]
    o_ref[0] = x_ref[0] + gate * n


def _out_projection(y, w_bf16, x, mod, gain, tm):
    b, s, d = x.shape
    k = y.shape[-1]
    return pl.pallas_call(
        functools.partial(_outproj_kernel, d_model=d),
        grid=(b, s // tm),
        in_specs=[
            pl.BlockSpec((1, tm, k), lambda bi, i: (bi, i, 0)),
            pl.BlockSpec((k, d), lambda bi, i: (0, 0)),
            pl.BlockSpec((1, tm, d), lambda bi, i: (bi, i, 0)),
            pl.BlockSpec((1, 1, 3 * d), lambda bi, i: (bi, 0, 0)),
            pl.BlockSpec((1, d), lambda bi, i: (0, 0)),
        ],
        out_specs=pl.BlockSpec((1, tm, d), lambda bi, i: (bi, i, 0)),
        out_shape=jax.ShapeDtypeStruct((b, s, d), F32),
        compiler_params=_cparams(("arbitrary", "arbitrary")),
        name="out_projection",
    )(y, w_bf16, x, mod, gain)


N_ALIBI_FEATURES = 6
ATTN_ROW_CHUNK = 16
ATTN_SUM_ROWS = 16


def _alibi_key_features(s):
    pos = jnp.arange(s, dtype=jnp.int32)[:, None]
    col = jnp.arange(LANES, dtype=jnp.int32)[None, :]
    lo = (pos % BF16_EXACT_INT).astype(F32)
    hi = (pos // BF16_EXACT_INT).astype(F32)
    feat = jnp.where(col < 3, lo, jnp.where(col < N_ALIBI_FEATURES, hi, 0.0))
    return feat.astype(BF16)


def _attn_kernel(slope_ref, qT_ref, k_ref, kf_ref, vT_ref, g_ref, lamv_ref, subln_ref, y_ref,
                 qs_ref, s_ref, mc_ref, p_ref, m_ref, acc_ref, corr_ref, *, tq, lam_init):
    h = pl.program_id(1)
    qi = pl.program_id(2)
    tk = tq
    w = 2 * tq
    rc = ATTN_ROW_CHUNK
    n_chunks = tk // rc

    @pl.when(qi == 0)
    def _():
        slope2 = slope_ref[h] * LOG2E
        fr = lax.broadcasted_iota(jnp.int32, (LANES, w), 0)
        x = jnp.full((LANES, w), slope2, F32)
        hi = x.astype(BF16).astype(F32)
        mid = (x - hi).astype(BF16).astype(F32)
        lo = x - hi - mid
        part = jnp.where((fr == 0) | (fr == 3), hi, jnp.where((fr == 1) | (fr == 4), mid, lo))
        feat = jnp.where(fr < 3, part,
                         jnp.where(fr < N_ALIBI_FEATURES, part * BF16_EXACT_INT, 0.0))
        qs_ref[LANES:2 * LANES, :] = feat.astype(BF16)
        kk = lax.broadcasted_iota(jnp.int32, (tk, w), 0)
        qq = lax.broadcasted_iota(jnp.int32, (tk, w), 1)
        qq = jnp.where(qq >= tq, qq - tq, qq)
        ahead = (kk - qq).astype(F32)
        same_chunk = (kk >> CHUNK_SHIFT) == (qq >> CHUNK_SHIFT)
        corr_ref[...] = jnp.where(kk <= qq, 0.0,
                                  jnp.where(same_chunk, -2.0 * slope2 * ahead, MASK_VALUE))

    qT = qT_ref[0, 0, 0]
    row = lax.broadcasted_iota(jnp.int32, qT.shape, 0)
    zero = jnp.zeros_like(qT)
    qs_ref[0:LANES, 0:tq] = jnp.where(row < DA_HEAD_DIM, qT, zero)
    qs_ref[0:LANES, tq:w] = jnp.where(row >= DA_HEAD_DIM, qT, zero)

    m_ref[...] = jnp.full(m_ref.shape, MASK_VALUE, F32)
    acc_ref[...] = jnp.zeros(acc_ref.shape, F32)
    ones_rows = jnp.ones((ATTN_SUM_ROWS, tk), BF16)

    def scores(j, slot, diag):
        start = pl.multiple_of(j * tk, tk)
        ka = jnp.concatenate([k_ref[0, 0, pl.ds(start, tk), :], kf_ref[pl.ds(start, tk), :]],
                             axis=1)
        s = jnp.dot(ka, qs_ref[...], preferred_element_type=F32)
        if diag:
            s = s + corr_ref[...]
        s_ref[slot] = s
        mc_ref[slot] = jnp.max(s, axis=0, keepdims=True)

    def process(j, slot):
        m_prev = m_ref[...]
        m_next = jnp.maximum(m_prev, mc_ref[slot])
        alpha = jnp.exp2(m_prev - m_next)
        m_rows = jnp.broadcast_to(m_next, (rc, w))
        for c in range(n_chunks):
            p = jnp.exp2(s_ref[slot, c * rc:(c + 1) * rc, :] - m_rows)
            p_ref[c * rc:(c + 1) * rc, :] = p.astype(BF16)
        m_ref[...] = m_next
        v_aug = jnp.concatenate([vT_ref[0, 0, j], ones_rows], axis=0)
        pv = jnp.dot(v_aug, p_ref[...], preferred_element_type=F32)
        acc_ref[...] = alpha * acc_ref[...] + pv

    def block_at(t):
        return jnp.where(t == 0, qi, t - 1)

    scores(qi, 0, True)

    def position_pair(i, carry):
        scores(2 * i, 1, False)
        process(block_at(2 * i), 0)
        scores(2 * i + 1, 0, False)
        process(2 * i, 1)
        return carry

    lax.fori_loop(0, qi >> 1, position_pair, 0)

    @pl.when((qi & 1) == 0)
    def _():
        process(block_at(qi), 0)

    @pl.when((qi & 1) == 1)
    def _():
        scores(qi - 1, 1, False)
        process(block_at(qi - 1), 0)
        process(qi - 1, 1)

    lv = lamv_ref[...]
    lam = (jnp.exp(jnp.sum(lv[0:1] * lv[1:2], axis=-1, keepdims=True))
           - jnp.exp(jnp.sum(lv[2:3] * lv[3:4], axis=-1, keepdims=True)) + lam_init)
    o = acc_ref[0:DA_V_DIM, :] / acc_ref[DA_V_DIM:DA_V_DIM + 1, :]
    out = o[:, 0:tq] - lam * o[:, tq:w]
    ms = jnp.mean(out * out, axis=0, keepdims=True)
    out = out * lax.rsqrt(ms + HEAD_NORM_EPS) * subln_ref[...] * (1.0 - lam_init)
    g = g_ref[0].astype(F32)
    y_ref[0] = (g * jax.nn.sigmoid(g) * out.T).astype(BF16)


def _diff_attention(qT, k, vT, g, slopes, lamv, subln, lam_init):
    b, nh, nq, dk, tq = qT.shape
    s = k.shape[2]
    w = 2 * tq
    kernel = functools.partial(_attn_kernel, tq=tq, lam_init=lam_init)
    return pl.pallas_call(
        kernel,
        grid=(b, nh, nq),
        in_specs=[
            pl.BlockSpec(memory_space=pltpu.SMEM),
            pl.BlockSpec((1, 1, 1, dk, tq), lambda bi, h, i: (bi, h, i, 0, 0)),
            pl.BlockSpec((1, 1, s, dk), lambda bi, h, i: (bi, h, 0, 0)),
            pl.BlockSpec((s, LANES), lambda bi, h, i: (0, 0)),
            pl.BlockSpec((1, 1, nq, DA_V_DIM, tq), lambda bi, h, i: (bi, h, 0, 0, 0)),
            pl.BlockSpec((1, tq, DA_V_DIM), lambda bi, h, i: (bi, i, h)),
            pl.BlockSpec((4, DA_HEAD_DIM), lambda bi, h, i: (0, 0)),
            pl.BlockSpec((DA_V_DIM, 1), lambda bi, h, i: (0, 0)),
        ],
        out_specs=pl.BlockSpec((1, tq, DA_V_DIM), lambda bi, h, i: (bi, i, h)),
        out_shape=jax.ShapeDtypeStruct((b, s, nh * DA_V_DIM), BF16),
        scratch_shapes=[
            pltpu.VMEM((2 * LANES, w), BF16),
            pltpu.VMEM((2, tq, w), F32),
            pltpu.VMEM((2, 1, w), F32),
            pltpu.VMEM((tq, w), BF16),
            pltpu.VMEM((1, w), F32),
            pltpu.VMEM((DA_V_DIM + ATTN_SUM_ROWS, w), F32),
            pltpu.VMEM((tq, w), F32),
        ],
        compiler_params=_cparams(("arbitrary", "arbitrary", "arbitrary")),
        name="diff_attention",
    )(slopes, qT, k, _alibi_key_features(s), vT, g, lamv, subln.reshape(DA_V_DIM, 1))


def _ret_kernel(lg_ref, q_ref, k_ref, v_ref, g_ref, y_ref, state_ref, decay_ref, *, blk):
    h = pl.program_id(1)
    n = pl.program_id(2)
    lg = lg_ref[h]

    @pl.when(n == 0)
    def _():
        state_ref[...] = jnp.zeros(state_ref.shape, F32)
        ti = lax.broadcasted_iota(jnp.int32, (blk, blk), 0)
        ui = lax.broadcasted_iota(jnp.int32, (blk, blk), 1)
        dist = jnp.abs(ti - ui).astype(F32)
        allowed = (ui >> CHUNK_SHIFT) <= (ti >> CHUNK_SHIFT)
        decay_ref[...] = jnp.where(allowed, jnp.exp(lg * dist), 0.0)

    q = q_ref[0, 0]
    k = k_ref[0, 0] * jnp.asarray(RET_QK_DIM ** -0.5, BF16)
    v = v_ref[0, 0]
    t = lax.broadcasted_iota(jnp.int32, (blk, 1), 0).astype(F32)
    qd = (q.astype(F32) * jnp.exp(lg * t)).astype(BF16)
    kd = (k.astype(F32) * jnp.exp(lg * (blk - t))).astype(BF16)

    a = lax.dot_general(q, k, (((1,), (1,)), ((), ())), preferred_element_type=F32)
    a = (a * decay_ref[...]).astype(BF16)
    out = jnp.dot(a, v, preferred_element_type=F32)
    out = out + jnp.dot(qd, state_ref[...].astype(BF16), preferred_element_type=F32)
    kv = lax.dot_general(kd, v, (((0,), (0,)), ((), ())), preferred_element_type=F32)
    block_decay = jnp.exp(lg * jnp.full((1, 1), float(blk), F32))
    state_ref[...] = block_decay * state_ref[...] + kv

    ms = jnp.mean(out * out, axis=-1, keepdims=True)
    out = out * lax.rsqrt(ms + HEAD_NORM_EPS)
    g = g_ref[0].astype(F32)
    y_ref[0] = (g * jax.nn.sigmoid(g) * out).astype(BF16)


def _retention(q, k, v, g, log_gammas, blk):
    b, nh, s, dk = q.shape
    dv = v.shape[-1]
    return pl.pallas_call(
        functools.partial(_ret_kernel, blk=blk),
        grid=(b, nh, s // blk),
        in_specs=[
            pl.BlockSpec(memory_space=pltpu.SMEM),
            pl.BlockSpec((1, 1, blk, dk), lambda bi, h, i: (bi, h, i, 0)),
            pl.BlockSpec((1, 1, blk, dk), lambda bi, h, i: (bi, h, i, 0)),
            pl.BlockSpec((1, 1, blk, dv), lambda bi, h, i: (bi, h, i, 0)),
            pl.BlockSpec((1, blk, dv), lambda bi, h, i: (bi, i, h)),
        ],
        out_specs=pl.BlockSpec((1, blk, dv), lambda bi, h, i: (bi, i, h)),
        out_shape=jax.ShapeDtypeStruct((b, s, nh * dv), BF16),
        scratch_shapes=[
            pltpu.VMEM((dk, dv), F32),
            pltpu.VMEM((blk, blk), F32),
        ],
        compiler_params=_cparams(("arbitrary", "arbitrary", "arbitrary")),
        name="retention",
    )(log_gammas, q, k, v, g)


def kernel(x, c, ada_w, ada_b, pre_gain, post_gain, da_w_in, da_w_out, da_lambda_q1, da_lambda_k1,
           da_lambda_q2, da_lambda_k2, da_subln_gain, ret_w_in, ret_w_out):
    b, s, d = x.shape
    mod = _modulation(c, ada_w, ada_b)

    attn_tile = 512
    qk_w = 2 * DA_HEADS * DA_HEAD_DIM
    v_w = DA_HEADS * DA_V_DIM
    q_scale = LOG2E * DA_HEAD_DIM ** -0.5
    da_plan = (("headT", 0, DA_HEADS, 2 * DA_HEAD_DIM, q_scale),
               ("head", qk_w, DA_HEADS, 2 * DA_HEAD_DIM, 1.0),
               ("headT", 2 * qk_w, DA_HEADS, DA_V_DIM, 1.0),
               ("tok", 2 * qk_w + v_w, None, v_w, 1.0))
    mod0 = mod[0].reshape(b, 1, 3 * d)
    qT, k, vT, g = _in_projection(x, mod0, pre_gain[0:1], da_w_in[0].astype(BF16), da_plan,
                                  tm=attn_tile, slab=256)
    lam_init = 0.8 - 0.6 * math.exp(-0.3 * 0)
    slopes = jnp.asarray([2.0 ** (-8.0 * (hh + 1) / DA_HEADS) for hh in range(DA_HEADS)], F32)
    lamv = jnp.stack([da_lambda_q1[0], da_lambda_k1[0], da_lambda_q2[0], da_lambda_k2[0]])
    y = _diff_attention(qT, k, vT, g, slopes, lamv, da_subln_gain[0], lam_init)
    x = _out_projection(y, da_w_out[0].astype(BF16), x, mod0, post_gain[0:1], tm=512)

    rqk = RET_HEADS * RET_QK_DIM
    rv = RET_HEADS * RET_V_DIM
    ret_plan = (("head", 0, RET_HEADS, RET_QK_DIM, 1.0), ("head", rqk, RET_HEADS, RET_QK_DIM, 1.0),
                ("head", 2 * rqk, RET_HEADS, RET_V_DIM, 1.0), ("tok", 2 * rqk + rv, None, rv, 1.0))
    mod1 = mod[1].reshape(b, 1, 3 * d)
    q, k, v, g = _in_projection(x, mod1, pre_gain[1:2], ret_w_in[0].astype(BF16), ret_plan,
                                tm=256, slab=256)
    gammas = 1.0 - 2.0 ** (-5.0 - jnp.arange(RET_HEADS, dtype=F32))
    y = _retention(q, k, v, g, jnp.log(gammas), blk=256)
    return _out_projection(y, ret_w_out[0].astype(BF16), x, mod1, post_gain[1:2], tm=512)
```

```python
import functools
import math

import jax
import jax.numpy as jnp
from jax import lax
from jax.experimental import pallas as pl
from jax.experimental.pallas import tpu as pltpu

CHUNK = 64
CHUNK_SHIFT = 6
DA_HEADS = 8
DA_HEAD_DIM = 64
DA_V_DIM = 128
RET_HEADS = 4
RET_QK_DIM = 256
RET_V_DIM = 512
NORM_EPS = 1e-6
HEAD_NORM_EPS = 1e-5
MASK_VALUE = -1e30
LOG2E = 1.4426950408889634

V7X_VMEM_LIMIT_BYTES = 56 * 1024 * 1024
LANES = 128
SUBLANES = 8
BF16_EXACT_INT = 256

BF16 = jnp.bfloat16
F32 = jnp.float32


def _cparams(sem, vmem=V7X_VMEM_LIMIT_BYTES, flags=None):
    return pltpu.CompilerParams(dimension_semantics=sem, vmem_limit_bytes=vmem, flags=flags)


def _mod_kernel(c_ref, w_ref, b_ref, o_ref):
    c = c_ref[...]
    cond = c * jax.nn.sigmoid(c)
    o_ref[0] = jnp.dot(cond, w_ref[0], preferred_element_type=F32) + b_ref[0]


def _modulation(c, ada_w, ada_b):
    depth, d, d3 = ada_w.shape
    b = c.shape[0]
    n_tiles = d3 // d
    return pl.pallas_call(
        _mod_kernel,
        grid=(depth, n_tiles),
        in_specs=[
            pl.BlockSpec((b, d), lambda l, j: (0, 0)),
            pl.BlockSpec((1, d, d), lambda l, j: (l, 0, j)),
            pl.BlockSpec((1, 1, d), lambda l, j: (l, 0, j)),
        ],
        out_specs=pl.BlockSpec((1, b, d), lambda l, j: (l, 0, j)),
        out_shape=jax.ShapeDtypeStruct((depth, b, d3), F32),
        compiler_params=_cparams(("arbitrary", "arbitrary")),
        name="modulation",
    )(c, ada_w, ada_b.reshape(depth, 1, d3))


def _inproj_kernel(x_ref, mod_ref, gain_ref, w_ref, *refs, d_model, plan, slab):
    out_refs = refs[:len(plan)]
    hb_ref = refs[len(plan)]
    x = x_ref[0]
    ms = jnp.mean(x * x, axis=-1, keepdims=True)
    hn = x * lax.rsqrt(ms + NORM_EPS) * gain_ref[...]
    shift = mod_ref[0, :, 0:d_model]
    scale = mod_ref[0, :, d_model:2 * d_model]
    hb_ref[...] = (hn * (1.0 + scale) + shift).astype(BF16)
    for (kind, c0, n_heads, width, out_scale), o_ref in zip(plan, out_refs):
        total = width if kind == "tok" else n_heads * width
        for s0 in range(0, total, slab):
            r = jnp.dot(hb_ref[...], w_ref[:, c0 + s0:c0 + s0 + slab],
                        preferred_element_type=F32)
            if out_scale != 1.0:
                r = r * out_scale
            if kind == "tok":
                o_ref[0, :, s0:s0 + slab] = r.astype(BF16)
            elif kind == "head" and width >= slab:
                hh, off = divmod(s0, width)
                o_ref[0, hh, :, off:off + slab] = r.astype(BF16)
            elif kind == "head":
                for p in range(slab // width):
                    o_ref[0, s0 // width + p] = r[:, p * width:(p + 1) * width].astype(BF16)
            else:
                for p in range(slab // width):
                    o_ref[0, s0 // width + p, 0] = r[:, p * width:(p + 1) * width].T.astype(BF16)


def _in_projection(x, mod, gain, w_bf16, plan, tm, slab):
    b, s, d = x.shape
    n = w_bf16.shape[1]
    out_shapes, out_specs = [], []
    for (kind, c0, n_heads, width, _) in plan:
        if kind == "tok":
            out_shapes.append(jax.ShapeDtypeStruct((b, s, width), BF16))
            out_specs.append(pl.BlockSpec((1, tm, width), lambda bi, i: (bi, i, 0)))
        elif kind == "head":
            out_shapes.append(jax.ShapeDtypeStruct((b, n_heads, s, width), BF16))
            out_specs.append(pl.BlockSpec((1, n_heads, tm, width), lambda bi, i: (bi, 0, i, 0)))
        else:
            assert slab % width == 0
            out_shapes.append(jax.ShapeDtypeStruct((b, n_heads, s // tm, width, tm), BF16))
            out_specs.append(pl.BlockSpec((1, n_heads, 1, width, tm),
                                          lambda bi, i: (bi, 0, i, 0, 0)))
    return pl.pallas_call(
        functools.partial(_inproj_kernel, d_model=d, plan=plan, slab=slab),
        grid=(b, s // tm),
        in_specs=[
            pl.BlockSpec((1, tm, d), lambda bi, i: (bi, i, 0)),
            pl.BlockSpec((1, 1, 3 * d), lambda bi, i: (bi, 0, 0)),
            pl.BlockSpec((1, d), lambda bi, i: (0, 0)),
            pl.BlockSpec((d, n), lambda bi, i: (0, 0)),
        ],
        out_specs=out_specs,
        out_shape=out_shapes,
        scratch_shapes=[pltpu.VMEM((tm, d), BF16)],
        compiler_params=_cparams(("arbitrary", "arbitrary")),
        name="in_projection",
    )(x, mod, gain, w_bf16)


def _outproj_kernel(y_ref, w_ref, x_ref, mod_ref, gain_ref, o_ref, *, d_model):
    t = jnp.dot(y_ref[0], w_ref[...], preferred_element_type=F32)
    ms = jnp.mean(t * t, axis=-1, keepdims=True)
    n = t * lax.rsqrt(ms + NORM_EPS) * gain_ref[...]
    gate = mod_ref[0, :, 2 * d_model:3 * d_model]
    o_ref[0] = x_ref[0] + gate * n


def _out_projection(y, w_bf16, x, mod, gain, tm):
    b, s, d = x.shape
    k = y.shape[-1]
    return pl.pallas_call(
        functools.partial(_outproj_kernel, d_model=d),
        grid=(b, s // tm),
        in_specs=[
            pl.BlockSpec((1, tm, k), lambda bi, i: (bi, i, 0)),
            pl.BlockSpec((k, d), lambda bi, i: (0, 0)),
            pl.BlockSpec((1, tm, d), lambda bi, i: (bi, i, 0)),
            pl.BlockSpec((1, 1, 3 * d), lambda bi, i: (bi, 0, 0)),
            pl.BlockSpec((1, d), lambda bi, i: (0, 0)),
        ],
        out_specs=pl.BlockSpec((1, tm, d), lambda bi, i: (bi, i, 0)),
        out_shape=jax.ShapeDtypeStruct((b, s, d), F32),
        compiler_params=_cparams(("arbitrary", "arbitrary")),
        name="out_projection",
    )(y, w_bf16, x, mod, gain)


N_ALIBI_FEATURES = 6
ATTN_ROW_CHUNK = 16
ATTN_SUM_ROWS = 16


def _alibi_key_features(s):
    pos = jnp.arange(s, dtype=jnp.int32)[:, None]
    col = jnp.arange(LANES, dtype=jnp.int32)[None, :]
    lo = (pos % BF16_EXACT_INT).astype(F32)
    hi = (pos // BF16_EXACT_INT).astype(F32)
    feat = jnp.where(col < 3, lo, jnp.where(col < N_ALIBI_FEATURES, hi, 0.0))
    return feat.astype(BF16)


def _attn_kernel(slope_ref, qT_ref, k_ref, kf_ref, vT_ref, g_ref, lamv_ref, subln_ref, y_ref,
                 qs_ref, s0_ref, s1_ref, mc0_ref, mc1_ref, p0_ref, p1_ref, al0_ref, al1_ref,
                 m_ref, acc_ref, corr_ref, *, tq, lam_init):
    h = pl.program_id(1)
    qi = pl.program_id(2)
    tk = tq
    w = 2 * tq
    rc = ATTN_ROW_CHUNK
    n_chunks = tk // rc
    s_refs, mc_refs = (s0_ref, s1_ref), (mc0_ref, mc1_ref)
    p_refs, al_refs = (p0_ref, p1_ref), (al0_ref, al1_ref)

    @pl.when(qi == 0)
    def _():
        slope2 = slope_ref[h] * LOG2E
        fr = lax.broadcasted_iota(jnp.int32, (LANES, w), 0)
        x = jnp.full((LANES, w), slope2, F32)
        hi = x.astype(BF16).astype(F32)
        mid = (x - hi).astype(BF16).astype(F32)
        lo = x - hi - mid
        part = jnp.where((fr == 0) | (fr == 3), hi, jnp.where((fr == 1) | (fr == 4), mid, lo))
        feat = jnp.where(fr < 3, part,
                         jnp.where(fr < N_ALIBI_FEATURES, part * BF16_EXACT_INT, 0.0))
        qs_ref[LANES:2 * LANES, :] = feat.astype(BF16)
        kk = lax.broadcasted_iota(jnp.int32, (tk, w), 0)
        qq = lax.broadcasted_iota(jnp.int32, (tk, w), 1)
        qq = jnp.where(qq >= tq, qq - tq, qq)
        ahead = (kk - qq).astype(F32)
        same_chunk = (kk >> CHUNK_SHIFT) == (qq >> CHUNK_SHIFT)
        corr_ref[...] = jnp.where(kk <= qq, 0.0,
                                  jnp.where(same_chunk, -2.0 * slope2 * ahead, MASK_VALUE))

    qT = qT_ref[0, 0, 0]
    row = lax.broadcasted_iota(jnp.int32, qT.shape, 0)
    zero = jnp.zeros_like(qT)
    qs_ref[0:LANES, 0:tq] = jnp.where(row < DA_HEAD_DIM, qT, zero)
    qs_ref[0:LANES, tq:w] = jnp.where(row >= DA_HEAD_DIM, qT, zero)

    m_ref[...] = jnp.full(m_ref.shape, MASK_VALUE, F32)
    acc_ref[...] = jnp.zeros(acc_ref.shape, F32)
    ones_rows = jnp.ones((ATTN_SUM_ROWS, tk), BF16)

    def qk(j, slot, diag):
        start = pl.multiple_of(j * tk, tk)
        ka = jnp.concatenate([k_ref[0, 0, pl.ds(start, tk), :], kf_ref[pl.ds(start, tk), :]],
                             axis=1)
        s = jnp.dot(ka, qs_ref[...], preferred_element_type=F32)
        if diag:
            s = s + corr_ref[...]
        s_refs[slot][...] = s
        mc_refs[slot][...] = jnp.max(s, axis=0, keepdims=True)

    def sm(slot):
        m_prev = m_ref[...]
        m_next = jnp.maximum(m_prev, mc_refs[slot][...])
        al_refs[slot][...] = jnp.exp2(m_prev - m_next)
        m_ref[...] = m_next
        m_rows = jnp.broadcast_to(m_next, (rc, w))
        for c in range(n_chunks):
            p = jnp.exp2(s_refs[slot][c * rc:(c + 1) * rc, :] - m_rows)
            p_refs[slot][c * rc:(c + 1) * rc, :] = p.astype(BF16)

    def pv(j, slot):
        v_aug = jnp.concatenate([vT_ref[0, 0, j], ones_rows], axis=0)
        upd = jnp.dot(v_aug, p_refs[slot][...], preferred_element_type=F32)
        acc_ref[...] = al_refs[slot][...] * acc_ref[...] + upd

    def block_at(t):
        return jnp.where(t == 0, qi, t - 1)

    qk(qi, 0, True)

    @pl.when(qi == 0)
    def _():
        sm(0)
        pv(qi, 0)

    @pl.when(qi >= 1)
    def _():
        qk(0, 1, False)
        sm(0)

        def time_step_pair(i, carry):
            qk(2 * i + 1, 0, False)
            sm(1)
            pv(block_at(2 * i), 0)
            qk(2 * i + 2, 1, False)
            sm(0)
            pv(2 * i, 1)
            return carry

        lax.fori_loop(0, (qi - 1) >> 1, time_step_pair, 0)

        @pl.when((qi & 1) == 1)
        def _():
            sm(1)
            pv(block_at(qi - 1), 0)
            pv(qi - 1, 1)

        @pl.when((qi & 1) == 0)
        def _():
            qk(qi - 1, 0, False)
            sm(1)
            pv(block_at(qi - 2), 0)
            sm(0)
            pv(qi - 2, 1)
            pv(qi - 1, 0)

    lv = lamv_ref[...]
    lam = (jnp.exp(jnp.sum(lv[0:1] * lv[1:2], axis=-1, keepdims=True))
           - jnp.exp(jnp.sum(lv[2:3] * lv[3:4], axis=-1, keepdims=True)) + lam_init)
    o = acc_ref[0:DA_V_DIM, :] / acc_ref[DA_V_DIM:DA_V_DIM + 1, :]
    out = o[:, 0:tq] - lam * o[:, tq:w]
    ms = jnp.mean(out * out, axis=0, keepdims=True)
    out = out * lax.rsqrt(ms + HEAD_NORM_EPS) * subln_ref[...] * (1.0 - lam_init)
    g = g_ref[0].astype(F32)
    y_ref[0] = (g * jax.nn.sigmoid(g) * out.T).astype(BF16)


def _diff_attention(qT, k, vT, g, slopes, lamv, subln, lam_init):
    b, nh, nq, dk, tq = qT.shape
    s = k.shape[2]
    w = 2 * tq
    kernel = functools.partial(_attn_kernel, tq=tq, lam_init=lam_init)
    return pl.pallas_call(
        kernel,
        grid=(b, nh, nq),
        in_specs=[
            pl.BlockSpec(memory_space=pltpu.SMEM),
            pl.BlockSpec((1, 1, 1, dk, tq), lambda bi, h, i: (bi, h, i, 0, 0)),
            pl.BlockSpec((1, 1, s, dk), lambda bi, h, i: (bi, h, 0, 0)),
            pl.BlockSpec((s, LANES), lambda bi, h, i: (0, 0)),
            pl.BlockSpec((1, 1, nq, DA_V_DIM, tq), lambda bi, h, i: (bi, h, 0, 0, 0)),
            pl.BlockSpec((1, tq, DA_V_DIM), lambda bi, h, i: (bi, i, h)),
            pl.BlockSpec((4, DA_HEAD_DIM), lambda bi, h, i: (0, 0)),
            pl.BlockSpec((DA_V_DIM, 1), lambda bi, h, i: (0, 0)),
        ],
        out_specs=pl.BlockSpec((1, tq, DA_V_DIM), lambda bi, h, i: (bi, i, h)),
        out_shape=jax.ShapeDtypeStruct((b, s, nh * DA_V_DIM), BF16),
        scratch_shapes=[
            pltpu.VMEM((2 * LANES, w), BF16),
            pltpu.VMEM((tq, w), F32),
            pltpu.VMEM((tq, w), F32),
            pltpu.VMEM((1, w), F32),
            pltpu.VMEM((1, w), F32),
            pltpu.VMEM((tq, w), BF16),
            pltpu.VMEM((tq, w), BF16),
            pltpu.VMEM((1, w), F32),
            pltpu.VMEM((1, w), F32),
            pltpu.VMEM((1, w), F32),
            pltpu.VMEM((DA_V_DIM + ATTN_SUM_ROWS, w), F32),
            pltpu.VMEM((tq, w), F32),
        ],
        compiler_params=_cparams(("arbitrary", "arbitrary", "arbitrary")),
        name="diff_attention",
    )(slopes, qT, k, _alibi_key_features(s), vT, g, lamv, subln.reshape(DA_V_DIM, 1))


def _ret_kernel(lg_ref, q_ref, k_ref, v_ref, g_ref, y_ref, state_ref, decay_ref, *, blk):
    h = pl.program_id(1)
    n = pl.program_id(2)
    lg = lg_ref[h]

    @pl.when(n == 0)
    def _():
        state_ref[...] = jnp.zeros(state_ref.shape, F32)
        ti = lax.broadcasted_iota(jnp.int32, (blk, blk), 0)
        ui = lax.broadcasted_iota(jnp.int32, (blk, blk), 1)
        dist = jnp.abs(ti - ui).astype(F32)
        allowed = (ui >> CHUNK_SHIFT) <= (ti >> CHUNK_SHIFT)
        decay_ref[...] = jnp.where(allowed, jnp.exp(lg * dist), 0.0)

    q = q_ref[0, 0]
    k = k_ref[0, 0] * jnp.asarray(RET_QK_DIM ** -0.5, BF16)
    v = v_ref[0, 0]
    t = lax.broadcasted_iota(jnp.int32, (blk, 1), 0).astype(F32)
    qd = (q.astype(F32) * jnp.exp(lg * t)).astype(BF16)
    kd = (k.astype(F32) * jnp.exp(lg * (blk - t))).astype(BF16)

    a = lax.dot_general(q, k, (((1,), (1,)), ((), ())), preferred_element_type=F32)
    a = (a * decay_ref[...]).astype(BF16)
    out = jnp.dot(a, v, preferred_element_type=F32)
    out = out + jnp.dot(qd, state_ref[...].astype(BF16), preferred_element_type=F32)
    kv = lax.dot_general(kd, v, (((0,), (0,)), ((), ())), preferred_element_type=F32)
    block_decay = jnp.exp(lg * jnp.full((1, 1), float(blk), F32))
    state_ref[...] = block_decay * state_ref[...] + kv

    ms = jnp.mean(out * out, axis=-1, keepdims=True)
    out = out * lax.rsqrt(ms + HEAD_NORM_EPS)
    g = g_ref[0].astype(F32)
    y_ref[0] = (g * jax.nn.sigmoid(g) * out).astype(BF16)


def _retention(q, k, v, g, log_gammas, blk):
    b, nh, s, dk = q.shape
    dv = v.shape[-1]
    return pl.pallas_call(
        functools.partial(_ret_kernel, blk=blk),
        grid=(b, nh, s // blk),
        in_specs=[
            pl.BlockSpec(memory_space=pltpu.SMEM),
            pl.BlockSpec((1, 1, blk, dk), lambda bi, h, i: (bi, h, i, 0)),
            pl.BlockSpec((1, 1, blk, dk), lambda bi, h, i: (bi, h, i, 0)),
            pl.BlockSpec((1, 1, blk, dv), lambda bi, h, i: (bi, h, i, 0)),
            pl.BlockSpec((1, blk, dv), lambda bi, h, i: (bi, i, h)),
        ],
        out_specs=pl.BlockSpec((1, blk, dv), lambda bi, h, i: (bi, i, h)),
        out_shape=jax.ShapeDtypeStruct((b, s, nh * dv), BF16),
        scratch_shapes=[
            pltpu.VMEM((dk, dv), F32),
            pltpu.VMEM((blk, blk), F32),
        ],
        compiler_params=_cparams(("arbitrary", "arbitrary", "arbitrary")),
        name="retention",
    )(log_gammas, q, k, v, g)


def kernel(x, c, ada_w, ada_b, pre_gain, post_gain, da_w_in, da_w_out, da_lambda_q1, da_lambda_k1,
           da_lambda_q2, da_lambda_k2, da_subln_gain, ret_w_in, ret_w_out):
    b, s, d = x.shape
    mod = _modulation(c, ada_w, ada_b)

    attn_tile = 512
    qk_w = 2 * DA_HEADS * DA_HEAD_DIM
    v_w = DA_HEADS * DA_V_DIM
    q_scale = LOG2E * DA_HEAD_DIM ** -0.5
    da_plan = (("headT", 0, DA_HEADS, 2 * DA_HEAD_DIM, q_scale),
               ("head", qk_w, DA_HEADS, 2 * DA_HEAD_DIM, 1.0),
               ("headT", 2 * qk_w, DA_HEADS, DA_V_DIM, 1.0),
               ("tok", 2 * qk_w + v_w, None, v_w, 1.0))
    mod0 = mod[0].reshape(b, 1, 3 * d)
    qT, k, vT, g = _in_projection(x, mod0, pre_gain[0:1], da_w_in[0].astype(BF16), da_plan,
                                  tm=attn_tile, slab=256)
    lam_init = 0.8 - 0.6 * math.exp(-0.3 * 0)
    slopes = jnp.asarray([2.0 ** (-8.0 * (hh + 1) / DA_HEADS) for hh in range(DA_HEADS)], F32)
    lamv = jnp.stack([da_lambda_q1[0], da_lambda_k1[0], da_lambda_q2[0], da_lambda_k2[0]])
    y = _diff_attention(qT, k, vT, g, slopes, lamv, da_subln_gain[0], lam_init)
    x = _out_projection(y, da_w_out[0].astype(BF16), x, mod0, post_gain[0:1], tm=512)

    rqk = RET_HEADS * RET_QK_DIM
    rv = RET_HEADS * RET_V_DIM
    ret_plan = (("head", 0, RET_HEADS, RET_QK_DIM, 1.0), ("head", rqk, RET_HEADS, RET_QK_DIM, 1.0),
                ("head", 2 * rqk, RET_HEADS, RET_V_DIM, 1.0), ("tok", 2 * rqk + rv, None, rv, 1.0))
    mod1 = mod[1].reshape(b, 1, 3 * d)
    q, k, v, g = _in_projection(x, mod1, pre_gain[1:2], ret_w_in[0].astype(BF16), ret_plan,
                                tm=256, slab=256)
    gammas = 1.0 - 2.0 ** (-5.0 - jnp.arange(RET_HEADS, dtype=F32))
    y = _retention(q, k, v, g, jnp.log(gammas), blk=256)
    return _out_projection(y, ret_w_out[0].astype(BF16), x, mod1, post_gain[1:2], tm=512)
```

```python
import functools
import math

import jax
import jax.numpy as jnp
import numpy as np
from jax import lax
from jax.experimental import pallas as pl
from jax.experimental.pallas import tpu as pltpu

CHUNK = 64
CHUNK_SHIFT = 6
DA_HEADS = 8
DA_HEAD_DIM = 64
DA_V_DIM = 128
RET_HEADS = 4
RET_QK_DIM = 256
RET_V_DIM = 512
NORM_EPS = 1e-6
HEAD_NORM_EPS = 1e-5
MASK_VALUE = -1e30
LOG2E = 1.4426950408889634

V7X_VMEM_LIMIT_BYTES = 56 * 1024 * 1024
LANES = 128
SUBLANES = 8
BF16_EXACT_INT = 256

BF16 = jnp.bfloat16
F32 = jnp.float32


def _cparams(sem, vmem=V7X_VMEM_LIMIT_BYTES, flags=None):
    return pltpu.CompilerParams(dimension_semantics=sem, vmem_limit_bytes=vmem, flags=flags)


def _mod_kernel(c_ref, w_ref, b_ref, o_ref):
    c = c_ref[...]
    cond = c * jax.nn.sigmoid(c)
    o_ref[0] = jnp.dot(cond, w_ref[0], preferred_element_type=F32) + b_ref[0]


def _modulation(c, ada_w, ada_b):
    depth, d, d3 = ada_w.shape
    b = c.shape[0]
    n_tiles = d3 // d
    return pl.pallas_call(
        _mod_kernel,
        grid=(depth, n_tiles),
        in_specs=[
            pl.BlockSpec((b, d), lambda l, j: (0, 0)),
            pl.BlockSpec((1, d, d), lambda l, j: (l, 0, j)),
            pl.BlockSpec((1, 1, d), lambda l, j: (l, 0, j)),
        ],
        out_specs=pl.BlockSpec((1, b, d), lambda l, j: (l, 0, j)),
        out_shape=jax.ShapeDtypeStruct((depth, b, d3), F32),
        compiler_params=_cparams(("arbitrary", "arbitrary")),
        name="modulation",
    )(c, ada_w, ada_b.reshape(depth, 1, d3))


def _inproj_kernel(x_ref, mod_ref, gain_ref, w_ref, *refs, d_model, plan, slab):
    out_refs = refs[:len(plan)]
    hb_ref = refs[len(plan)]
    x = x_ref[0]
    ms = jnp.mean(x * x, axis=-1, keepdims=True)
    hn = x * lax.rsqrt(ms + NORM_EPS) * gain_ref[...]
    shift = mod_ref[0, :, 0:d_model]
    scale = mod_ref[0, :, d_model:2 * d_model]
    hb_ref[...] = (hn * (1.0 + scale) + shift).astype(BF16)
    for (kind, c0, n_heads, width, out_scale), o_ref in zip(plan, out_refs):
        total = width if kind == "tok" else n_heads * width
        for s0 in range(0, total, slab):
            r = jnp.dot(hb_ref[...], w_ref[:, c0 + s0:c0 + s0 + slab],
                        preferred_element_type=F32)
            if out_scale != 1.0:
                r = r * out_scale
            if kind == "tok":
                o_ref[0, :, s0:s0 + slab] = r.astype(BF16)
            elif kind == "head" and width >= slab:
                hh, off = divmod(s0, width)
                o_ref[0, hh, :, off:off + slab] = r.astype(BF16)
            elif kind == "head":
                for p in range(slab // width):
                    o_ref[0, s0 // width + p] = r[:, p * width:(p + 1) * width].astype(BF16)
            else:
                for p in range(slab // width):
                    o_ref[0, s0 // width + p, 0] = r[:, p * width:(p + 1) * width].T.astype(BF16)


def _in_projection(x, mod, gain, w_bf16, plan, tm, slab):
    b, s, d = x.shape
    n = w_bf16.shape[1]
    out_shapes, out_specs = [], []
    for (kind, c0, n_heads, width, _) in plan:
        if kind == "tok":
            out_shapes.append(jax.ShapeDtypeStruct((b, s, width), BF16))
            out_specs.append(pl.BlockSpec((1, tm, width), lambda bi, i: (bi, i, 0)))
        elif kind == "head":
            out_shapes.append(jax.ShapeDtypeStruct((b, n_heads, s, width), BF16))
            out_specs.append(pl.BlockSpec((1, n_heads, tm, width), lambda bi, i: (bi, 0, i, 0)))
        else:
            assert slab % width == 0
            out_shapes.append(jax.ShapeDtypeStruct((b, n_heads, s // tm, width, tm), BF16))
            out_specs.append(pl.BlockSpec((1, n_heads, 1, width, tm),
                                          lambda bi, i: (bi, 0, i, 0, 0)))
    return pl.pallas_call(
        functools.partial(_inproj_kernel, d_model=d, plan=plan, slab=slab),
        grid=(b, s // tm),
        in_specs=[
            pl.BlockSpec((1, tm, d), lambda bi, i: (bi, i, 0)),
            pl.BlockSpec((1, 1, 3 * d), lambda bi, i: (bi, 0, 0)),
            pl.BlockSpec((1, d), lambda bi, i: (0, 0)),
            pl.BlockSpec((d, n), lambda bi, i: (0, 0)),
        ],
        out_specs=out_specs,
        out_shape=out_shapes,
        scratch_shapes=[pltpu.VMEM((tm, d), BF16)],
        compiler_params=_cparams(("arbitrary", "arbitrary")),
        name="in_projection",
    )(x, mod, gain, w_bf16)


def _outproj_kernel(y_ref, w_ref, x_ref, mod_ref, gain_ref, o_ref, *, d_model):
    t = jnp.dot(y_ref[0], w_ref[...], preferred_element_type=F32)
    ms = jnp.mean(t * t, axis=-1, keepdims=True)
    n = t * lax.rsqrt(ms + NORM_EPS) * gain_ref[...]
    gate = mod_ref[0, :, 2 * d_model:3 * d_model]
    o_ref[0] = x_ref[0] + gate * n


def _out_projection(y, w_bf16, x, mod, gain, tm):
    b, s, d = x.shape
    k = y.shape[-1]
    return pl.pallas_call(
        functools.partial(_outproj_kernel, d_model=d),
        grid=(b, s // tm),
        in_specs=[
            pl.BlockSpec((1, tm, k), lambda bi, i: (bi, i, 0)),
            pl.BlockSpec((k, d), lambda bi, i: (0, 0)),
            pl.BlockSpec((1, tm, d), lambda bi, i: (bi, i, 0)),
            pl.BlockSpec((1, 1, 3 * d), lambda bi, i: (bi, 0, 0)),
            pl.BlockSpec((1, d), lambda bi, i: (0, 0)),
        ],
        out_specs=pl.BlockSpec((1, tm, d), lambda bi, i: (bi, i, 0)),
        out_shape=jax.ShapeDtypeStruct((b, s, d), F32),
        compiler_params=_cparams(("arbitrary", "arbitrary")),
        name="out_projection",
    )(y, w_bf16, x, mod, gain)


N_ALIBI_FEATURES = 6
ATTN_ROW_CHUNK = 16
ATTN_SUM_ROWS = 16
ATTN_SCORE_SLOTS = 4
ATTN_PROB_SLOTS = 2


def _alibi_key_features(s):
    pos = jnp.arange(s, dtype=jnp.int32)[:, None]
    col = jnp.arange(LANES, dtype=jnp.int32)[None, :]
    lo = (pos % BF16_EXACT_INT).astype(F32)
    hi = (pos // BF16_EXACT_INT).astype(F32)
    feat = jnp.where(col < 3, lo, jnp.where(col < N_ALIBI_FEATURES, hi, 0.0))
    return feat.astype(BF16)


def _attn_schedule(nq):
    qb, kb, first = [], [], []
    for qi in range(nq):
        qb.append(qi), kb.append(qi), first.append(1)
        for j in range(qi):
            qb.append(qi), kb.append(j), first.append(0)
    return (np.asarray(qb, np.int32), np.asarray(kb, np.int32), np.asarray(first, np.int32))


def _attn_kernel(slope_ref, qb_ref, kb_ref, first_ref, qT_ref, k_ref, kf_ref, vT_ref, g_ref,
                 lamv_ref, subln_ref, y_ref,
                 qs_ref, feat_ref, s0_ref, s1_ref, s2_ref, s3_ref, mc0_ref, mc1_ref, mc2_ref,
                 mc3_ref, p0_ref, p1_ref, al0_ref, al1_ref, m_ref, acc_ref, corr_ref,
                 *, tq, nq, lam_init):
    h = pl.program_id(1)
    tk = tq
    w = 2 * tq
    rc = ATTN_ROW_CHUNK
    n_chunks = tk // rc
    n_pos = nq * (nq + 1) // 2
    s_refs = (s0_ref, s1_ref, s2_ref, s3_ref)
    mc_refs = (mc0_ref, mc1_ref, mc2_ref, mc3_ref)
    p_refs, al_refs = (p0_ref, p1_ref), (al0_ref, al1_ref)
    assert n_pos % ATTN_SCORE_SLOTS == 0 and n_pos >= 2 * ATTN_SCORE_SLOTS

    slope2 = slope_ref[h] * LOG2E
    fr = lax.broadcasted_iota(jnp.int32, (LANES, w), 0)
    x = jnp.full((LANES, w), slope2, F32)
    hi = x.astype(BF16).astype(F32)
    mid = (x - hi).astype(BF16).astype(F32)
    lo = x - hi - mid
    part = jnp.where((fr == 0) | (fr == 3), hi, jnp.where((fr == 1) | (fr == 4), mid, lo))
    feat = jnp.where(fr < 3, part, jnp.where(fr < N_ALIBI_FEATURES, part * BF16_EXACT_INT, 0.0))
    feat_ref[...] = feat.astype(BF16)
    kk = lax.broadcasted_iota(jnp.int32, (tk, w), 0)
    qq = lax.broadcasted_iota(jnp.int32, (tk, w), 1)
    qq = jnp.where(qq >= tq, qq - tq, qq)
    ahead = (kk - qq).astype(F32)
    same_chunk = (kk >> CHUNK_SHIFT) == (qq >> CHUNK_SHIFT)
    corr_ref[0] = jnp.zeros((tk, w), F32)
    corr_ref[1] = jnp.where(kk <= qq, 0.0,
                            jnp.where(same_chunk, -2.0 * slope2 * ahead, MASK_VALUE))

    def init_query_block(qi, carry):
        qT = qT_ref[0, 0, qi]
        row = lax.broadcasted_iota(jnp.int32, qT.shape, 0)
        zero = jnp.zeros_like(qT)
        qs_ref[qi, :, 0:tq] = jnp.where(row < DA_HEAD_DIM, qT, zero)
        qs_ref[qi, :, tq:w] = jnp.where(row >= DA_HEAD_DIM, qT, zero)
        acc_ref[qi] = jnp.zeros(acc_ref.shape[1:], F32)
        m_ref[qi] = jnp.full(m_ref.shape[1:], MASK_VALUE, F32)
        return carry

    lax.fori_loop(0, nq, init_query_block, 0)
    ones_rows = jnp.ones((ATTN_SUM_ROWS, tk), BF16)

    def qk(t, s_slot):
        start = pl.multiple_of(kb_ref[t] * tk, tk)
        ka = jnp.concatenate([k_ref[0, 0, pl.ds(start, tk), :], kf_ref[pl.ds(start, tk), :]],
                             axis=1)
        qa = jnp.concatenate([qs_ref[qb_ref[t]], feat_ref[...]], axis=0)
        s = jnp.dot(ka, qa, preferred_element_type=F32) + corr_ref[first_ref[t]]
        s_refs[s_slot][...] = s
        mc_refs[s_slot][...] = jnp.max(s, axis=0, keepdims=True)

    def sm(t, s_slot, p_slot):
        qi = qb_ref[t]
        m_prev = m_ref[qi]
        m_next = jnp.maximum(m_prev, mc_refs[s_slot][...])
        al_refs[p_slot][...] = jnp.exp2(m_prev - m_next)
        m_ref[qi] = m_next
        m_rows = jnp.broadcast_to(m_next, (rc, w))
        for c in range(n_chunks):
            p = jnp.exp2(s_refs[s_slot][c * rc:(c + 1) * rc, :] - m_rows)
            p_refs[p_slot][c * rc:(c + 1) * rc, :] = p.astype(BF16)

    def pv(t, p_slot):
        qi = qb_ref[t]
        v_aug = jnp.concatenate([vT_ref[0, 0, kb_ref[t]], ones_rows], axis=0)
        upd = jnp.dot(v_aug, p_refs[p_slot][...], preferred_element_type=F32)
        acc_ref[qi] = al_refs[p_slot][...] * acc_ref[qi] + upd

    def time_step(tau, u, do_qk=True, do_sm=True, do_pv=True):
        if do_qk:
            qk(tau, u)
        if do_sm:
            sm(tau - 2, (u + 2) % ATTN_SCORE_SLOTS, u % ATTN_PROB_SLOTS)
        if do_pv:
            pv(tau - 3, (u + 1) % ATTN_PROB_SLOTS)

    time_step(0, 0, do_sm=False, do_pv=False)
    time_step(1, 1, do_sm=False, do_pv=False)
    time_step(2, 2, do_pv=False)
    time_step(3, 3)

    def steady(i, carry):
        for u in range(ATTN_SCORE_SLOTS):
            time_step(ATTN_SCORE_SLOTS * i + u, u)
        return carry

    lax.fori_loop(1, n_pos // ATTN_SCORE_SLOTS, steady, 0)
    time_step(n_pos, 0, do_qk=False)
    time_step(n_pos + 1, 1, do_qk=False)
    time_step(n_pos + 2, 2, do_qk=False, do_sm=False)

    lv = lamv_ref[...]
    lam = (jnp.exp(jnp.sum(lv[0:1] * lv[1:2], axis=-1, keepdims=True))
           - jnp.exp(jnp.sum(lv[2:3] * lv[3:4], axis=-1, keepdims=True)) + lam_init)

    def finish_query_block(qi, carry):
        o = acc_ref[qi, 0:DA_V_DIM, :] / acc_ref[qi, DA_V_DIM:DA_V_DIM + 1, :]
        out = o[:, 0:tq] - lam * o[:, tq:w]
        ms = jnp.mean(out * out, axis=0, keepdims=True)
        out = out * lax.rsqrt(ms + HEAD_NORM_EPS) * subln_ref[...] * (1.0 - lam_init)
        rows = pl.ds(pl.multiple_of(qi * tq, tq), tq)
        g = g_ref[0, rows, :].astype(F32)
        y_ref[0, rows, :] = (g * jax.nn.sigmoid(g) * out.T).astype(BF16)
        return carry

    lax.fori_loop(0, nq, finish_query_block, 0)


def _diff_attention(qT, k, vT, g, slopes, lamv, subln, lam_init):
    b, nh, nq, dk, tq = qT.shape
    s = k.shape[2]
    w = 2 * tq
    qb, kb, first = _attn_schedule(nq)
    kernel = functools.partial(_attn_kernel, tq=tq, nq=nq, lam_init=lam_init)
    smem = pl.BlockSpec(memory_space=pltpu.SMEM)
    once = pl.Buffered(1)
    return pl.pallas_call(
        kernel,
        grid=(b, nh),
        in_specs=[
            smem, smem, smem, smem,
            pl.BlockSpec((1, 1, nq, dk, tq), lambda bi, h: (bi, h, 0, 0, 0), pipeline_mode=once),
            pl.BlockSpec((1, 1, s, dk), lambda bi, h: (bi, h, 0, 0)),
            pl.BlockSpec((s, LANES), lambda bi, h: (0, 0), pipeline_mode=once),
            pl.BlockSpec((1, 1, nq, DA_V_DIM, tq), lambda bi, h: (bi, h, 0, 0, 0)),
            pl.BlockSpec((1, s, DA_V_DIM), lambda bi, h: (bi, 0, h)),
            pl.BlockSpec((4, DA_HEAD_DIM), lambda bi, h: (0, 0)),
            pl.BlockSpec((DA_V_DIM, 1), lambda bi, h: (0, 0)),
        ],
        out_specs=pl.BlockSpec((1, s, DA_V_DIM), lambda bi, h: (bi, 0, h)),
        out_shape=jax.ShapeDtypeStruct((b, s, nh * DA_V_DIM), BF16),
        scratch_shapes=(
            [pltpu.VMEM((nq, LANES, w), BF16),
             pltpu.VMEM((LANES, w), BF16)]
            + [pltpu.VMEM((tq, w), F32)] * ATTN_SCORE_SLOTS
            + [pltpu.VMEM((1, w), F32)] * ATTN_SCORE_SLOTS
            + [pltpu.VMEM((tq, w), BF16)] * ATTN_PROB_SLOTS
            + [pltpu.VMEM((1, w), F32)] * ATTN_PROB_SLOTS
            + [pltpu.VMEM((nq, 1, w), F32),
               pltpu.VMEM((nq, DA_V_DIM + ATTN_SUM_ROWS, w), F32),
               pltpu.VMEM((2, tq, w), F32)]
        ),
        compiler_params=_cparams(("arbitrary", "arbitrary")),
        name="diff_attention",
    )(slopes, jnp.asarray(qb), jnp.asarray(kb), jnp.asarray(first), qT, k,
      _alibi_key_features(s), vT, g, lamv, subln.reshape(DA_V_DIM, 1))


def _ret_kernel(lg_ref, q_ref, k_ref, v_ref, g_ref, y_ref, state_ref, decay_ref, *, blk):
    h = pl.program_id(1)
    n = pl.program_id(2)
    lg = lg_ref[h]

    @pl.when(n == 0)
    def _():
        state_ref[...] = jnp.zeros(state_ref.shape, F32)
        ti = lax.broadcasted_iota(jnp.int32, (blk, blk), 0)
        ui = lax.broadcasted_iota(jnp.int32, (blk, blk), 1)
        dist = jnp.abs(ti - ui).astype(F32)
        allowed = (ui >> CHUNK_SHIFT) <= (ti >> CHUNK_SHIFT)
        decay_ref[...] = jnp.where(allowed, jnp.exp(lg * dist), 0.0)

    q = q_ref[0, 0]
    k = k_ref[0, 0] * jnp.asarray(RET_QK_DIM ** -0.5, BF16)
    v = v_ref[0, 0]
    t = lax.broadcasted_iota(jnp.int32, (blk, 1), 0).astype(F32)
    qd = (q.astype(F32) * jnp.exp(lg * t)).astype(BF16)
    kd = (k.astype(F32) * jnp.exp(lg * (blk - t))).astype(BF16)

    a = lax.dot_general(q, k, (((1,), (1,)), ((), ())), preferred_element_type=F32)
    a = (a * decay_ref[...]).astype(BF16)
    out = jnp.dot(a, v, preferred_element_type=F32)
    out = out + jnp.dot(qd, state_ref[...].astype(BF16), preferred_element_type=F32)
    kv = lax.dot_general(kd, v, (((0,), (0,)), ((), ())), preferred_element_type=F32)
    block_decay = jnp.exp(lg * jnp.full((1, 1), float(blk), F32))
    state_ref[...] = block_decay * state_ref[...] + kv

    ms = jnp.mean(out * out, axis=-1, keepdims=True)
    out = out * lax.rsqrt(ms + HEAD_NORM_EPS)
    g = g_ref[0].astype(F32)
    y_ref[0] = (g * jax.nn.sigmoid(g) * out).astype(BF16)


def _retention(q, k, v, g, log_gammas, blk):
    b, nh, s, dk = q.shape
    dv = v.shape[-1]
    return pl.pallas_call(
        functools.partial(_ret_kernel, blk=blk),
        grid=(b, nh, s // blk),
        in_specs=[
            pl.BlockSpec(memory_space=pltpu.SMEM),
            pl.BlockSpec((1, 1, blk, dk), lambda bi, h, i: (bi, h, i, 0)),
            pl.BlockSpec((1, 1, blk, dk), lambda bi, h, i: (bi, h, i, 0)),
            pl.BlockSpec((1, 1, blk, dv), lambda bi, h, i: (bi, h, i, 0)),
            pl.BlockSpec((1, blk, dv), lambda bi, h, i: (bi, i, h)),
        ],
        out_specs=pl.BlockSpec((1, blk, dv), lambda bi, h, i: (bi, i, h)),
        out_shape=jax.ShapeDtypeStruct((b, s, nh * dv), BF16),
        scratch_shapes=[
            pltpu.VMEM((dk, dv), F32),
            pltpu.VMEM((blk, blk), F32),
        ],
        compiler_params=_cparams(("arbitrary", "arbitrary", "arbitrary")),
        name="retention",
    )(log_gammas, q, k, v, g)


def kernel(x, c, ada_w, ada_b, pre_gain, post_gain, da_w_in, da_w_out, da_lambda_q1, da_lambda_k1,
           da_lambda_q2, da_lambda_k2, da_subln_gain, ret_w_in, ret_w_out):
    b, s, d = x.shape
    mod = _modulation(c, ada_w, ada_b)

    attn_tile = 512
    qk_w = 2 * DA_HEADS * DA_HEAD_DIM
    v_w = DA_HEADS * DA_V_DIM
    q_scale = LOG2E * DA_HEAD_DIM ** -0.5
    da_plan = (("headT", 0, DA_HEADS, 2 * DA_HEAD_DIM, q_scale),
               ("head", qk_w, DA_HEADS, 2 * DA_HEAD_DIM, 1.0),
               ("headT", 2 * qk_w, DA_HEADS, DA_V_DIM, 1.0),
               ("tok", 2 * qk_w + v_w, None, v_w, 1.0))
    mod0 = mod[0].reshape(b, 1, 3 * d)
    qT, k, vT, g = _in_projection(x, mod0, pre_gain[0:1], da_w_in[0].astype(BF16), da_plan,
                                  tm=attn_tile, slab=256)
    lam_init = 0.8 - 0.6 * math.exp(-0.3 * 0)
    slopes = jnp.asarray([2.0 ** (-8.0 * (hh + 1) / DA_HEADS) for hh in range(DA_HEADS)], F32)
    lamv = jnp.stack([da_lambda_q1[0], da_lambda_k1[0], da_lambda_q2[0], da_lambda_k2[0]])
    y = _diff_attention(qT, k, vT, g, slopes, lamv, da_subln_gain[0], lam_init)
    x = _out_projection(y, da_w_out[0].astype(BF16), x, mod0, post_gain[0:1], tm=512)

    rqk = RET_HEADS * RET_QK_DIM
    rv = RET_HEADS * RET_V_DIM
    ret_plan = (("head", 0, RET_HEADS, RET_QK_DIM, 1.0), ("head", rqk, RET_HEADS, RET_QK_DIM, 1.0),
                ("head", 2 * rqk, RET_HEADS, RET_V_DIM, 1.0), ("tok", 2 * rqk + rv, None, rv, 1.0))
    mod1 = mod[1].reshape(b, 1, 3 * d)
    q, k, v, g = _in_projection(x, mod1, pre_gain[1:2], ret_w_in[0].astype(BF16), ret_plan,
                                tm=256, slab=256)
    gammas = 1.0 - 2.0 ** (-5.0 - jnp.arange(RET_HEADS, dtype=F32))
    y = _retention(q, k, v, g, jnp.log(gammas), blk=256)
    return _out_projection(y, ret_w_out[0].astype(BF16), x, mod1, post_gain[1:2], tm=512)
```

```python
import functools
import math

import jax
import jax.numpy as jnp
import numpy as np
from jax import lax
from jax.experimental import pallas as pl
from jax.experimental.pallas import tpu as pltpu

CHUNK = 64
CHUNK_SHIFT = 6
DA_HEADS = 8
DA_HEAD_DIM = 64
DA_V_DIM = 128
RET_HEADS = 4
RET_QK_DIM = 256
RET_V_DIM = 512
NORM_EPS = 1e-6
HEAD_NORM_EPS = 1e-5
MASK_VALUE = -1e30
LOG2E = 1.4426950408889634

V7X_VMEM_LIMIT_BYTES = 56 * 1024 * 1024
LANES = 128
SUBLANES = 8
BF16_EXACT_INT = 256

BF16 = jnp.bfloat16
F32 = jnp.float32


def _cparams(sem, vmem=V7X_VMEM_LIMIT_BYTES, flags=None):
    return pltpu.CompilerParams(dimension_semantics=sem, vmem_limit_bytes=vmem, flags=flags)


def _mod_kernel(c_ref, w_ref, b_ref, o_ref):
    c = c_ref[...]
    cond = c * jax.nn.sigmoid(c)
    o_ref[0] = jnp.dot(cond, w_ref[0], preferred_element_type=F32) + b_ref[0]


def _modulation(c, ada_w, ada_b):
    depth, d, d3 = ada_w.shape
    b = c.shape[0]
    n_tiles = d3 // d
    return pl.pallas_call(
        _mod_kernel,
        grid=(depth, n_tiles),
        in_specs=[
            pl.BlockSpec((b, d), lambda l, j: (0, 0)),
            pl.BlockSpec((1, d, d), lambda l, j: (l, 0, j)),
            pl.BlockSpec((1, 1, d), lambda l, j: (l, 0, j)),
        ],
        out_specs=pl.BlockSpec((1, b, d), lambda l, j: (l, 0, j)),
        out_shape=jax.ShapeDtypeStruct((depth, b, d3), F32),
        compiler_params=_cparams(("arbitrary", "arbitrary")),
        name="modulation",
    )(c, ada_w, ada_b.reshape(depth, 1, d3))


def _inproj_kernel(x_ref, mod_ref, gain_ref, w_ref, *refs, d_model, plan, slab):
    out_refs = refs[:len(plan)]
    hb_ref = refs[len(plan)]
    x = x_ref[0]
    ms = jnp.mean(x * x, axis=-1, keepdims=True)
    hn = x * lax.rsqrt(ms + NORM_EPS) * gain_ref[...]
    shift = mod_ref[0, :, 0:d_model]
    scale = mod_ref[0, :, d_model:2 * d_model]
    hb_ref[...] = (hn * (1.0 + scale) + shift).astype(BF16)
    for (kind, c0, n_heads, width, out_scale), o_ref in zip(plan, out_refs):
        total = width if kind == "tok" else n_heads * width
        for s0 in range(0, total, slab):
            r = jnp.dot(hb_ref[...], w_ref[:, c0 + s0:c0 + s0 + slab],
                        preferred_element_type=F32)
            if out_scale != 1.0:
                r = r * out_scale
            if kind == "tok":
                o_ref[0, :, s0:s0 + slab] = r.astype(BF16)
            elif kind == "head" and width >= slab:
                hh, off = divmod(s0, width)
                o_ref[0, hh, :, off:off + slab] = r.astype(BF16)
            elif kind == "head":
                for p in range(slab // width):
                    o_ref[0, s0 // width + p] = r[:, p * width:(p + 1) * width].astype(BF16)
            else:
                for p in range(slab // width):
                    o_ref[0, s0 // width + p, 0] = r[:, p * width:(p + 1) * width].T.astype(BF16)


def _in_projection(x, mod, gain, w_bf16, plan, tm, slab):
    b, s, d = x.shape
    n = w_bf16.shape[1]
    out_shapes, out_specs = [], []
    for (kind, c0, n_heads, width, _) in plan:
        if kind == "tok":
            out_shapes.append(jax.ShapeDtypeStruct((b, s, width), BF16))
            out_specs.append(pl.BlockSpec((1, tm, width), lambda bi, i: (bi, i, 0)))
        elif kind == "head":
            out_shapes.append(jax.ShapeDtypeStruct((b, n_heads, s, width), BF16))
            out_specs.append(pl.BlockSpec((1, n_heads, tm, width), lambda bi, i: (bi, 0, i, 0)))
        else:
            assert slab % width == 0
            out_shapes.append(jax.ShapeDtypeStruct((b, n_heads, s // tm, width, tm), BF16))
            out_specs.append(pl.BlockSpec((1, n_heads, 1, width, tm),
                                          lambda bi, i: (bi, 0, i, 0, 0)))
    return pl.pallas_call(
        functools.partial(_inproj_kernel, d_model=d, plan=plan, slab=slab),
        grid=(b, s // tm),
        in_specs=[
            pl.BlockSpec((1, tm, d), lambda bi, i: (bi, i, 0)),
            pl.BlockSpec((1, 1, 3 * d), lambda bi, i: (bi, 0, 0)),
            pl.BlockSpec((1, d), lambda bi, i: (0, 0)),
            pl.BlockSpec((d, n), lambda bi, i: (0, 0)),
        ],
        out_specs=out_specs,
        out_shape=out_shapes,
        scratch_shapes=[pltpu.VMEM((tm, d), BF16)],
        compiler_params=_cparams(("arbitrary", "arbitrary")),
        name="in_projection",
    )(x, mod, gain, w_bf16)


def _outproj_kernel(y_ref, w_ref, x_ref, mod_ref, gain_ref, o_ref, *, d_model):
    t = jnp.dot(y_ref[0], w_ref[...], preferred_element_type=F32)
    ms = jnp.mean(t * t, axis=-1, keepdims=True)
    n = t * lax.rsqrt(ms + NORM_EPS) * gain_ref[...]
    gate = mod_ref[0, :, 2 * d_model:3 * d_model]
    o_ref[0] = x_ref[0] + gate * n


def _out_projection(y, w_bf16, x, mod, gain, tm):
    b, s, d = x.shape
    k = y.shape[-1]
    return pl.pallas_call(
        functools.partial(_outproj_kernel, d_model=d),
        grid=(b, s // tm),
        in_specs=[
            pl.BlockSpec((1, tm, k), lambda bi, i: (bi, i, 0)),
            pl.BlockSpec((k, d), lambda bi, i: (0, 0)),
            pl.BlockSpec((1, tm, d), lambda bi, i: (bi, i, 0)),
            pl.BlockSpec((1, 1, 3 * d), lambda bi, i: (bi, 0, 0)),
            pl.BlockSpec((1, d), lambda bi, i: (0, 0)),
        ],
        out_specs=pl.BlockSpec((1, tm, d), lambda bi, i: (bi, i, 0)),
        out_shape=jax.ShapeDtypeStruct((b, s, d), F32),
        compiler_params=_cparams(("arbitrary", "arbitrary")),
        name="out_projection",
    )(y, w_bf16, x, mod, gain)


N_ALIBI_FEATURES = 6
ATTN_ROW_CHUNK = 16
ATTN_SUM_ROWS = 16
ATTN_SCORE_SLOTS = 4
ATTN_PROB_SLOTS = 2


def _alibi_key_features(s):
    pos = jnp.arange(s, dtype=jnp.int32)[:, None]
    col = jnp.arange(LANES, dtype=jnp.int32)[None, :]
    lo = (pos % BF16_EXACT_INT).astype(F32)
    hi = (pos // BF16_EXACT_INT).astype(F32)
    feat = jnp.where(col < 3, lo, jnp.where(col < N_ALIBI_FEATURES, hi, 0.0))
    return feat.astype(BF16)


def _attn_schedule(nq):
    pairs = [(qi, qi) for qi in range(nq)] + [(qi, j) for qi in range(nq) for j in range(qi)]
    qb, kb = zip(*pairs)
    return np.asarray(qb, np.int32), np.asarray(kb, np.int32)


def _attn_kernel(slope_ref, qb_ref, kb_ref, qT_ref, k_ref, kf_ref, vT_ref, g_ref,
                 lamv_ref, subln_ref, y_ref,
                 qs_ref, feat_ref, s0_ref, s1_ref, s2_ref, s3_ref, mc0_ref, mc1_ref, mc2_ref,
                 mc3_ref, p0_ref, p1_ref, al0_ref, al1_ref, m_ref, acc_ref, corr_ref,
                 *, tq, nq, lam_init):
    h = pl.program_id(1)
    tk = tq
    w = 2 * tq
    rc = ATTN_ROW_CHUNK
    n_chunks = tk // rc
    n_pos = nq * (nq + 1) // 2
    s_refs = (s0_ref, s1_ref, s2_ref, s3_ref)
    mc_refs = (mc0_ref, mc1_ref, mc2_ref, mc3_ref)
    p_refs, al_refs = (p0_ref, p1_ref), (al0_ref, al1_ref)
    n_bodies = n_pos // ATTN_SCORE_SLOTS
    n_diag_bodies = nq // ATTN_SCORE_SLOTS
    assert n_pos % ATTN_SCORE_SLOTS == 0 and nq % ATTN_SCORE_SLOTS == 0 and n_diag_bodies >= 1

    slope2 = slope_ref[h] * LOG2E
    fr = lax.broadcasted_iota(jnp.int32, (LANES, w), 0)
    x = jnp.full((LANES, w), slope2, F32)
    hi = x.astype(BF16).astype(F32)
    mid = (x - hi).astype(BF16).astype(F32)
    lo = x - hi - mid
    part = jnp.where((fr == 0) | (fr == 3), hi, jnp.where((fr == 1) | (fr == 4), mid, lo))
    feat = jnp.where(fr < 3, part, jnp.where(fr < N_ALIBI_FEATURES, part * BF16_EXACT_INT, 0.0))
    feat_ref[...] = feat.astype(BF16)
    kk = lax.broadcasted_iota(jnp.int32, (tk, w), 0)
    qq = lax.broadcasted_iota(jnp.int32, (tk, w), 1)
    qq = jnp.where(qq >= tq, qq - tq, qq)
    ahead = (kk - qq).astype(F32)
    same_chunk = (kk >> CHUNK_SHIFT) == (qq >> CHUNK_SHIFT)
    corr_ref[...] = jnp.where(kk <= qq, 0.0,
                              jnp.where(same_chunk, -2.0 * slope2 * ahead, MASK_VALUE))

    def init_query_block(qi, carry):
        qT = qT_ref[0, 0, qi]
        row = lax.broadcasted_iota(jnp.int32, qT.shape, 0)
        zero = jnp.zeros_like(qT)
        qs_ref[qi, :, 0:tq] = jnp.where(row < DA_HEAD_DIM, qT, zero)
        qs_ref[qi, :, tq:w] = jnp.where(row >= DA_HEAD_DIM, qT, zero)
        acc_ref[qi] = jnp.zeros(acc_ref.shape[1:], F32)
        m_ref[qi] = jnp.full(m_ref.shape[1:], MASK_VALUE, F32)
        return carry

    lax.fori_loop(0, nq, init_query_block, 0)
    ones_rows = jnp.ones((ATTN_SUM_ROWS, tk), BF16)

    def qk(t, s_slot, diag):
        start = pl.multiple_of(kb_ref[t] * tk, tk)
        ka = jnp.concatenate([k_ref[0, 0, pl.ds(start, tk), :], kf_ref[pl.ds(start, tk), :]],
                             axis=1)
        qa = jnp.concatenate([qs_ref[qb_ref[t]], feat_ref[...]], axis=0)
        s = jnp.dot(ka, qa, preferred_element_type=F32)
        if diag:
            s = s + corr_ref[...]
        s_refs[s_slot][...] = s
        mc_refs[s_slot][...] = jnp.max(s, axis=0, keepdims=True)

    def sm(t, s_slot, p_slot):
        qi = qb_ref[t]
        m_prev = m_ref[qi]
        m_next = jnp.maximum(m_prev, mc_refs[s_slot][...])
        al_refs[p_slot][...] = jnp.exp2(m_prev - m_next)
        m_ref[qi] = m_next
        m_rows = jnp.broadcast_to(m_next, (rc, w))
        for c in range(n_chunks):
            p = jnp.exp2(s_refs[s_slot][c * rc:(c + 1) * rc, :] - m_rows)
            p_refs[p_slot][c * rc:(c + 1) * rc, :] = p.astype(BF16)

    def pv(t, p_slot):
        qi = qb_ref[t]
        v_aug = jnp.concatenate([vT_ref[0, 0, kb_ref[t]], ones_rows], axis=0)
        upd = jnp.dot(v_aug, p_refs[p_slot][...], preferred_element_type=F32)
        acc_ref[qi] = al_refs[p_slot][...] * acc_ref[qi] + upd

    def time_step(tau, u, diag=False, do_qk=True, do_sm=True, do_pv=True):
        if do_qk:
            qk(tau, u, diag)
        if do_sm:
            sm(tau - 2, (u + 2) % ATTN_SCORE_SLOTS, u % ATTN_PROB_SLOTS)
        if do_pv:
            pv(tau - 3, (u + 1) % ATTN_PROB_SLOTS)

    time_step(0, 0, diag=True, do_sm=False, do_pv=False)
    time_step(1, 1, diag=True, do_sm=False, do_pv=False)
    time_step(2, 2, diag=True, do_pv=False)
    time_step(3, 3, diag=True)

    def steady(diag, i, carry):
        for u in range(ATTN_SCORE_SLOTS):
            time_step(ATTN_SCORE_SLOTS * i + u, u, diag=diag)
        return carry

    lax.fori_loop(1, n_diag_bodies, functools.partial(steady, True), 0)
    lax.fori_loop(n_diag_bodies, n_bodies, functools.partial(steady, False), 0)
    time_step(n_pos, 0, do_qk=False)
    time_step(n_pos + 1, 1, do_qk=False)
    time_step(n_pos + 2, 2, do_qk=False, do_sm=False)

    lv = lamv_ref[...]
    lam = (jnp.exp(jnp.sum(lv[0:1] * lv[1:2], axis=-1, keepdims=True))
           - jnp.exp(jnp.sum(lv[2:3] * lv[3:4], axis=-1, keepdims=True)) + lam_init)

    def finish_query_block(qi, carry):
        o = acc_ref[qi, 0:DA_V_DIM, :] / acc_ref[qi, DA_V_DIM:DA_V_DIM + 1, :]
        out = o[:, 0:tq] - lam * o[:, tq:w]
        ms = jnp.mean(out * out, axis=0, keepdims=True)
        out = out * lax.rsqrt(ms + HEAD_NORM_EPS) * subln_ref[...] * (1.0 - lam_init)
        rows = pl.ds(pl.multiple_of(qi * tq, tq), tq)
        g = g_ref[0, rows, :].astype(F32)
        y_ref[0, rows, :] = (g * jax.nn.sigmoid(g) * out.T).astype(BF16)
        return carry

    lax.fori_loop(0, nq, finish_query_block, 0)


def _diff_attention(qT, k, vT, g, slopes, lamv, subln, lam_init):
    b, nh, nq, dk, tq = qT.shape
    s = k.shape[2]
    w = 2 * tq
    qb, kb = _attn_schedule(nq)
    kernel = functools.partial(_attn_kernel, tq=tq, nq=nq, lam_init=lam_init)
    smem = pl.BlockSpec(memory_space=pltpu.SMEM)
    once = pl.Buffered(1)
    return pl.pallas_call(
        kernel,
        grid=(b, nh),
        in_specs=[
            smem, smem, smem,
            pl.BlockSpec((1, 1, nq, dk, tq), lambda bi, h: (bi, h, 0, 0, 0), pipeline_mode=once),
            pl.BlockSpec((1, 1, s, dk), lambda bi, h: (bi, h, 0, 0)),
            pl.BlockSpec((s, LANES), lambda bi, h: (0, 0), pipeline_mode=once),
            pl.BlockSpec((1, 1, nq, DA_V_DIM, tq), lambda bi, h: (bi, h, 0, 0, 0)),
            pl.BlockSpec((1, s, DA_V_DIM), lambda bi, h: (bi, 0, h)),
            pl.BlockSpec((4, DA_HEAD_DIM), lambda bi, h: (0, 0)),
            pl.BlockSpec((DA_V_DIM, 1), lambda bi, h: (0, 0)),
        ],
        out_specs=pl.BlockSpec((1, s, DA_V_DIM), lambda bi, h: (bi, 0, h)),
        out_shape=jax.ShapeDtypeStruct((b, s, nh * DA_V_DIM), BF16),
        scratch_shapes=(
            [pltpu.VMEM((nq, LANES, w), BF16),
             pltpu.VMEM((LANES, w), BF16)]
            + [pltpu.VMEM((tq, w), F32)] * ATTN_SCORE_SLOTS
            + [pltpu.VMEM((1, w), F32)] * ATTN_SCORE_SLOTS
            + [pltpu.VMEM((tq, w), BF16)] * ATTN_PROB_SLOTS
            + [pltpu.VMEM((1, w), F32)] * ATTN_PROB_SLOTS
            + [pltpu.VMEM((nq, 1, w), F32),
               pltpu.VMEM((nq, DA_V_DIM + ATTN_SUM_ROWS, w), F32),
               pltpu.VMEM((tq, w), F32)]
        ),
        compiler_params=_cparams(("arbitrary", "arbitrary")),
        name="diff_attention",
    )(slopes, jnp.asarray(qb), jnp.asarray(kb), qT, k,
      _alibi_key_features(s), vT, g, lamv, subln.reshape(DA_V_DIM, 1))


def _ret_kernel(lg_ref, q_ref, k_ref, v_ref, g_ref, y_ref, state_ref, decay_ref, *, blk):
    h = pl.program_id(1)
    n = pl.program_id(2)
    lg = lg_ref[h]

    @pl.when(n == 0)
    def _():
        state_ref[...] = jnp.zeros(state_ref.shape, F32)
        ti = lax.broadcasted_iota(jnp.int32, (blk, blk), 0)
        ui = lax.broadcasted_iota(jnp.int32, (blk, blk), 1)
        dist = jnp.abs(ti - ui).astype(F32)
        allowed = (ui >> CHUNK_SHIFT) <= (ti >> CHUNK_SHIFT)
        decay_ref[...] = jnp.where(allowed, jnp.exp(lg * dist), 0.0)

    q = q_ref[0, 0]
    k = k_ref[0, 0] * jnp.asarray(RET_QK_DIM ** -0.5, BF16)
    v = v_ref[0, 0]
    t = lax.broadcasted_iota(jnp.int32, (blk, 1), 0).astype(F32)
    qd = (q.astype(F32) * jnp.exp(lg * t)).astype(BF16)
    kd = (k.astype(F32) * jnp.exp(lg * (blk - t))).astype(BF16)

    a = lax.dot_general(q, k, (((1,), (1,)), ((), ())), preferred_element_type=F32)
    a = (a * decay_ref[...]).astype(BF16)
    out = jnp.dot(a, v, preferred_element_type=F32)
    out = out + jnp.dot(qd, state_ref[...].astype(BF16), preferred_element_type=F32)
    kv = lax.dot_general(kd, v, (((0,), (0,)), ((), ())), preferred_element_type=F32)
    block_decay = jnp.exp(lg * jnp.full((1, 1), float(blk), F32))
    state_ref[...] = block_decay * state_ref[...] + kv

    ms = jnp.mean(out * out, axis=-1, keepdims=True)
    out = out * lax.rsqrt(ms + HEAD_NORM_EPS)
    g = g_ref[0].astype(F32)
    y_ref[0] = (g * jax.nn.sigmoid(g) * out).astype(BF16)


def _retention(q, k, v, g, log_gammas, blk):
    b, nh, s, dk = q.shape
    dv = v.shape[-1]
    return pl.pallas_call(
        functools.partial(_ret_kernel, blk=blk),
        grid=(b, nh, s // blk),
        in_specs=[
            pl.BlockSpec(memory_space=pltpu.SMEM),
            pl.BlockSpec((1, 1, blk, dk), lambda bi, h, i: (bi, h, i, 0)),
            pl.BlockSpec((1, 1, blk, dk), lambda bi, h, i: (bi, h, i, 0)),
            pl.BlockSpec((1, 1, blk, dv), lambda bi, h, i: (bi, h, i, 0)),
            pl.BlockSpec((1, blk, dv), lambda bi, h, i: (bi, i, h)),
        ],
        out_specs=pl.BlockSpec((1, blk, dv), lambda bi, h, i: (bi, i, h)),
        out_shape=jax.ShapeDtypeStruct((b, s, nh * dv), BF16),
        scratch_shapes=[
            pltpu.VMEM((dk, dv), F32),
            pltpu.VMEM((blk, blk), F32),
        ],
        compiler_params=_cparams(("arbitrary", "arbitrary", "arbitrary")),
        name="retention",
    )(log_gammas, q, k, v, g)


def kernel(x, c, ada_w, ada_b, pre_gain, post_gain, da_w_in, da_w_out, da_lambda_q1, da_lambda_k1,
           da_lambda_q2, da_lambda_k2, da_subln_gain, ret_w_in, ret_w_out):
    b, s, d = x.shape
    mod = _modulation(c, ada_w, ada_b)

    attn_tile = 512
    qk_w = 2 * DA_HEADS * DA_HEAD_DIM
    v_w = DA_HEADS * DA_V_DIM
    q_scale = LOG2E * DA_HEAD_DIM ** -0.5
    da_plan = (("headT", 0, DA_HEADS, 2 * DA_HEAD_DIM, q_scale),
               ("head", qk_w, DA_HEADS, 2 * DA_HEAD_DIM, 1.0),
               ("headT", 2 * qk_w, DA_HEADS, DA_V_DIM, 1.0),
               ("tok", 2 * qk_w + v_w, None, v_w, 1.0))
    mod0 = mod[0].reshape(b, 1, 3 * d)
    qT, k, vT, g = _in_projection(x, mod0, pre_gain[0:1], da_w_in[0].astype(BF16), da_plan,
                                  tm=attn_tile, slab=256)
    lam_init = 0.8 - 0.6 * math.exp(-0.3 * 0)
    slopes = jnp.asarray([2.0 ** (-8.0 * (hh + 1) / DA_HEADS) for hh in range(DA_HEADS)], F32)
    lamv = jnp.stack([da_lambda_q1[0], da_lambda_k1[0], da_lambda_q2[0], da_lambda_k2[0]])
    y = _diff_attention(qT, k, vT, g, slopes, lamv, da_subln_gain[0], lam_init)
    x = _out_projection(y, da_w_out[0].astype(BF16), x, mod0, post_gain[0:1], tm=512)

    rqk = RET_HEADS * RET_QK_DIM
    rv = RET_HEADS * RET_V_DIM
    ret_plan = (("head", 0, RET_HEADS, RET_QK_DIM, 1.0), ("head", rqk, RET_HEADS, RET_QK_DIM, 1.0),
                ("head", 2 * rqk, RET_HEADS, RET_V_DIM, 1.0), ("tok", 2 * rqk + rv, None, rv, 1.0))
    mod1 = mod[1].reshape(b, 1, 3 * d)
    q, k, v, g = _in_projection(x, mod1, pre_gain[1:2], ret_w_in[0].astype(BF16), ret_plan,
                                tm=256, slab=256)
    gammas = 1.0 - 2.0 ** (-5.0 - jnp.arange(RET_HEADS, dtype=F32))
    y = _retention(q, k, v, g, jnp.log(gammas), blk=256)
    return _out_projection(y, ret_w_out[0].astype(BF16), x, mod1, post_gain[1:2], tm=512)
```

```python
import functools
import math

import jax
import jax.numpy as jnp
import numpy as np
from jax import lax
from jax.experimental import pallas as pl
from jax.experimental.pallas import tpu as pltpu

CHUNK = 64
CHUNK_SHIFT = 6
DA_HEADS = 8
DA_HEAD_DIM = 64
DA_V_DIM = 128
RET_HEADS = 4
RET_QK_DIM = 256
RET_V_DIM = 512
NORM_EPS = 1e-6
HEAD_NORM_EPS = 1e-5
MASK_VALUE = -1e30
LOG2E = 1.4426950408889634

V7X_VMEM_LIMIT_BYTES = 56 * 1024 * 1024
LANES = 128
SUBLANES = 8
BF16_EXACT_INT = 256

BF16 = jnp.bfloat16
F32 = jnp.float32


def _cparams(sem, vmem=V7X_VMEM_LIMIT_BYTES, flags=None):
    return pltpu.CompilerParams(dimension_semantics=sem, vmem_limit_bytes=vmem, flags=flags)


def _mod_kernel(c_ref, w_ref, b_ref, o_ref):
    c = c_ref[...]
    cond = c * jax.nn.sigmoid(c)
    o_ref[0] = jnp.dot(cond, w_ref[0], preferred_element_type=F32) + b_ref[0]


def _modulation(c, ada_w, ada_b):
    depth, d, d3 = ada_w.shape
    b = c.shape[0]
    n_tiles = d3 // d
    return pl.pallas_call(
        _mod_kernel,
        grid=(depth, n_tiles),
        in_specs=[
            pl.BlockSpec((b, d), lambda l, j: (0, 0)),
            pl.BlockSpec((1, d, d), lambda l, j: (l, 0, j)),
            pl.BlockSpec((1, 1, d), lambda l, j: (l, 0, j)),
        ],
        out_specs=pl.BlockSpec((1, b, d), lambda l, j: (l, 0, j)),
        out_shape=jax.ShapeDtypeStruct((depth, b, d3), F32),
        compiler_params=_cparams(("arbitrary", "arbitrary")),
        name="modulation",
    )(c, ada_w, ada_b.reshape(depth, 1, d3))


def _inproj_kernel(x_ref, mod_ref, gain_ref, w_ref, *refs, d_model, plan, slab):
    out_refs = refs[:len(plan)]
    hb_ref = refs[len(plan)]
    x = x_ref[0]
    ms = jnp.mean(x * x, axis=-1, keepdims=True)
    hn = x * lax.rsqrt(ms + NORM_EPS) * gain_ref[...]
    shift = mod_ref[0, :, 0:d_model]
    scale = mod_ref[0, :, d_model:2 * d_model]
    hb_ref[...] = (hn * (1.0 + scale) + shift).astype(BF16)
    for (kind, c0, n_heads, width, out_scale), o_ref in zip(plan, out_refs):
        total = width if kind == "tok" else n_heads * width
        for s0 in range(0, total, slab):
            r = jnp.dot(hb_ref[...], w_ref[:, c0 + s0:c0 + s0 + slab],
                        preferred_element_type=F32)
            if out_scale != 1.0:
                r = r * out_scale
            if kind == "tok":
                o_ref[0, :, s0:s0 + slab] = r.astype(BF16)
            elif kind == "head" and width >= slab:
                hh, off = divmod(s0, width)
                o_ref[0, hh, :, off:off + slab] = r.astype(BF16)
            elif kind == "head":
                for p in range(slab // width):
                    o_ref[0, s0 // width + p] = r[:, p * width:(p + 1) * width].astype(BF16)
            else:
                for p in range(slab // width):
                    o_ref[0, s0 // width + p, 0] = r[:, p * width:(p + 1) * width].T.astype(BF16)


def _in_projection(x, mod, gain, w_bf16, plan, tm, slab):
    b, s, d = x.shape
    n = w_bf16.shape[1]
    out_shapes, out_specs = [], []
    for (kind, c0, n_heads, width, _) in plan:
        if kind == "tok":
            out_shapes.append(jax.ShapeDtypeStruct((b, s, width), BF16))
            out_specs.append(pl.BlockSpec((1, tm, width), lambda bi, i: (bi, i, 0)))
        elif kind == "head":
            out_shapes.append(jax.ShapeDtypeStruct((b, n_heads, s, width), BF16))
            out_specs.append(pl.BlockSpec((1, n_heads, tm, width), lambda bi, i: (bi, 0, i, 0)))
        else:
            assert slab % width == 0
            out_shapes.append(jax.ShapeDtypeStruct((b, n_heads, s // tm, width, tm), BF16))
            out_specs.append(pl.BlockSpec((1, n_heads, 1, width, tm),
                                          lambda bi, i: (bi, 0, i, 0, 0)))
    return pl.pallas_call(
        functools.partial(_inproj_kernel, d_model=d, plan=plan, slab=slab),
        grid=(b, s // tm),
        in_specs=[
            pl.BlockSpec((1, tm, d), lambda bi, i: (bi, i, 0)),
            pl.BlockSpec((1, 1, 3 * d), lambda bi, i: (bi, 0, 0)),
            pl.BlockSpec((1, d), lambda bi, i: (0, 0)),
            pl.BlockSpec((d, n), lambda bi, i: (0, 0)),
        ],
        out_specs=out_specs,
        out_shape=out_shapes,
        scratch_shapes=[pltpu.VMEM((tm, d), BF16)],
        compiler_params=_cparams(("arbitrary", "arbitrary")),
        name="in_projection",
    )(x, mod, gain, w_bf16)


def _outproj_kernel(y_ref, w_ref, x_ref, mod_ref, gain_ref, o_ref, *, d_model):
    t = jnp.dot(y_ref[0], w_ref[...], preferred_element_type=F32)
    ms = jnp.mean(t * t, axis=-1, keepdims=True)
    n = t * lax.rsqrt(ms + NORM_EPS) * gain_ref[...]
    gate = mod_ref[0, :, 2 * d_model:3 * d_model]
    o_ref[0] = x_ref[0] + gate * n


def _out_projection(y, w_bf16, x, mod, gain, tm):
    b, s, d = x.shape
    k = y.shape[-1]
    return pl.pallas_call(
        functools.partial(_outproj_kernel, d_model=d),
        grid=(b, s // tm),
        in_specs=[
            pl.BlockSpec((1, tm, k), lambda bi, i: (bi, i, 0)),
            pl.BlockSpec((k, d), lambda bi, i: (0, 0)),
            pl.BlockSpec((1, tm, d), lambda bi, i: (bi, i, 0)),
            pl.BlockSpec((1, 1, 3 * d), lambda bi, i: (bi, 0, 0)),
            pl.BlockSpec((1, d), lambda bi, i: (0, 0)),
        ],
        out_specs=pl.BlockSpec((1, tm, d), lambda bi, i: (bi, i, 0)),
        out_shape=jax.ShapeDtypeStruct((b, s, d), F32),
        compiler_params=_cparams(("arbitrary", "arbitrary")),
        name="out_projection",
    )(y, w_bf16, x, mod, gain)


N_ALIBI_FEATURES = 6
ATTN_ROW_CHUNK = 16
ATTN_SUM_ROWS = 16
ATTN_SCORE_SLOTS = 4
ATTN_PROB_SLOTS = 2


def _alibi_key_features(s):
    pos = jnp.arange(s, dtype=jnp.int32)[:, None]
    col = jnp.arange(LANES, dtype=jnp.int32)[None, :]
    lo = (pos % BF16_EXACT_INT).astype(F32)
    hi = (pos // BF16_EXACT_INT).astype(F32)
    feat = jnp.where(col < 3, lo, jnp.where(col < N_ALIBI_FEATURES, hi, 0.0))
    return feat.astype(BF16)


def _attn_schedule(nq):
    pairs = [(qi, qi) for qi in range(nq)] + [(qi, j) for qi in range(nq) for j in range(qi)]
    qb, kb = zip(*pairs)
    return np.asarray(qb, np.int32), np.asarray(kb, np.int32)


def _attn_kernel(slope_ref, qb_ref, kb_ref, qT_ref, k_ref, kf_ref, vT_ref, g_ref,
                 lamv_ref, subln_ref, y_ref,
                 qs_ref, feat_ref, s0_ref, s1_ref, s2_ref, s3_ref, mc0_ref, mc1_ref, mc2_ref,
                 mc3_ref, p0_ref, p1_ref, al0_ref, al1_ref, m_ref, acc_ref, corr_ref,
                 *, tq, nq, lam_init):
    h = pl.program_id(1)
    tk = tq
    w = 2 * tq
    rc = ATTN_ROW_CHUNK
    n_chunks = tk // rc
    n_pos = nq * (nq + 1) // 2
    s_refs = (s0_ref, s1_ref, s2_ref, s3_ref)
    mc_refs = (mc0_ref, mc1_ref, mc2_ref, mc3_ref)
    p_refs, al_refs = (p0_ref, p1_ref), (al0_ref, al1_ref)
    n_bodies = n_pos // ATTN_SCORE_SLOTS
    n_diag_bodies = nq // ATTN_SCORE_SLOTS
    assert n_pos % ATTN_SCORE_SLOTS == 0 and nq % ATTN_SCORE_SLOTS == 0 and n_diag_bodies >= 1

    slope2 = slope_ref[h] * LOG2E
    fr = lax.broadcasted_iota(jnp.int32, (LANES, w), 0)
    x = jnp.full((LANES, w), slope2, F32)
    hi = x.astype(BF16).astype(F32)
    mid = (x - hi).astype(BF16).astype(F32)
    lo = x - hi - mid
    part = jnp.where((fr == 0) | (fr == 3), hi, jnp.where((fr == 1) | (fr == 4), mid, lo))
    feat = jnp.where(fr < 3, part, jnp.where(fr < N_ALIBI_FEATURES, part * BF16_EXACT_INT, 0.0))
    feat_ref[...] = feat.astype(BF16)
    kk = lax.broadcasted_iota(jnp.int32, (tk, w), 0)
    qq = lax.broadcasted_iota(jnp.int32, (tk, w), 1)
    qq = jnp.where(qq >= tq, qq - tq, qq)
    ahead = (kk - qq).astype(F32)
    same_chunk = (kk >> CHUNK_SHIFT) == (qq >> CHUNK_SHIFT)
    corr_ref[...] = jnp.where(kk <= qq, 0.0,
                              jnp.where(same_chunk, -2.0 * slope2 * ahead, MASK_VALUE))

    def init_query_block(qi, carry):
        qT = qT_ref[0, 0, qi]
        row = lax.broadcasted_iota(jnp.int32, qT.shape, 0)
        zero = jnp.zeros_like(qT)
        qs_ref[qi, :, 0:tq] = jnp.where(row < DA_HEAD_DIM, qT, zero)
        qs_ref[qi, :, tq:w] = jnp.where(row >= DA_HEAD_DIM, qT, zero)
        acc_ref[qi] = jnp.zeros(acc_ref.shape[1:], F32)
        m_ref[qi] = jnp.full(m_ref.shape[1:], MASK_VALUE, F32)
        return carry

    lax.fori_loop(0, nq, init_query_block, 0)
    ones_rows = jnp.ones((ATTN_SUM_ROWS, tk), BF16)

    def qk(t, s_slot, diag):
        start = pl.multiple_of(kb_ref[t] * tk, tk)
        ka = jnp.concatenate([k_ref[0, 0, pl.ds(start, tk), :], kf_ref[pl.ds(start, tk), :]],
                             axis=1)
        qa = jnp.concatenate([qs_ref[qb_ref[t]], feat_ref[...]], axis=0)
        s = jnp.dot(ka, qa, preferred_element_type=F32)
        if diag:
            s = s + corr_ref[...]
        s_refs[s_slot][...] = s
        mc_refs[s_slot][...] = jnp.max(s, axis=0, keepdims=True)

    def sm(t, s_slot, p_slot):
        qi = qb_ref[t]
        m_prev = m_ref[qi]
        m_next = jnp.maximum(m_prev, mc_refs[s_slot][...])
        al_refs[p_slot][...] = jnp.exp2(m_prev - m_next)
        m_ref[qi] = m_next
        m_rows = jnp.broadcast_to(m_next, (rc, w))
        for c in range(n_chunks):
            p = jnp.exp2(s_refs[s_slot][c * rc:(c + 1) * rc, :] - m_rows)
            p_refs[p_slot][c * rc:(c + 1) * rc, :] = p.astype(BF16)

    def pv(t, p_slot):
        qi = qb_ref[t]
        v_aug = jnp.concatenate([vT_ref[0, 0, kb_ref[t]], ones_rows], axis=0)
        upd = jnp.dot(v_aug, p_refs[p_slot][...], preferred_element_type=F32)
        acc_ref[qi] = al_refs[p_slot][...] * acc_ref[qi] + upd

    def time_step(tau, u, diag=False, do_qk=True, do_sm=True, do_pv=True):
        if do_qk:
            qk(tau, u, diag)
        if do_sm:
            sm(tau - 2, (u + 2) % ATTN_SCORE_SLOTS, u % ATTN_PROB_SLOTS)
        if do_pv:
            pv(tau - 3, (u + 1) % ATTN_PROB_SLOTS)

    time_step(0, 0, diag=True, do_sm=False, do_pv=False)
    time_step(1, 1, diag=True, do_sm=False, do_pv=False)
    time_step(2, 2, diag=True, do_pv=False)
    time_step(3, 3, diag=True)

    def steady(diag, i, carry):
        for u in range(ATTN_SCORE_SLOTS):
            time_step(ATTN_SCORE_SLOTS * i + u, u, diag=diag)
        return carry

    lax.fori_loop(1, n_diag_bodies, functools.partial(steady, True), 0)
    lax.fori_loop(n_diag_bodies, n_bodies, functools.partial(steady, False), 0)
    time_step(n_pos, 0, do_qk=False)
    time_step(n_pos + 1, 1, do_qk=False)
    time_step(n_pos + 2, 2, do_qk=False, do_sm=False)

    lv = lamv_ref[...]
    lam = (jnp.exp(jnp.sum(lv[0:1] * lv[1:2], axis=-1, keepdims=True))
           - jnp.exp(jnp.sum(lv[2:3] * lv[3:4], axis=-1, keepdims=True)) + lam_init)

    def finish_query_block(qi, carry):
        o = acc_ref[qi, 0:DA_V_DIM, :] / acc_ref[qi, DA_V_DIM:DA_V_DIM + 1, :]
        out = o[:, 0:tq] - lam * o[:, tq:w]
        ms = jnp.mean(out * out, axis=0, keepdims=True)
        out = out * lax.rsqrt(ms + HEAD_NORM_EPS) * subln_ref[...] * (1.0 - lam_init)
        rows = pl.ds(pl.multiple_of(qi * tq, tq), tq)
        g = g_ref[0, rows, :].astype(F32)
        y_ref[0, rows, :] = (g * jax.nn.sigmoid(g) * out.T).astype(BF16)
        return carry

    lax.fori_loop(0, nq, finish_query_block, 0)


def _diff_attention(qT, k, vT, g, slopes, lamv, subln, lam_init):
    b, nh, nq, dk, tq = qT.shape
    s = k.shape[2]
    w = 2 * tq
    qb, kb = _attn_schedule(nq)
    kernel = functools.partial(_attn_kernel, tq=tq, nq=nq, lam_init=lam_init)
    smem = pl.BlockSpec(memory_space=pltpu.SMEM)
    once = pl.Buffered(1)
    return pl.pallas_call(
        kernel,
        grid=(b, nh),
        in_specs=[
            smem, smem, smem,
            pl.BlockSpec((1, 1, nq, dk, tq), lambda bi, h: (bi, h, 0, 0, 0), pipeline_mode=once),
            pl.BlockSpec((1, 1, s, dk), lambda bi, h: (bi, h, 0, 0)),
            pl.BlockSpec((s, LANES), lambda bi, h: (0, 0), pipeline_mode=once),
            pl.BlockSpec((1, 1, nq, DA_V_DIM, tq), lambda bi, h: (bi, h, 0, 0, 0)),
            pl.BlockSpec((1, s, DA_V_DIM), lambda bi, h: (bi, 0, h)),
            pl.BlockSpec((4, DA_HEAD_DIM), lambda bi, h: (0, 0)),
            pl.BlockSpec((DA_V_DIM, 1), lambda bi, h: (0, 0)),
        ],
        out_specs=pl.BlockSpec((1, s, DA_V_DIM), lambda bi, h: (bi, 0, h)),
        out_shape=jax.ShapeDtypeStruct((b, s, nh * DA_V_DIM), BF16),
        scratch_shapes=(
            [pltpu.VMEM((nq, LANES, w), BF16),
             pltpu.VMEM((LANES, w), BF16)]
            + [pltpu.VMEM((tq, w), F32)] * ATTN_SCORE_SLOTS
            + [pltpu.VMEM((1, w), F32)] * ATTN_SCORE_SLOTS
            + [pltpu.VMEM((tq, w), BF16)] * ATTN_PROB_SLOTS
            + [pltpu.VMEM((1, w), F32)] * ATTN_PROB_SLOTS
            + [pltpu.VMEM((nq, 1, w), F32),
               pltpu.VMEM((nq, DA_V_DIM + ATTN_SUM_ROWS, w), F32),
               pltpu.VMEM((tq, w), F32)]
        ),
        compiler_params=_cparams(("arbitrary", "arbitrary")),
        name="diff_attention",
    )(slopes, jnp.asarray(qb), jnp.asarray(kb), qT, k,
      _alibi_key_features(s), vT, g, lamv, subln.reshape(DA_V_DIM, 1))


def _ret_kernel(lg_ref, q_ref, k_ref, v_ref, g_ref, y_ref, state_ref, decay_ref, qdec_ref,
                kdec_ref, *, blk):
    n = pl.program_id(1)
    dk, dv = RET_QK_DIM, RET_V_DIM

    @pl.when(n == 0)
    def _():
        state_ref[...] = jnp.zeros(state_ref.shape, F32)
        ti = lax.broadcasted_iota(jnp.int32, (blk, blk), 0)
        ui = lax.broadcasted_iota(jnp.int32, (blk, blk), 1)
        dist = jnp.abs(ti - ui).astype(F32)
        allowed = (ui >> CHUNK_SHIFT) <= (ti >> CHUNK_SHIFT)
        t = lax.broadcasted_iota(jnp.int32, (blk, dk), 0).astype(F32)
        for h in range(RET_HEADS):
            decay_ref[h] = jnp.where(allowed, jnp.exp(lg_ref[h] * dist), 0.0)
            qdec_ref[h] = jnp.exp(lg_ref[h] * t)
            kdec_ref[h] = jnp.exp(lg_ref[h] * (blk - t)) * dk ** -0.5

    for h in range(RET_HEADS):
        lg = lg_ref[h]
        q = q_ref[0, :, h * dk:(h + 1) * dk]
        k = k_ref[0, :, h * dk:(h + 1) * dk] * jnp.asarray(dk ** -0.5, BF16)
        v = v_ref[0, :, h * dv:(h + 1) * dv]
        qd = (q.astype(F32) * qdec_ref[h]).astype(BF16)
        kd = (k_ref[0, :, h * dk:(h + 1) * dk].astype(F32) * kdec_ref[h]).astype(BF16)

        a = lax.dot_general(q, k, (((1,), (1,)), ((), ())), preferred_element_type=F32)
        a = (a * decay_ref[h]).astype(BF16)
        out = jnp.dot(a, v, preferred_element_type=F32)
        out = out + jnp.dot(qd, state_ref[h].astype(BF16), preferred_element_type=F32)
        kv = lax.dot_general(kd, v, (((0,), (0,)), ((), ())), preferred_element_type=F32)
        block_decay = jnp.exp(lg * jnp.full((1, 1), float(blk), F32))
        state_ref[h] = block_decay * state_ref[h] + kv

        ms = jnp.mean(out * out, axis=-1, keepdims=True)
        out = out * lax.rsqrt(ms + HEAD_NORM_EPS)
        g = g_ref[0, :, h * dv:(h + 1) * dv].astype(F32)
        y_ref[0, :, h * dv:(h + 1) * dv] = (g * jax.nn.sigmoid(g) * out).astype(BF16)


def _retention(q, k, v, g, log_gammas, blk):
    b, s, qk_w = q.shape
    v_w = v.shape[-1]
    return pl.pallas_call(
        functools.partial(_ret_kernel, blk=blk),
        grid=(b, s // blk),
        in_specs=[
            pl.BlockSpec(memory_space=pltpu.SMEM),
            pl.BlockSpec((1, blk, qk_w), lambda bi, i: (bi, i, 0)),
            pl.BlockSpec((1, blk, qk_w), lambda bi, i: (bi, i, 0)),
            pl.BlockSpec((1, blk, v_w), lambda bi, i: (bi, i, 0)),
            pl.BlockSpec((1, blk, v_w), lambda bi, i: (bi, i, 0)),
        ],
        out_specs=pl.BlockSpec((1, blk, v_w), lambda bi, i: (bi, i, 0)),
        out_shape=jax.ShapeDtypeStruct((b, s, v_w), BF16),
        scratch_shapes=[
            pltpu.VMEM((RET_HEADS, RET_QK_DIM, RET_V_DIM), F32),
            pltpu.VMEM((RET_HEADS, blk, blk), F32),
            pltpu.VMEM((RET_HEADS, blk, RET_QK_DIM), F32),
            pltpu.VMEM((RET_HEADS, blk, RET_QK_DIM), F32),
        ],
        compiler_params=_cparams(("arbitrary", "arbitrary")),
        name="retention",
    )(log_gammas, q, k, v, g)


def kernel(x, c, ada_w, ada_b, pre_gain, post_gain, da_w_in, da_w_out, da_lambda_q1, da_lambda_k1,
           da_lambda_q2, da_lambda_k2, da_subln_gain, ret_w_in, ret_w_out):
    b, s, d = x.shape
    mod = _modulation(c, ada_w, ada_b)

    attn_tile = 512
    qk_w = 2 * DA_HEADS * DA_HEAD_DIM
    v_w = DA_HEADS * DA_V_DIM
    q_scale = LOG2E * DA_HEAD_DIM ** -0.5
    da_plan = (("headT", 0, DA_HEADS, 2 * DA_HEAD_DIM, q_scale),
               ("head", qk_w, DA_HEADS, 2 * DA_HEAD_DIM, 1.0),
               ("headT", 2 * qk_w, DA_HEADS, DA_V_DIM, 1.0),
               ("tok", 2 * qk_w + v_w, None, v_w, 1.0))
    mod0 = mod[0].reshape(b, 1, 3 * d)
    qT, k, vT, g = _in_projection(x, mod0, pre_gain[0:1], da_w_in[0].astype(BF16), da_plan,
                                  tm=attn_tile, slab=256)
    lam_init = 0.8 - 0.6 * math.exp(-0.3 * 0)
    slopes = jnp.asarray([2.0 ** (-8.0 * (hh + 1) / DA_HEADS) for hh in range(DA_HEADS)], F32)
    lamv = jnp.stack([da_lambda_q1[0], da_lambda_k1[0], da_lambda_q2[0], da_lambda_k2[0]])
    y = _diff_attention(qT, k, vT, g, slopes, lamv, da_subln_gain[0], lam_init)
    x = _out_projection(y, da_w_out[0].astype(BF16), x, mod0, post_gain[0:1], tm=512)

    rqk = RET_HEADS * RET_QK_DIM
    rv = RET_HEADS * RET_V_DIM
    ret_plan = (("tok", 0, None, rqk, 1.0), ("tok", rqk, None, rqk, 1.0),
                ("tok", 2 * rqk, None, rv, 1.0), ("tok", 2 * rqk + rv, None, rv, 1.0))
    mod1 = mod[1].reshape(b, 1, 3 * d)
    q, k, v, g = _in_projection(x, mod1, pre_gain[1:2], ret_w_in[0].astype(BF16), ret_plan,
                                tm=256, slab=256)
    gammas = 1.0 - 2.0 ** (-5.0 - jnp.arange(RET_HEADS, dtype=F32))
    y = _retention(q, k, v, g, jnp.log(gammas), blk=256)
    return _out_projection(y, ret_w_out[0].astype(BF16), x, mod1, post_gain[1:2], tm=512)
```

```python
import functools
import math

import jax
import jax.numpy as jnp
import numpy as np
from jax import lax
from jax.experimental import pallas as pl
from jax.experimental.pallas import tpu as pltpu

CHUNK = 64
CHUNK_SHIFT = 6
DA_HEADS = 8
DA_HEAD_DIM = 64
DA_V_DIM = 128
RET_HEADS = 4
RET_QK_DIM = 256
RET_V_DIM = 512
NORM_EPS = 1e-6
HEAD_NORM_EPS = 1e-5
MASK_VALUE = -1e30
LOG2E = 1.4426950408889634

V7X_VMEM_LIMIT_BYTES = 56 * 1024 * 1024
LANES = 128
SUBLANES = 8
BF16_EXACT_INT = 256

BF16 = jnp.bfloat16
F32 = jnp.float32


def _cparams(sem, vmem=V7X_VMEM_LIMIT_BYTES, flags=None):
    return pltpu.CompilerParams(dimension_semantics=sem, vmem_limit_bytes=vmem, flags=flags)


def _mod_kernel(c_ref, w_ref, b_ref, o_ref):
    c = c_ref[...]
    cond = c * jax.nn.sigmoid(c)
    o_ref[0] = jnp.dot(cond, w_ref[0], preferred_element_type=F32) + b_ref[0]


def _modulation(c, ada_w, ada_b):
    depth, d, d3 = ada_w.shape
    b = c.shape[0]
    n_tiles = d3 // d
    return pl.pallas_call(
        _mod_kernel,
        grid=(depth, n_tiles),
        in_specs=[
            pl.BlockSpec((b, d), lambda l, j: (0, 0)),
            pl.BlockSpec((1, d, d), lambda l, j: (l, 0, j)),
            pl.BlockSpec((1, 1, d), lambda l, j: (l, 0, j)),
        ],
        out_specs=pl.BlockSpec((1, b, d), lambda l, j: (l, 0, j)),
        out_shape=jax.ShapeDtypeStruct((depth, b, d3), F32),
        compiler_params=_cparams(("arbitrary", "arbitrary")),
        name="modulation",
    )(c, ada_w, ada_b.reshape(depth, 1, d3))


def _inproj_kernel(x_ref, mod_ref, gain_ref, w_ref, *refs, d_model, plan, slab):
    out_refs = refs[:len(plan)]
    hb_ref = refs[len(plan)]
    x = x_ref[0]
    ms = jnp.mean(x * x, axis=-1, keepdims=True)
    hn = x * lax.rsqrt(ms + NORM_EPS) * gain_ref[...]
    shift = mod_ref[0, :, 0:d_model]
    scale = mod_ref[0, :, d_model:2 * d_model]
    hb_ref[...] = (hn * (1.0 + scale) + shift).astype(BF16)
    for (kind, c0, n_heads, width, out_scale), o_ref in zip(plan, out_refs):
        total = width if kind == "tok" else n_heads * width
        for s0 in range(0, total, slab):
            r = jnp.dot(hb_ref[...], w_ref[:, c0 + s0:c0 + s0 + slab],
                        preferred_element_type=F32)
            if out_scale != 1.0:
                r = r * out_scale
            if kind == "tok":
                o_ref[0, :, s0:s0 + slab] = r.astype(BF16)
            elif kind == "head" and width >= slab:
                hh, off = divmod(s0, width)
                o_ref[0, hh, :, off:off + slab] = r.astype(BF16)
            elif kind == "head":
                for p in range(slab // width):
                    o_ref[0, s0 // width + p] = r[:, p * width:(p + 1) * width].astype(BF16)
            else:
                for p in range(slab // width):
                    o_ref[0, s0 // width + p, 0] = r[:, p * width:(p + 1) * width].T.astype(BF16)


def _in_projection(x, mod, gain, w_bf16, plan, tm, slab):
    b, s, d = x.shape
    n = w_bf16.shape[1]
    out_shapes, out_specs = [], []
    for (kind, c0, n_heads, width, _) in plan:
        if kind == "tok":
            out_shapes.append(jax.ShapeDtypeStruct((b, s, width), BF16))
            out_specs.append(pl.BlockSpec((1, tm, width), lambda bi, i: (bi, i, 0)))
        elif kind == "head":
            out_shapes.append(jax.ShapeDtypeStruct((b, n_heads, s, width), BF16))
            out_specs.append(pl.BlockSpec((1, n_heads, tm, width), lambda bi, i: (bi, 0, i, 0)))
        else:
            assert slab % width == 0
            out_shapes.append(jax.ShapeDtypeStruct((b, n_heads, s // tm, width, tm), BF16))
            out_specs.append(pl.BlockSpec((1, n_heads, 1, width, tm),
                                          lambda bi, i: (bi, 0, i, 0, 0)))
    return pl.pallas_call(
        functools.partial(_inproj_kernel, d_model=d, plan=plan, slab=slab),
        grid=(b, s // tm),
        in_specs=[
            pl.BlockSpec((1, tm, d), lambda bi, i: (bi, i, 0)),
            pl.BlockSpec((1, 1, 3 * d), lambda bi, i: (bi, 0, 0)),
            pl.BlockSpec((1, d), lambda bi, i: (0, 0)),
            pl.BlockSpec((d, n), lambda bi, i: (0, 0)),
        ],
        out_specs=out_specs,
        out_shape=out_shapes,
        scratch_shapes=[pltpu.VMEM((tm, d), BF16)],
        compiler_params=_cparams(("arbitrary", "arbitrary")),
        name="in_projection",
    )(x, mod, gain, w_bf16)


def _outproj_kernel(y_ref, w_ref, x_ref, mod_ref, gain_ref, o_ref, *, d_model):
    t = jnp.dot(y_ref[0], w_ref[...], preferred_element_type=F32)
    ms = jnp.mean(t * t, axis=-1, keepdims=True)
    n = t * lax.rsqrt(ms + NORM_EPS) * gain_ref[...]
    gate = mod_ref[0, :, 2 * d_model:3 * d_model]
    o_ref[0] = x_ref[0] + gate * n


def _out_projection(y, w_bf16, x, mod, gain, tm):
    b, s, d = x.shape
    k = y.shape[-1]
    return pl.pallas_call(
        functools.partial(_outproj_kernel, d_model=d),
        grid=(b, s // tm),
        in_specs=[
            pl.BlockSpec((1, tm, k), lambda bi, i: (bi, i, 0)),
            pl.BlockSpec((k, d), lambda bi, i: (0, 0)),
            pl.BlockSpec((1, tm, d), lambda bi, i: (bi, i, 0)),
            pl.BlockSpec((1, 1, 3 * d), lambda bi, i: (bi, 0, 0)),
            pl.BlockSpec((1, d), lambda bi, i: (0, 0)),
        ],
        out_specs=pl.BlockSpec((1, tm, d), lambda bi, i: (bi, i, 0)),
        out_shape=jax.ShapeDtypeStruct((b, s, d), F32),
        compiler_params=_cparams(("arbitrary", "arbitrary")),
        name="out_projection",
    )(y, w_bf16, x, mod, gain)


N_ALIBI_FEATURES = 6
ATTN_ROW_CHUNK = 16
ATTN_SUM_ROWS = 16
ATTN_SCORE_SLOTS = 4
ATTN_PROB_SLOTS = 2
ATTN_OFFDIAG_STEPS = 8


def _alibi_key_features(s):
    pos = jnp.arange(s, dtype=jnp.int32)[:, None]
    col = jnp.arange(LANES, dtype=jnp.int32)[None, :]
    lo = (pos % BF16_EXACT_INT).astype(F32)
    hi = (pos // BF16_EXACT_INT).astype(F32)
    feat = jnp.where(col < 3, lo, jnp.where(col < N_ALIBI_FEATURES, hi, 0.0))
    return feat.astype(BF16)


def _attn_schedule(nq):
    pairs = [(qi, qi) for qi in range(nq)] + [(qi, j) for qi in range(nq) for j in range(qi)]
    qb, kb = zip(*pairs)
    return np.asarray(qb, np.int32), np.asarray(kb, np.int32)


def _attn_kernel(slope_ref, qb_ref, kb_ref, qT_ref, k_ref, kf_ref, vT_ref, g_ref,
                 lamv_ref, subln_ref, y_ref,
                 qs_ref, feat_ref, s0_ref, s1_ref, s2_ref, s3_ref, mc0_ref, mc1_ref, mc2_ref,
                 mc3_ref, p0_ref, p1_ref, al0_ref, al1_ref, m_ref, acc_ref, corr_ref,
                 *, tq, nq, lam_init):
    h = pl.program_id(1)
    tk = tq
    w = 2 * tq
    rc = ATTN_ROW_CHUNK
    n_chunks = tk // rc
    n_pos = nq * (nq + 1) // 2
    s_refs = (s0_ref, s1_ref, s2_ref, s3_ref)
    mc_refs = (mc0_ref, mc1_ref, mc2_ref, mc3_ref)
    p_refs, al_refs = (p0_ref, p1_ref), (al0_ref, al1_ref)
    n_diag_bodies = nq // ATTN_SCORE_SLOTS
    assert nq % ATTN_SCORE_SLOTS == 0 and n_diag_bodies >= 1
    assert ATTN_OFFDIAG_STEPS % ATTN_SCORE_SLOTS == 0 and (n_pos - nq) % ATTN_OFFDIAG_STEPS == 0

    slope2 = slope_ref[h] * LOG2E
    fr = lax.broadcasted_iota(jnp.int32, (LANES, w), 0)
    x = jnp.full((LANES, w), slope2, F32)
    hi = x.astype(BF16).astype(F32)
    mid = (x - hi).astype(BF16).astype(F32)
    lo = x - hi - mid
    part = jnp.where((fr == 0) | (fr == 3), hi, jnp.where((fr == 1) | (fr == 4), mid, lo))
    feat = jnp.where(fr < 3, part, jnp.where(fr < N_ALIBI_FEATURES, part * BF16_EXACT_INT, 0.0))
    feat_ref[...] = feat.astype(BF16)
    kk = lax.broadcasted_iota(jnp.int32, (tk, w), 0)
    qq = lax.broadcasted_iota(jnp.int32, (tk, w), 1)
    qq = jnp.where(qq >= tq, qq - tq, qq)
    ahead = (kk - qq).astype(F32)
    same_chunk = (kk >> CHUNK_SHIFT) == (qq >> CHUNK_SHIFT)
    corr_ref[...] = jnp.where(kk <= qq, 0.0,
                              jnp.where(same_chunk, -2.0 * slope2 * ahead, MASK_VALUE))

    def init_query_block(qi, carry):
        qT = qT_ref[0, 0, qi]
        row = lax.broadcasted_iota(jnp.int32, qT.shape, 0)
        zero = jnp.zeros_like(qT)
        qs_ref[qi, :, 0:tq] = jnp.where(row < DA_HEAD_DIM, qT, zero)
        qs_ref[qi, :, tq:w] = jnp.where(row >= DA_HEAD_DIM, qT, zero)
        acc_ref[qi] = jnp.zeros(acc_ref.shape[1:], F32)
        m_ref[qi] = jnp.full(m_ref.shape[1:], MASK_VALUE, F32)
        return carry

    lax.fori_loop(0, nq, init_query_block, 0)
    ones_rows = jnp.ones((ATTN_SUM_ROWS, tk), BF16)

    def qk(t, s_slot, diag):
        start = pl.multiple_of(kb_ref[t] * tk, tk)
        ka = jnp.concatenate([k_ref[0, 0, pl.ds(start, tk), :], kf_ref[pl.ds(start, tk), :]],
                             axis=1)
        qa = jnp.concatenate([qs_ref[qb_ref[t]], feat_ref[...]], axis=0)
        s = jnp.dot(ka, qa, preferred_element_type=F32)
        if diag:
            s = s + corr_ref[...]
        s_refs[s_slot][...] = s
        mc_refs[s_slot][...] = jnp.max(s, axis=0, keepdims=True)

    def sm(t, s_slot, p_slot):
        qi = qb_ref[t]
        m_prev = m_ref[qi]
        m_next = jnp.maximum(m_prev, mc_refs[s_slot][...])
        al_refs[p_slot][...] = jnp.exp2(m_prev - m_next)
        m_ref[qi] = m_next
        m_rows = jnp.broadcast_to(m_next, (rc, w))
        for c in range(n_chunks):
            p = jnp.exp2(s_refs[s_slot][c * rc:(c + 1) * rc, :] - m_rows)
            p_refs[p_slot][c * rc:(c + 1) * rc, :] = p.astype(BF16)

    def pv(t, p_slot):
        qi = qb_ref[t]
        v_aug = jnp.concatenate([vT_ref[0, 0, kb_ref[t]], ones_rows], axis=0)
        upd = jnp.dot(v_aug, p_refs[p_slot][...], preferred_element_type=F32)
        acc_ref[qi] = al_refs[p_slot][...] * acc_ref[qi] + upd

    def time_step(tau, u, diag=False, do_qk=True, do_sm=True, do_pv=True):
        if do_qk:
            qk(tau, u, diag)
        if do_sm:
            sm(tau - 2, (u + 2) % ATTN_SCORE_SLOTS, u % ATTN_PROB_SLOTS)
        if do_pv:
            pv(tau - 3, (u + 1) % ATTN_PROB_SLOTS)

    time_step(0, 0, diag=True, do_sm=False, do_pv=False)
    time_step(1, 1, diag=True, do_sm=False, do_pv=False)
    time_step(2, 2, diag=True, do_pv=False)
    time_step(3, 3, diag=True)

    def steady(diag, steps, first_tau, i, carry):
        for u in range(steps):
            time_step(first_tau + steps * i + u, u % ATTN_SCORE_SLOTS, diag=diag)
        return carry

    lax.fori_loop(1, n_diag_bodies,
                  functools.partial(steady, True, ATTN_SCORE_SLOTS, 0), 0)
    lax.fori_loop(0, (n_pos - nq) // ATTN_OFFDIAG_STEPS,
                  functools.partial(steady, False, ATTN_OFFDIAG_STEPS, nq), 0)
    time_step(n_pos, 0, do_qk=False)
    time_step(n_pos + 1, 1, do_qk=False)
    time_step(n_pos + 2, 2, do_qk=False, do_sm=False)

    lv = lamv_ref[...]
    lam = (jnp.exp(jnp.sum(lv[0:1] * lv[1:2], axis=-1, keepdims=True))
           - jnp.exp(jnp.sum(lv[2:3] * lv[3:4], axis=-1, keepdims=True)) + lam_init)

    def finish_query_block(qi, carry):
        o = acc_ref[qi, 0:DA_V_DIM, :] / acc_ref[qi, DA_V_DIM:DA_V_DIM + 1, :]
        out = o[:, 0:tq] - lam * o[:, tq:w]
        ms = jnp.mean(out * out, axis=0, keepdims=True)
        out = out * lax.rsqrt(ms + HEAD_NORM_EPS) * subln_ref[...] * (1.0 - lam_init)
        rows = pl.ds(pl.multiple_of(qi * tq, tq), tq)
        g = g_ref[0, rows, :].astype(F32)
        y_ref[0, rows, :] = (g * jax.nn.sigmoid(g) * out.T).astype(BF16)
        return carry

    lax.fori_loop(0, nq, finish_query_block, 0)


def _diff_attention(qT, k, vT, g, slopes, lamv, subln, lam_init):
    b, nh, nq, dk, tq = qT.shape
    s = k.shape[2]
    w = 2 * tq
    qb, kb = _attn_schedule(nq)
    kernel = functools.partial(_attn_kernel, tq=tq, nq=nq, lam_init=lam_init)
    smem = pl.BlockSpec(memory_space=pltpu.SMEM)
    once = pl.Buffered(1)
    return pl.pallas_call(
        kernel,
        grid=(b, nh),
        in_specs=[
            smem, smem, smem,
            pl.BlockSpec((1, 1, nq, dk, tq), lambda bi, h: (bi, h, 0, 0, 0), pipeline_mode=once),
            pl.BlockSpec((1, 1, s, dk), lambda bi, h: (bi, h, 0, 0)),
            pl.BlockSpec((s, LANES), lambda bi, h: (0, 0), pipeline_mode=once),
            pl.BlockSpec((1, 1, nq, DA_V_DIM, tq), lambda bi, h: (bi, h, 0, 0, 0)),
            pl.BlockSpec((1, s, DA_V_DIM), lambda bi, h: (bi, 0, h)),
            pl.BlockSpec((4, DA_HEAD_DIM), lambda bi, h: (0, 0)),
            pl.BlockSpec((DA_V_DIM, 1), lambda bi, h: (0, 0)),
        ],
        out_specs=pl.BlockSpec((1, s, DA_V_DIM), lambda bi, h: (bi, 0, h)),
        out_shape=jax.ShapeDtypeStruct((b, s, nh * DA_V_DIM), BF16),
        scratch_shapes=(
            [pltpu.VMEM((nq, LANES, w), BF16),
             pltpu.VMEM((LANES, w), BF16)]
            + [pltpu.VMEM((tq, w), F32)] * ATTN_SCORE_SLOTS
            + [pltpu.VMEM((1, w), F32)] * ATTN_SCORE_SLOTS
            + [pltpu.VMEM((tq, w), BF16)] * ATTN_PROB_SLOTS
            + [pltpu.VMEM((1, w), F32)] * ATTN_PROB_SLOTS
            + [pltpu.VMEM((nq, 1, w), F32),
               pltpu.VMEM((nq, DA_V_DIM + ATTN_SUM_ROWS, w), F32),
               pltpu.VMEM((tq, w), F32)]
        ),
        compiler_params=_cparams(("arbitrary", "arbitrary")),
        name="diff_attention",
    )(slopes, jnp.asarray(qb), jnp.asarray(kb), qT, k,
      _alibi_key_features(s), vT, g, lamv, subln.reshape(DA_V_DIM, 1))


def _ret_kernel(lg_ref, q_ref, k_ref, v_ref, g_ref, y_ref, state_ref, decay_ref, qdec_ref,
                kdec_ref, *, blk):
    n = pl.program_id(1)
    dk, dv = RET_QK_DIM, RET_V_DIM

    @pl.when(n == 0)
    def _():
        state_ref[...] = jnp.zeros(state_ref.shape, F32)
        ti = lax.broadcasted_iota(jnp.int32, (blk, blk), 0)
        ui = lax.broadcasted_iota(jnp.int32, (blk, blk), 1)
        dist = jnp.abs(ti - ui).astype(F32)
        allowed = (ui >> CHUNK_SHIFT) <= (ti >> CHUNK_SHIFT)
        t = lax.broadcasted_iota(jnp.int32, (blk, dk), 0).astype(F32)
        for h in range(RET_HEADS):
            decay_ref[h] = jnp.where(allowed, jnp.exp(lg_ref[h] * dist), 0.0)
            qdec_ref[h] = jnp.exp(lg_ref[h] * t)
            kdec_ref[h] = jnp.exp(lg_ref[h] * (blk - t)) * dk ** -0.5

    for h in range(RET_HEADS):
        lg = lg_ref[h]
        q = q_ref[0, :, h * dk:(h + 1) * dk]
        k = k_ref[0, :, h * dk:(h + 1) * dk] * jnp.asarray(dk ** -0.5, BF16)
        v = v_ref[0, :, h * dv:(h + 1) * dv]
        qd = (q.astype(F32) * qdec_ref[h]).astype(BF16)
        kd = (k_ref[0, :, h * dk:(h + 1) * dk].astype(F32) * kdec_ref[h]).astype(BF16)

        a = lax.dot_general(q, k, (((1,), (1,)), ((), ())), preferred_element_type=F32)
        a = (a * decay_ref[h]).astype(BF16)
        out = jnp.dot(a, v, preferred_element_type=F32)
        out = out + jnp.dot(qd, state_ref[h].astype(BF16), preferred_element_type=F32)
        kv = lax.dot_general(kd, v, (((0,), (0,)), ((), ())), preferred_element_type=F32)
        block_decay = jnp.exp(lg * jnp.full((1, 1), float(blk), F32))
        state_ref[h] = block_decay * state_ref[h] + kv

        ms = jnp.mean(out * out, axis=-1, keepdims=True)
        out = out * lax.rsqrt(ms + HEAD_NORM_EPS)
        g = g_ref[0, :, h * dv:(h + 1) * dv].astype(F32)
        y_ref[0, :, h * dv:(h + 1) * dv] = (g * jax.nn.sigmoid(g) * out).astype(BF16)


def _retention(q, k, v, g, log_gammas, blk):
    b, s, qk_w = q.shape
    v_w = v.shape[-1]
    return pl.pallas_call(
        functools.partial(_ret_kernel, blk=blk),
        grid=(b, s // blk),
        in_specs=[
            pl.BlockSpec(memory_space=pltpu.SMEM),
            pl.BlockSpec((1, blk, qk_w), lambda bi, i: (bi, i, 0)),
            pl.BlockSpec((1, blk, qk_w), lambda bi, i: (bi, i, 0)),
            pl.BlockSpec((1, blk, v_w), lambda bi, i: (bi, i, 0)),
            pl.BlockSpec((1, blk, v_w), lambda bi, i: (bi, i, 0)),
        ],
        out_specs=pl.BlockSpec((1, blk, v_w), lambda bi, i: (bi, i, 0)),
        out_shape=jax.ShapeDtypeStruct((b, s, v_w), BF16),
        scratch_shapes=[
            pltpu.VMEM((RET_HEADS, RET_QK_DIM, RET_V_DIM), F32),
            pltpu.VMEM((RET_HEADS, blk, blk), F32),
            pltpu.VMEM((RET_HEADS, blk, RET_QK_DIM), F32),
            pltpu.VMEM((RET_HEADS, blk, RET_QK_DIM), F32),
        ],
        compiler_params=_cparams(("arbitrary", "arbitrary")),
        name="retention",
    )(log_gammas, q, k, v, g)


def kernel(x, c, ada_w, ada_b, pre_gain, post_gain, da_w_in, da_w_out, da_lambda_q1, da_lambda_k1,
           da_lambda_q2, da_lambda_k2, da_subln_gain, ret_w_in, ret_w_out):
    b, s, d = x.shape
    mod = _modulation(c, ada_w, ada_b)

    attn_tile = 512
    qk_w = 2 * DA_HEADS * DA_HEAD_DIM
    v_w = DA_HEADS * DA_V_DIM
    q_scale = LOG2E * DA_HEAD_DIM ** -0.5
    da_plan = (("headT", 0, DA_HEADS, 2 * DA_HEAD_DIM, q_scale),
               ("head", qk_w, DA_HEADS, 2 * DA_HEAD_DIM, 1.0),
               ("headT", 2 * qk_w, DA_HEADS, DA_V_DIM, 1.0),
               ("tok", 2 * qk_w + v_w, None, v_w, 1.0))
    mod0 = mod[0].reshape(b, 1, 3 * d)
    qT, k, vT, g = _in_projection(x, mod0, pre_gain[0:1], da_w_in[0].astype(BF16), da_plan,
                                  tm=attn_tile, slab=256)
    lam_init = 0.8 - 0.6 * math.exp(-0.3 * 0)
    slopes = jnp.asarray([2.0 ** (-8.0 * (hh + 1) / DA_HEADS) for hh in range(DA_HEADS)], F32)
    lamv = jnp.stack([da_lambda_q1[0], da_lambda_k1[0], da_lambda_q2[0], da_lambda_k2[0]])
    y = _diff_attention(qT, k, vT, g, slopes, lamv, da_subln_gain[0], lam_init)
    x = _out_projection(y, da_w_out[0].astype(BF16), x, mod0, post_gain[0:1], tm=512)

    rqk = RET_HEADS * RET_QK_DIM
    rv = RET_HEADS * RET_V_DIM
    ret_plan = (("tok", 0, None, rqk, 1.0), ("tok", rqk, None, rqk, 1.0),
                ("tok", 2 * rqk, None, rv, 1.0), ("tok", 2 * rqk + rv, None, rv, 1.0))
    mod1 = mod[1].reshape(b, 1, 3 * d)
    q, k, v, g = _in_projection(x, mod1, pre_gain[1:2], ret_w_in[0].astype(BF16), ret_plan,
                                tm=256, slab=256)
    gammas = 1.0 - 2.0 ** (-5.0 - jnp.arange(RET_HEADS, dtype=F32))
    y = _retention(q, k, v, g, jnp.log(gammas), blk=256)
    return _out_projection(y, ret_w_out[0].astype(BF16), x, mod1, post_gain[1:2], tm=512)
```

```python
import functools
import math

import jax
import jax.numpy as jnp
import numpy as np
from jax import lax
from jax.experimental import pallas as pl
from jax.experimental.pallas import tpu as pltpu

CHUNK = 64
CHUNK_SHIFT = 6
DA_HEADS = 8
DA_HEAD_DIM = 64
DA_V_DIM = 128
RET_HEADS = 4
RET_QK_DIM = 256
RET_V_DIM = 512
NORM_EPS = 1e-6
HEAD_NORM_EPS = 1e-5
MASK_VALUE = -1e30
LOG2E = 1.4426950408889634

V7X_VMEM_LIMIT_BYTES = 56 * 1024 * 1024
LANES = 128
SUBLANES = 8
BF16_EXACT_INT = 256

BF16 = jnp.bfloat16
F32 = jnp.float32


def _cparams(sem, vmem=V7X_VMEM_LIMIT_BYTES, flags=None):
    return pltpu.CompilerParams(dimension_semantics=sem, vmem_limit_bytes=vmem, flags=flags)


def _mod_kernel(c_ref, w_ref, b_ref, o_ref):
    c = c_ref[...]
    cond = c * jax.nn.sigmoid(c)
    o_ref[0] = jnp.dot(cond, w_ref[0], preferred_element_type=F32) + b_ref[0]


def _modulation(c, ada_w, ada_b):
    depth, d, d3 = ada_w.shape
    b = c.shape[0]
    n_tiles = d3 // d
    return pl.pallas_call(
        _mod_kernel,
        grid=(depth, n_tiles),
        in_specs=[
            pl.BlockSpec((b, d), lambda l, j: (0, 0)),
            pl.BlockSpec((1, d, d), lambda l, j: (l, 0, j)),
            pl.BlockSpec((1, 1, d), lambda l, j: (l, 0, j)),
        ],
        out_specs=pl.BlockSpec((1, b, d), lambda l, j: (l, 0, j)),
        out_shape=jax.ShapeDtypeStruct((depth, b, d3), F32),
        compiler_params=_cparams(("arbitrary", "arbitrary")),
        name="modulation",
    )(c, ada_w, ada_b.reshape(depth, 1, d3))


def _inproj_kernel(x_ref, mod_ref, gain_ref, w_ref, *refs, d_model, plan, slab):
    out_refs = refs[:len(plan)]
    hb_ref = refs[len(plan)]
    x = x_ref[0]
    ms = jnp.mean(x * x, axis=-1, keepdims=True)
    hn = x * lax.rsqrt(ms + NORM_EPS) * gain_ref[...]
    shift = mod_ref[0, :, 0:d_model]
    scale = mod_ref[0, :, d_model:2 * d_model]
    hb_ref[...] = (hn * (1.0 + scale) + shift).astype(BF16)
    for (kind, c0, n_heads, width, out_scale), o_ref in zip(plan, out_refs):
        total = width if kind == "tok" else n_heads * width
        for s0 in range(0, total, slab):
            r = jnp.dot(hb_ref[...], w_ref[:, c0 + s0:c0 + s0 + slab],
                        preferred_element_type=F32)
            if out_scale != 1.0:
                r = r * out_scale
            if kind == "tok":
                o_ref[0, :, s0:s0 + slab] = r.astype(BF16)
            elif kind == "head" and width >= slab:
                hh, off = divmod(s0, width)
                o_ref[0, hh, :, off:off + slab] = r.astype(BF16)
            elif kind == "head":
                for p in range(slab // width):
                    o_ref[0, s0 // width + p] = r[:, p * width:(p + 1) * width].astype(BF16)
            else:
                for p in range(slab // width):
                    o_ref[0, s0 // width + p, 0] = r[:, p * width:(p + 1) * width].T.astype(BF16)


def _in_projection(x, mod, gain, w_bf16, plan, tm, slab):
    b, s, d = x.shape
    n = w_bf16.shape[1]
    out_shapes, out_specs = [], []
    for (kind, c0, n_heads, width, _) in plan:
        if kind == "tok":
            out_shapes.append(jax.ShapeDtypeStruct((b, s, width), BF16))
            out_specs.append(pl.BlockSpec((1, tm, width), lambda bi, i: (bi, i, 0)))
        elif kind == "head":
            out_shapes.append(jax.ShapeDtypeStruct((b, n_heads, s, width), BF16))
            out_specs.append(pl.BlockSpec((1, n_heads, tm, width), lambda bi, i: (bi, 0, i, 0)))
        else:
            assert slab % width == 0
            out_shapes.append(jax.ShapeDtypeStruct((b, n_heads, s // tm, width, tm), BF16))
            out_specs.append(pl.BlockSpec((1, n_heads, 1, width, tm),
                                          lambda bi, i: (bi, 0, i, 0, 0)))
    return pl.pallas_call(
        functools.partial(_inproj_kernel, d_model=d, plan=plan, slab=slab),
        grid=(b, s // tm),
        in_specs=[
            pl.BlockSpec((1, tm, d), lambda bi, i: (bi, i, 0)),
            pl.BlockSpec((1, 1, 3 * d), lambda bi, i: (bi, 0, 0)),
            pl.BlockSpec((1, d), lambda bi, i: (0, 0)),
            pl.BlockSpec((d, n), lambda bi, i: (0, 0)),
        ],
        out_specs=out_specs,
        out_shape=out_shapes,
        scratch_shapes=[pltpu.VMEM((tm, d), BF16)],
        compiler_params=_cparams(("arbitrary", "arbitrary")),
        name="in_projection",
    )(x, mod, gain, w_bf16)


def _outproj_kernel(y_ref, w_ref, x_ref, mod_ref, gain_ref, o_ref, *, d_model):
    t = jnp.dot(y_ref[0], w_ref[...], preferred_element_type=F32)
    ms = jnp.mean(t * t, axis=-1, keepdims=True)
    n = t * lax.rsqrt(ms + NORM_EPS) * gain_ref[...]
    gate = mod_ref[0, :, 2 * d_model:3 * d_model]
    o_ref[0] = x_ref[0] + gate * n


def _out_projection(y, w_bf16, x, mod, gain, tm):
    b, s, d = x.shape
    k = y.shape[-1]
    return pl.pallas_call(
        functools.partial(_outproj_kernel, d_model=d),
        grid=(b, s // tm),
        in_specs=[
            pl.BlockSpec((1, tm, k), lambda bi, i: (bi, i, 0)),
            pl.BlockSpec((k, d), lambda bi, i: (0, 0)),
            pl.BlockSpec((1, tm, d), lambda bi, i: (bi, i, 0)),
            pl.BlockSpec((1, 1, 3 * d), lambda bi, i: (bi, 0, 0)),
            pl.BlockSpec((1, d), lambda bi, i: (0, 0)),
        ],
        out_specs=pl.BlockSpec((1, tm, d), lambda bi, i: (bi, i, 0)),
        out_shape=jax.ShapeDtypeStruct((b, s, d), F32),
        compiler_params=_cparams(("arbitrary", "arbitrary")),
        name="out_projection",
    )(y, w_bf16, x, mod, gain)


N_ALIBI_FEATURES = 6
ATTN_ROW_CHUNK = 16
ATTN_SUM_ROWS = 16
ATTN_SCORE_SLOTS = 4
ATTN_PROB_SLOTS = 2
ATTN_DIAG_STEPS = 4
ATTN_PAST_STEPS = 8
ATTN_PIPE_DEPTH = 3
ATTN_ZERO_EXP2 = 150.0
ATTN_BOUND_SLACK = 1.0


def _alibi_key_features(s):
    pos = jnp.arange(s, dtype=jnp.int32)[:, None]
    col = jnp.arange(LANES, dtype=jnp.int32)[None, :]
    lo = (pos % BF16_EXACT_INT).astype(F32)
    hi = (pos // BF16_EXACT_INT).astype(F32)
    feat = jnp.where(col < 3, lo, jnp.where(col < N_ALIBI_FEATURES, hi, 0.0))
    return feat.astype(BF16)


def _attn_schedule(nq):
    pairs = [(qi, qi) for qi in range(nq)] + [(qi, j) for qi in range(nq) for j in range(qi)]
    qb, kb = zip(*pairs)
    return np.asarray(qb, np.int32), np.asarray(kb, np.int32)


def _attn_kernel(slope_ref, qb_ref, kb_ref, qT_ref, k_ref, kf_ref, vT_ref, g_ref,
                 lamv_ref, subln_ref, y_ref,
                 qs_ref, feat_ref, s0_ref, s1_ref, s2_ref, s3_ref, mc0_ref, mc1_ref, mc2_ref,
                 mc3_ref, p0_ref, p1_ref, al0_ref, al1_ref, m_ref, acc_ref, corr_ref,
                 qn_ref, kn_ref, thr_ref, ql_ref, kl_ref,
                 *, tq, nq, lam_init):
    h = pl.program_id(1)
    tk = tq
    w = 2 * tq
    rc = ATTN_ROW_CHUNK
    n_chunks = tk // rc
    n_pos = nq * (nq + 1) // 2
    s_refs = (s0_ref, s1_ref, s2_ref, s3_ref)
    mc_refs = (mc0_ref, mc1_ref, mc2_ref, mc3_ref)
    p_refs, al_refs = (p0_ref, p1_ref), (al0_ref, al1_ref)
    n_past = n_pos - nq
    assert nq % ATTN_DIAG_STEPS == 0 and nq >= ATTN_DIAG_STEPS
    assert ATTN_DIAG_STEPS % ATTN_SCORE_SLOTS == 0 and ATTN_PAST_STEPS % ATTN_SCORE_SLOTS == 0
    assert n_past % ATTN_PAST_STEPS == 0

    slope2 = slope_ref[h] * LOG2E
    fr = lax.broadcasted_iota(jnp.int32, (LANES, w), 0)
    x = jnp.full((LANES, w), slope2, F32)
    hi = x.astype(BF16).astype(F32)
    mid = (x - hi).astype(BF16).astype(F32)
    lo = x - hi - mid
    part = jnp.where((fr == 0) | (fr == 3), hi, jnp.where((fr == 1) | (fr == 4), mid, lo))
    feat = jnp.where(fr < 3, part, jnp.where(fr < N_ALIBI_FEATURES, part * BF16_EXACT_INT, 0.0))
    feat_ref[...] = feat.astype(BF16)
    kk = lax.broadcasted_iota(jnp.int32, (tk, w), 0)
    qq = lax.broadcasted_iota(jnp.int32, (tk, w), 1)
    qq = jnp.where(qq >= tq, qq - tq, qq)
    ahead = (kk - qq).astype(F32)
    same_chunk = (kk >> CHUNK_SHIFT) == (qq >> CHUNK_SHIFT)
    corr_ref[...] = jnp.where(kk <= qq, 0.0,
                              jnp.where(same_chunk, -2.0 * slope2 * ahead, MASK_VALUE))

    def init_query_block(qi, carry):
        qT = qT_ref[0, 0, qi]
        row = lax.broadcasted_iota(jnp.int32, qT.shape, 0)
        zero = jnp.zeros_like(qT)
        qs_ref[qi, :, 0:tq] = jnp.where(row < DA_HEAD_DIM, qT, zero)
        qs_ref[qi, :, tq:w] = jnp.where(row >= DA_HEAD_DIM, qT, zero)
        acc_ref[qi] = jnp.zeros(acc_ref.shape[1:], F32)
        m_ref[qi] = jnp.full(m_ref.shape[1:], MASK_VALUE, F32)
        qf = qT.astype(F32)
        q_norm2 = jnp.max(jnp.sum(qf * qf, axis=0, keepdims=True), axis=1, keepdims=True)
        qn_ref[qi] = jnp.sqrt(q_norm2)[0, 0]
        kf32 = k_ref[0, 0, pl.ds(pl.multiple_of(qi * tk, tk), tk), :].astype(F32)
        k_norm2 = jnp.max(jnp.sum(kf32 * kf32, axis=1, keepdims=True), axis=0, keepdims=True)
        kn_ref[qi] = jnp.sqrt(k_norm2)[0, 0]
        return carry

    lax.fori_loop(0, nq, init_query_block, 0)
    ones_rows = jnp.ones((ATTN_SUM_ROWS, tk), BF16)

    def qk(pair, t, s_slot, diag):
        qi, kj = pair(t)
        start = kj * tk if isinstance(kj, int) else pl.multiple_of(kj * tk, tk)
        ka = jnp.concatenate([k_ref[0, 0, pl.ds(start, tk), :], kf_ref[pl.ds(start, tk), :]],
                             axis=1)
        qa = jnp.concatenate([qs_ref[qi], feat_ref[...]], axis=0)
        s = jnp.dot(ka, qa, preferred_element_type=F32)
        if diag:
            s = s + corr_ref[...]
        s_refs[s_slot][...] = s
        mc_refs[s_slot][...] = jnp.max(s, axis=0, keepdims=True)

    def sm(pair, t, s_slot, p_slot):
        qi, _ = pair(t)
        m_prev = m_ref[qi]
        m_next = jnp.maximum(m_prev, mc_refs[s_slot][...])
        al_refs[p_slot][...] = jnp.exp2(m_prev - m_next)
        m_ref[qi] = m_next
        m_rows = jnp.broadcast_to(m_next, (rc, w))
        for c in range(n_chunks):
            p = jnp.exp2(s_refs[s_slot][c * rc:(c + 1) * rc, :] - m_rows)
            p_refs[p_slot][c * rc:(c + 1) * rc, :] = p.astype(BF16)

    def pv(pair, t, p_slot):
        qi, kj = pair(t)
        v_aug = jnp.concatenate([vT_ref[0, 0, kj], ones_rows], axis=0)
        upd = jnp.dot(v_aug, p_refs[p_slot][...], preferred_element_type=F32)
        acc_ref[qi] = al_refs[p_slot][...] * acc_ref[qi] + upd

    def run_pipeline(pair, n, steps, diag):
        def time_step(tau, u, do_qk=True, do_sm=True, do_pv=True):
            if do_qk:
                qk(pair, tau, u, diag)
            if do_sm:
                sm(pair, tau - 2, (u + 2) % ATTN_SCORE_SLOTS, u % ATTN_PROB_SLOTS)
            if do_pv:
                pv(pair, tau - 3, (u + 1) % ATTN_PROB_SLOTS)

        time_step(0, 0, do_sm=False, do_pv=False)
        time_step(1, 1, do_sm=False, do_pv=False)
        time_step(2, 2, do_pv=False)

        def body(i, carry):
            for u in range(steps):
                time_step(ATTN_PIPE_DEPTH + steps * i + u, (ATTN_PIPE_DEPTH + u) % ATTN_SCORE_SLOTS)
            return carry

        lax.fori_loop(0, n // steps - 1, body, 0)
        for u in range(steps - ATTN_PIPE_DEPTH):
            time_step(n - (steps - ATTN_PIPE_DEPTH) + u, (ATTN_PIPE_DEPTH + u) % ATTN_SCORE_SLOTS)
        time_step(n, 0, do_qk=False)
        time_step(n + 1, 1, do_qk=False)
        time_step(n + 2, 2, do_qk=False, do_sm=False)

    run_pipeline(lambda t: (t, t), nq, ATTN_DIAG_STEPS, diag=True)

    for qi in range(nq):
        thr_ref[qi] = jnp.min(m_ref[qi]) - (ATTN_ZERO_EXP2 + ATTN_BOUND_SLACK)

    def choose(i, carry):
        count, pad_q, pad_k = carry
        qi, kj = qb_ref[nq + i], kb_ref[nq + i]
        last_key = ((kj + 1) * tk - 1).astype(F32)
        bound = qn_ref[qi] * kn_ref[kj] * 1.001 + slope2 * last_key
        keep = bound >= thr_ref[qi]

        @pl.when(keep)
        def _():
            ql_ref[count] = qi
            kl_ref[count] = kj

        return (count + keep.astype(jnp.int32), jnp.where(keep, pad_q, qi),
                jnp.where(keep, pad_k, kj))

    n_keep, pad_q, pad_k = lax.fori_loop(0, n_past, choose,
                                         (jnp.int32(0), jnp.int32(0), jnp.int32(0)))
    n_pad = (-n_keep) & (ATTN_PAST_STEPS - 1)
    for r in range(ATTN_PAST_STEPS - 1):
        @pl.when(r < n_pad)
        def _():
            ql_ref[n_keep + r] = pad_q
            kl_ref[n_keep + r] = pad_k

    n_visit = n_keep + n_pad

    @pl.when(n_visit > 0)
    def _():
        run_pipeline(lambda t: (ql_ref[t], kl_ref[t]), n_visit, ATTN_PAST_STEPS, diag=False)

    lv = lamv_ref[...]
    lam = (jnp.exp(jnp.sum(lv[0:1] * lv[1:2], axis=-1, keepdims=True))
           - jnp.exp(jnp.sum(lv[2:3] * lv[3:4], axis=-1, keepdims=True)) + lam_init)

    def finish_query_block(qi, carry):
        o = acc_ref[qi, 0:DA_V_DIM, :] / acc_ref[qi, DA_V_DIM:DA_V_DIM + 1, :]
        out = o[:, 0:tq] - lam * o[:, tq:w]
        ms = jnp.mean(out * out, axis=0, keepdims=True)
        out = out * lax.rsqrt(ms + HEAD_NORM_EPS) * subln_ref[...] * (1.0 - lam_init)
        rows = pl.ds(pl.multiple_of(qi * tq, tq), tq)
        g = g_ref[0, rows, :].astype(F32)
        y_ref[0, rows, :] = (g * jax.nn.sigmoid(g) * out.T).astype(BF16)
        return carry

    lax.fori_loop(0, nq, finish_query_block, 0)


def _diff_attention(qT, k, vT, g, slopes, lamv, subln, lam_init):
    b, nh, nq, dk, tq = qT.shape
    s = k.shape[2]
    w = 2 * tq
    qb, kb = _attn_schedule(nq)
    kernel = functools.partial(_attn_kernel, tq=tq, nq=nq, lam_init=lam_init)
    smem = pl.BlockSpec(memory_space=pltpu.SMEM)
    once = pl.Buffered(1)
    return pl.pallas_call(
        kernel,
        grid=(b, nh),
        in_specs=[
            smem, smem, smem,
            pl.BlockSpec((1, 1, nq, dk, tq), lambda bi, h: (bi, h, 0, 0, 0), pipeline_mode=once),
            pl.BlockSpec((1, 1, s, dk), lambda bi, h: (bi, h, 0, 0)),
            pl.BlockSpec((s, LANES), lambda bi, h: (0, 0), pipeline_mode=once),
            pl.BlockSpec((1, 1, nq, DA_V_DIM, tq), lambda bi, h: (bi, h, 0, 0, 0)),
            pl.BlockSpec((1, s, DA_V_DIM), lambda bi, h: (bi, 0, h)),
            pl.BlockSpec((4, DA_HEAD_DIM), lambda bi, h: (0, 0)),
            pl.BlockSpec((DA_V_DIM, 1), lambda bi, h: (0, 0)),
        ],
        out_specs=pl.BlockSpec((1, s, DA_V_DIM), lambda bi, h: (bi, 0, h)),
        out_shape=jax.ShapeDtypeStruct((b, s, nh * DA_V_DIM), BF16),
        scratch_shapes=(
            [pltpu.VMEM((nq, LANES, w), BF16),
             pltpu.VMEM((LANES, w), BF16)]
            + [pltpu.VMEM((tq, w), F32)] * ATTN_SCORE_SLOTS
            + [pltpu.VMEM((1, w), F32)] * ATTN_SCORE_SLOTS
            + [pltpu.VMEM((tq, w), BF16)] * ATTN_PROB_SLOTS
            + [pltpu.VMEM((1, w), F32)] * ATTN_PROB_SLOTS
            + [pltpu.VMEM((nq, 1, w), F32),
               pltpu.VMEM((nq, DA_V_DIM + ATTN_SUM_ROWS, w), F32),
               pltpu.VMEM((tq, w), F32),
               pltpu.SMEM((nq,), F32),
               pltpu.SMEM((nq,), F32),
               pltpu.SMEM((nq,), F32),
               pltpu.SMEM((len(qb),), jnp.int32),
               pltpu.SMEM((len(qb),), jnp.int32)]
        ),
        compiler_params=_cparams(("arbitrary", "arbitrary")),
        name="diff_attention",
    )(slopes, jnp.asarray(qb), jnp.asarray(kb), qT, k,
      _alibi_key_features(s), vT, g, lamv, subln.reshape(DA_V_DIM, 1))


def _ret_kernel(lg_ref, q_ref, k_ref, v_ref, g_ref, y_ref, state_ref, decay_ref, qdec_ref,
                kdec_ref, *, blk):
    n = pl.program_id(1)
    dk, dv = RET_QK_DIM, RET_V_DIM

    @pl.when(n == 0)
    def _():
        state_ref[...] = jnp.zeros(state_ref.shape, F32)
        ti = lax.broadcasted_iota(jnp.int32, (blk, blk), 0)
        ui = lax.broadcasted_iota(jnp.int32, (blk, blk), 1)
        dist = jnp.abs(ti - ui).astype(F32)
        allowed = (ui >> CHUNK_SHIFT) <= (ti >> CHUNK_SHIFT)
        t = lax.broadcasted_iota(jnp.int32, (blk, dk), 0).astype(F32)
        for h in range(RET_HEADS):
            decay_ref[h] = jnp.where(allowed, jnp.exp(lg_ref[h] * dist), 0.0)
            qdec_ref[h] = jnp.exp(lg_ref[h] * t)
            kdec_ref[h] = jnp.exp(lg_ref[h] * (blk - t)) * dk ** -0.5

    for h in range(RET_HEADS):
        lg = lg_ref[h]
        q = q_ref[0, :, h * dk:(h + 1) * dk]
        k = k_ref[0, :, h * dk:(h + 1) * dk] * jnp.asarray(dk ** -0.5, BF16)
        v = v_ref[0, :, h * dv:(h + 1) * dv]
        qd = (q.astype(F32) * qdec_ref[h]).astype(BF16)
        kd = (k_ref[0, :, h * dk:(h + 1) * dk].astype(F32) * kdec_ref[h]).astype(BF16)

        a = lax.dot_general(q, k, (((1,), (1,)), ((), ())), preferred_element_type=F32)
        a = (a * decay_ref[h]).astype(BF16)
        out = jnp.dot(a, v, preferred_element_type=F32)
        out = out + jnp.dot(qd, state_ref[h].astype(BF16), preferred_element_type=F32)
        kv = lax.dot_general(kd, v, (((0,), (0,)), ((), ())), preferred_element_type=F32)
        block_decay = jnp.exp(lg * jnp.full((1, 1), float(blk), F32))
        state_ref[h] = block_decay * state_ref[h] + kv

        ms = jnp.mean(out * out, axis=-1, keepdims=True)
        out = out * lax.rsqrt(ms + HEAD_NORM_EPS)
        g = g_ref[0, :, h * dv:(h + 1) * dv].astype(F32)
        y_ref[0, :, h * dv:(h + 1) * dv] = (g * jax.nn.sigmoid(g) * out).astype(BF16)


def _retention(q, k, v, g, log_gammas, blk):
    b, s, qk_w = q.shape
    v_w = v.shape[-1]
    return pl.pallas_call(
        functools.partial(_ret_kernel, blk=blk),
        grid=(b, s // blk),
        in_specs=[
            pl.BlockSpec(memory_space=pltpu.SMEM),
            pl.BlockSpec((1, blk, qk_w), lambda bi, i: (bi, i, 0)),
            pl.BlockSpec((1, blk, qk_w), lambda bi, i: (bi, i, 0)),
            pl.BlockSpec((1, blk, v_w), lambda bi, i: (bi, i, 0)),
            pl.BlockSpec((1, blk, v_w), lambda bi, i: (bi, i, 0)),
        ],
        out_specs=pl.BlockSpec((1, blk, v_w), lambda bi, i: (bi, i, 0)),
        out_shape=jax.ShapeDtypeStruct((b, s, v_w), BF16),
        scratch_shapes=[
            pltpu.VMEM((RET_HEADS, RET_QK_DIM, RET_V_DIM), F32),
            pltpu.VMEM((RET_HEADS, blk, blk), F32),
            pltpu.VMEM((RET_HEADS, blk, RET_QK_DIM), F32),
            pltpu.VMEM((RET_HEADS, blk, RET_QK_DIM), F32),
        ],
        compiler_params=_cparams(("arbitrary", "arbitrary")),
        name="retention",
    )(log_gammas, q, k, v, g)


def kernel(x, c, ada_w, ada_b, pre_gain, post_gain, da_w_in, da_w_out, da_lambda_q1, da_lambda_k1,
           da_lambda_q2, da_lambda_k2, da_subln_gain, ret_w_in, ret_w_out):
    b, s, d = x.shape
    mod = _modulation(c, ada_w, ada_b)

    attn_tile = 512
    qk_w = 2 * DA_HEADS * DA_HEAD_DIM
    v_w = DA_HEADS * DA_V_DIM
    q_scale = LOG2E * DA_HEAD_DIM ** -0.5
    da_plan = (("headT", 0, DA_HEADS, 2 * DA_HEAD_DIM, q_scale),
               ("head", qk_w, DA_HEADS, 2 * DA_HEAD_DIM, 1.0),
               ("headT", 2 * qk_w, DA_HEADS, DA_V_DIM, 1.0),
               ("tok", 2 * qk_w + v_w, None, v_w, 1.0))
    mod0 = mod[0].reshape(b, 1, 3 * d)
    qT, k, vT, g = _in_projection(x, mod0, pre_gain[0:1], da_w_in[0].astype(BF16), da_plan,
                                  tm=attn_tile, slab=256)
    lam_init = 0.8 - 0.6 * math.exp(-0.3 * 0)
    slopes = jnp.asarray([2.0 ** (-8.0 * (hh + 1) / DA_HEADS) for hh in range(DA_HEADS)], F32)
    lamv = jnp.stack([da_lambda_q1[0], da_lambda_k1[0], da_lambda_q2[0], da_lambda_k2[0]])
    y = _diff_attention(qT, k, vT, g, slopes, lamv, da_subln_gain[0], lam_init)
    x = _out_projection(y, da_w_out[0].astype(BF16), x, mod0, post_gain[0:1], tm=512)

    rqk = RET_HEADS * RET_QK_DIM
    rv = RET_HEADS * RET_V_DIM
    ret_plan = (("tok", 0, None, rqk, 1.0), ("tok", rqk, None, rqk, 1.0),
                ("tok", 2 * rqk, None, rv, 1.0), ("tok", 2 * rqk + rv, None, rv, 1.0))
    mod1 = mod[1].reshape(b, 1, 3 * d)
    q, k, v, g = _in_projection(x, mod1, pre_gain[1:2], ret_w_in[0].astype(BF16), ret_plan,
                                tm=256, slab=256)
    gammas = 1.0 - 2.0 ** (-5.0 - jnp.arange(RET_HEADS, dtype=F32))
    y = _retention(q, k, v, g, jnp.log(gammas), blk=256)
    return _out_projection(y, ret_w_out[0].astype(BF16), x, mod1, post_gain[1:2], tm=512)
```

```python
import functools
import math

import jax
import jax.numpy as jnp
import numpy as np
from jax import lax
from jax.experimental import pallas as pl
from jax.experimental.pallas import tpu as pltpu

CHUNK = 64
CHUNK_SHIFT = 6
DA_HEADS = 8
DA_HEAD_DIM = 64
DA_V_DIM = 128
RET_HEADS = 4
RET_QK_DIM = 256
RET_V_DIM = 512
NORM_EPS = 1e-6
HEAD_NORM_EPS = 1e-5
MASK_VALUE = -1e30
LOG2E = 1.4426950408889634

V7X_VMEM_LIMIT_BYTES = 56 * 1024 * 1024
LANES = 128
SUBLANES = 8
BF16_EXACT_INT = 256

BF16 = jnp.bfloat16
F32 = jnp.float32


def _cparams(sem, vmem=V7X_VMEM_LIMIT_BYTES, flags=None):
    return pltpu.CompilerParams(dimension_semantics=sem, vmem_limit_bytes=vmem, flags=flags)


def _mod_kernel(c_ref, w_ref, b_ref, o_ref):
    c = c_ref[...]
    cond = c * jax.nn.sigmoid(c)
    o_ref[0] = jnp.dot(cond, w_ref[0], preferred_element_type=F32) + b_ref[0]


def _modulation(c, ada_w, ada_b):
    depth, d, d3 = ada_w.shape
    b = c.shape[0]
    n_tiles = d3 // d
    return pl.pallas_call(
        _mod_kernel,
        grid=(depth, n_tiles),
        in_specs=[
            pl.BlockSpec((b, d), lambda l, j: (0, 0)),
            pl.BlockSpec((1, d, d), lambda l, j: (l, 0, j)),
            pl.BlockSpec((1, 1, d), lambda l, j: (l, 0, j)),
        ],
        out_specs=pl.BlockSpec((1, b, d), lambda l, j: (l, 0, j)),
        out_shape=jax.ShapeDtypeStruct((depth, b, d3), F32),
        compiler_params=_cparams(("arbitrary", "arbitrary")),
        name="modulation",
    )(c, ada_w, ada_b.reshape(depth, 1, d3))


def _inproj_kernel(x_ref, mod_ref, gain_ref, w_ref, *refs, d_model, plan, slab):
    out_refs = refs[:len(plan)]
    hb_ref = refs[len(plan)]
    x = x_ref[0]
    ms = jnp.mean(x * x, axis=-1, keepdims=True)
    hn = x * lax.rsqrt(ms + NORM_EPS) * gain_ref[...]
    shift = mod_ref[0, :, 0:d_model]
    scale = mod_ref[0, :, d_model:2 * d_model]
    hb_ref[...] = (hn * (1.0 + scale) + shift).astype(BF16)
    for (kind, c0, n_heads, width, out_scale), o_ref in zip(plan, out_refs):
        total = width if kind == "tok" else n_heads * width
        for s0 in range(0, total, slab):
            r = jnp.dot(hb_ref[...], w_ref[:, c0 + s0:c0 + s0 + slab],
                        preferred_element_type=F32)
            if out_scale != 1.0:
                r = r * out_scale
            if kind == "tok":
                o_ref[0, :, s0:s0 + slab] = r.astype(BF16)
            elif kind == "head" and width >= slab:
                hh, off = divmod(s0, width)
                o_ref[0, hh, :, off:off + slab] = r.astype(BF16)
            elif kind == "head":
                for p in range(slab // width):
                    o_ref[0, s0 // width + p] = r[:, p * width:(p + 1) * width].astype(BF16)
            else:
                for p in range(slab // width):
                    o_ref[0, s0 // width + p, 0] = r[:, p * width:(p + 1) * width].T.astype(BF16)


def _in_projection(x, mod, gain, w_bf16, plan, tm, slab):
    b, s, d = x.shape
    n = w_bf16.shape[1]
    out_shapes, out_specs = [], []
    for (kind, c0, n_heads, width, _) in plan:
        if kind == "tok":
            out_shapes.append(jax.ShapeDtypeStruct((b, s, width), BF16))
            out_specs.append(pl.BlockSpec((1, tm, width), lambda bi, i: (bi, i, 0)))
        elif kind == "head":
            out_shapes.append(jax.ShapeDtypeStruct((b, n_heads, s, width), BF16))
            out_specs.append(pl.BlockSpec((1, n_heads, tm, width), lambda bi, i: (bi, 0, i, 0)))
        else:
            assert slab % width == 0
            out_shapes.append(jax.ShapeDtypeStruct((b, n_heads, s // tm, width, tm), BF16))
            out_specs.append(pl.BlockSpec((1, n_heads, 1, width, tm),
                                          lambda bi, i: (bi, 0, i, 0, 0)))
    return pl.pallas_call(
        functools.partial(_inproj_kernel, d_model=d, plan=plan, slab=slab),
        grid=(b, s // tm),
        in_specs=[
            pl.BlockSpec((1, tm, d), lambda bi, i: (bi, i, 0)),
            pl.BlockSpec((1, 1, 3 * d), lambda bi, i: (bi, 0, 0)),
            pl.BlockSpec((1, d), lambda bi, i: (0, 0)),
            pl.BlockSpec((d, n), lambda bi, i: (0, 0)),
        ],
        out_specs=out_specs,
        out_shape=out_shapes,
        scratch_shapes=[pltpu.VMEM((tm, d), BF16)],
        compiler_params=_cparams(("arbitrary", "arbitrary")),
        name="in_projection",
    )(x, mod, gain, w_bf16)


def _outproj_kernel(y_ref, w_ref, x_ref, mod_ref, gain_ref, o_ref, *, d_model):
    t = jnp.dot(y_ref[0], w_ref[...], preferred_element_type=F32)
    ms = jnp.mean(t * t, axis=-1, keepdims=True)
    n = t * lax.rsqrt(ms + NORM_EPS) * gain_ref[...]
    gate = mod_ref[0, :, 2 * d_model:3 * d_model]
    o_ref[0] = x_ref[0] + gate * n


def _out_projection(y, w_bf16, x, mod, gain, tm):
    b, s, d = x.shape
    k = y.shape[-1]
    return pl.pallas_call(
        functools.partial(_outproj_kernel, d_model=d),
        grid=(b, s // tm),
        in_specs=[
            pl.BlockSpec((1, tm, k), lambda bi, i: (bi, i, 0)),
            pl.BlockSpec((k, d), lambda bi, i: (0, 0)),
            pl.BlockSpec((1, tm, d), lambda bi, i: (bi, i, 0)),
            pl.BlockSpec((1, 1, 3 * d), lambda bi, i: (bi, 0, 0)),
            pl.BlockSpec((1, d), lambda bi, i: (0, 0)),
        ],
        out_specs=pl.BlockSpec((1, tm, d), lambda bi, i: (bi, i, 0)),
        out_shape=jax.ShapeDtypeStruct((b, s, d), F32),
        compiler_params=_cparams(("arbitrary", "arbitrary")),
        name="out_projection",
    )(y, w_bf16, x, mod, gain)


N_ALIBI_FEATURES = 6
ATTN_ROW_CHUNK = 16
ATTN_SUM_ROWS = 16
ATTN_SCORE_SLOTS = 4
ATTN_PROB_SLOTS = 2
ATTN_DIAG_STEPS = 4
ATTN_PAST_STEPS = 8
ATTN_PIPE_DEPTH = 3
ATTN_LAMBDA_ROWS = 32
ATTN_STAT_ROWS = 4
ATTN_ZERO_EXP2 = 150.0
ATTN_BOUND_SLACK = 1.0


def _alibi_key_features(s):
    pos = jnp.arange(s, dtype=jnp.int32)[:, None]
    col = jnp.arange(LANES, dtype=jnp.int32)[None, :]
    lo = (pos % BF16_EXACT_INT).astype(F32)
    hi = (pos // BF16_EXACT_INT).astype(F32)
    feat = jnp.where(col < 3, lo, jnp.where(col < N_ALIBI_FEATURES, hi, 0.0))
    return feat.astype(BF16)


def _attn_schedule(nq):
    pairs = [(qi, qi) for qi in range(nq)] + [(qi, j) for qi in range(nq) for j in range(qi)]
    qb, kb = zip(*pairs)
    return np.asarray(qb, np.int32), np.asarray(kb, np.int32)


def _attn_kernel(slope_ref, qb_ref, kb_ref, qT_ref, k_ref, kf_ref, vT_ref, g_ref,
                 lamv_ref, subln_ref, y_ref,
                 qs_ref, feat_ref, s0_ref, s1_ref, s2_ref, s3_ref, mc0_ref, mc1_ref, mc2_ref,
                 mc3_ref, p0_ref, p1_ref, al0_ref, al1_ref, m_ref, acc_ref, corr_ref,
                 qn_ref, kn_ref, thr_ref, ql_ref, kl_ref,
                 *, tq, nq, lam_init):
    h = pl.program_id(1)
    tk = tq
    w = 2 * tq
    rc = ATTN_ROW_CHUNK
    n_chunks = tk // rc
    n_pos = nq * (nq + 1) // 2
    s_refs = (s0_ref, s1_ref, s2_ref, s3_ref)
    mc_refs = (mc0_ref, mc1_ref, mc2_ref, mc3_ref)
    p_refs, al_refs = (p0_ref, p1_ref), (al0_ref, al1_ref)
    n_past = n_pos - nq
    assert nq % ATTN_DIAG_STEPS == 0 and nq >= ATTN_DIAG_STEPS
    assert ATTN_DIAG_STEPS % ATTN_SCORE_SLOTS == 0 and ATTN_PAST_STEPS % ATTN_SCORE_SLOTS == 0
    assert n_past % ATTN_PAST_STEPS == 0

    slope2 = slope_ref[h] * LOG2E
    fr = lax.broadcasted_iota(jnp.int32, (LANES, w), 0)
    x = jnp.full((LANES, w), slope2, F32)
    hi = x.astype(BF16).astype(F32)
    mid = (x - hi).astype(BF16).astype(F32)
    lo = x - hi - mid
    part = jnp.where((fr == 0) | (fr == 3), hi, jnp.where((fr == 1) | (fr == 4), mid, lo))
    feat = jnp.where(fr < 3, part, jnp.where(fr < N_ALIBI_FEATURES, part * BF16_EXACT_INT, 0.0))
    feat_ref[...] = feat.astype(BF16)
    kk = lax.broadcasted_iota(jnp.int32, (tk, w), 0)
    qq = lax.broadcasted_iota(jnp.int32, (tk, w), 1)
    qq = jnp.where(qq >= tq, qq - tq, qq)
    ahead = (kk - qq).astype(F32)
    same_chunk = (kk >> CHUNK_SHIFT) == (qq >> CHUNK_SHIFT)
    corr_ref[...] = jnp.where(kk <= qq, 0.0,
                              jnp.where(same_chunk, -2.0 * slope2 * ahead, MASK_VALUE))

    def init_query_block(qi, carry):
        qT = qT_ref[0, 0, qi]
        row = lax.broadcasted_iota(jnp.int32, qT.shape, 0)
        zero = jnp.zeros_like(qT)
        qs_ref[qi, :, 0:tq] = jnp.where(row < DA_HEAD_DIM, qT, zero)
        qs_ref[qi, :, tq:w] = jnp.where(row >= DA_HEAD_DIM, qT, zero)
        acc_ref[qi] = jnp.zeros(acc_ref.shape[1:], F32)
        m_ref[qi] = jnp.full(m_ref.shape[1:], MASK_VALUE, F32)
        qf = qT.astype(F32)
        q_norm2 = jnp.max(jnp.sum(qf * qf, axis=0, keepdims=True), axis=1, keepdims=True)
        qn_ref[qi] = jnp.sqrt(q_norm2)[0, 0]
        kf32 = k_ref[0, 0, pl.ds(pl.multiple_of(qi * tk, tk), tk), :].astype(F32)
        k_norm2 = jnp.max(jnp.sum(kf32 * kf32, axis=1, keepdims=True), axis=0, keepdims=True)
        kn_ref[qi] = jnp.sqrt(k_norm2)[0, 0]
        return carry

    lax.fori_loop(0, nq, init_query_block, 0)
    ones_rows = jnp.ones((ATTN_SUM_ROWS, tk), BF16)

    def qk(pair, t, s_slot, diag):
        qi, kj = pair(t)
        start = kj * tk if isinstance(kj, int) else pl.multiple_of(kj * tk, tk)
        ka = jnp.concatenate([k_ref[0, 0, pl.ds(start, tk), :], kf_ref[pl.ds(start, tk), :]],
                             axis=1)
        qa = jnp.concatenate([qs_ref[qi], feat_ref[...]], axis=0)
        s = jnp.dot(ka, qa, preferred_element_type=F32)
        if diag:
            s = s + corr_ref[...]
        s_refs[s_slot][...] = s
        mc_refs[s_slot][0:1, :] = jnp.max(s, axis=0, keepdims=True)

    def sm(pair, t, s_slot, p_slot):
        qi, _ = pair(t)
        m_prev = m_ref[qi]
        m_next = jnp.maximum(m_prev, mc_refs[s_slot][0:1, :])
        al_refs[p_slot][0:1, :] = jnp.exp2(m_prev - m_next)
        m_ref[qi] = m_next
        m_rows = jnp.broadcast_to(m_next, (rc, w))
        for c in range(n_chunks):
            p = jnp.exp2(s_refs[s_slot][c * rc:(c + 1) * rc, :] - m_rows)
            p_refs[p_slot][c * rc:(c + 1) * rc, :] = p.astype(BF16)

    def pv(pair, t, p_slot):
        qi, kj = pair(t)
        v_aug = jnp.concatenate([vT_ref[0, 0, kj], ones_rows], axis=0)
        upd = jnp.dot(v_aug, p_refs[p_slot][...], preferred_element_type=F32)
        acc_ref[qi] = al_refs[p_slot][0:1, :] * acc_ref[qi] + upd

    def run_pipeline(pair, n, steps, diag):
        def time_step(tau, u, do_qk=True, do_sm=True, do_pv=True):
            if do_qk:
                qk(pair, tau, u, diag)
            if do_sm:
                sm(pair, tau - 2, (u + 2) % ATTN_SCORE_SLOTS, u % ATTN_PROB_SLOTS)
            if do_pv:
                pv(pair, tau - 3, (u + 1) % ATTN_PROB_SLOTS)

        time_step(0, 0, do_sm=False, do_pv=False)
        time_step(1, 1, do_sm=False, do_pv=False)
        time_step(2, 2, do_pv=False)

        def body(i, carry):
            for u in range(steps):
                time_step(ATTN_PIPE_DEPTH + steps * i + u, (ATTN_PIPE_DEPTH + u) % ATTN_SCORE_SLOTS)
            return carry

        lax.fori_loop(0, n // steps - 1, body, 0)
        for u in range(steps - ATTN_PIPE_DEPTH):
            time_step(n - (steps - ATTN_PIPE_DEPTH) + u, (ATTN_PIPE_DEPTH + u) % ATTN_SCORE_SLOTS)
        time_step(n, 0, do_qk=False)
        time_step(n + 1, 1, do_qk=False)
        time_step(n + 2, 2, do_qk=False, do_sm=False)

    run_pipeline(lambda t: (t, t), nq, ATTN_DIAG_STEPS, diag=True)

    for qi in range(nq):
        thr_ref[qi] = jnp.min(m_ref[qi]) - (ATTN_ZERO_EXP2 + ATTN_BOUND_SLACK)

    def choose(i, carry):
        count, pad_q, pad_k = carry
        qi, kj = qb_ref[nq + i], kb_ref[nq + i]
        last_key = ((kj + 1) * tk - 1).astype(F32)
        bound = qn_ref[qi] * kn_ref[kj] * 1.001 + slope2 * last_key
        keep = bound >= thr_ref[qi]

        @pl.when(keep)
        def _():
            ql_ref[count] = qi
            kl_ref[count] = kj

        return (count + keep.astype(jnp.int32), jnp.where(keep, pad_q, qi),
                jnp.where(keep, pad_k, kj))

    n_keep, pad_q, pad_k = lax.fori_loop(0, n_past, choose,
                                         (jnp.int32(0), jnp.int32(0), jnp.int32(0)))
    n_pad = (-n_keep) & (ATTN_PAST_STEPS - 1)
    for r in range(ATTN_PAST_STEPS - 1):
        @pl.when(r < n_pad)
        def _():
            ql_ref[n_keep + r] = pad_q
            kl_ref[n_keep + r] = pad_k

    n_visit = n_keep + n_pad

    @pl.when(n_visit > 0)
    def _():
        run_pipeline(lambda t: (ql_ref[t], kl_ref[t]), n_visit, ATTN_PAST_STEPS, diag=False)

    lv = lamv_ref[...]
    lam = (jnp.exp(jnp.sum(lv[0:1] * lv[1:2], axis=-1, keepdims=True))
           - jnp.exp(jnp.sum(lv[2:3] * lv[3:4], axis=-1, keepdims=True)) + lam_init)

    def finish_query_block(qi, carry):
        o = acc_ref[qi, 0:DA_V_DIM, :] / acc_ref[qi, DA_V_DIM:DA_V_DIM + 1, :]
        out = o[:, 0:tq] - lam * o[:, tq:w]
        ms = jnp.mean(out * out, axis=0, keepdims=True)
        out = out * lax.rsqrt(ms + HEAD_NORM_EPS) * subln_ref[...] * (1.0 - lam_init)
        rows = pl.ds(pl.multiple_of(qi * tq, tq), tq)
        g = g_ref[0, rows, :].astype(F32)
        y_ref[0, rows, :] = (g * jax.nn.sigmoid(g) * out.T).astype(BF16)
        return carry

    lax.fori_loop(0, nq, finish_query_block, 0)


def _diff_attention(qT, k, vT, g, slopes, lamv, subln, lam_init):
    b, nh, nq, dk, tq = qT.shape
    s = k.shape[2]
    w = 2 * tq
    qb, kb = _attn_schedule(nq)
    kernel = functools.partial(_attn_kernel, tq=tq, nq=nq, lam_init=lam_init)
    smem = pl.BlockSpec(memory_space=pltpu.SMEM)
    once = pl.Buffered(1)
    return pl.pallas_call(
        kernel,
        grid=(b, nh),
        in_specs=[
            smem, smem, smem,
            pl.BlockSpec((1, 1, nq, dk, tq), lambda bi, h: (bi, h, 0, 0, 0), pipeline_mode=once),
            pl.BlockSpec((1, 1, s, dk), lambda bi, h: (bi, h, 0, 0)),
            pl.BlockSpec((s, LANES), lambda bi, h: (0, 0), pipeline_mode=once),
            pl.BlockSpec((1, 1, nq, DA_V_DIM, tq), lambda bi, h: (bi, h, 0, 0, 0)),
            pl.BlockSpec((1, s, DA_V_DIM), lambda bi, h: (bi, 0, h)),
            pl.BlockSpec((ATTN_LAMBDA_ROWS, LANES), lambda bi, h: (0, 0)),
            pl.BlockSpec((DA_V_DIM, 1), lambda bi, h: (0, 0)),
        ],
        out_specs=pl.BlockSpec((1, s, DA_V_DIM), lambda bi, h: (bi, 0, h)),
        out_shape=jax.ShapeDtypeStruct((b, s, nh * DA_V_DIM), BF16),
        scratch_shapes=(
            [pltpu.VMEM((nq, LANES, w), BF16),
             pltpu.VMEM((LANES, w), BF16)]
            + [pltpu.VMEM((tq, w), F32)] * ATTN_SCORE_SLOTS
            + [pltpu.VMEM((ATTN_STAT_ROWS, w), F32)] * ATTN_SCORE_SLOTS
            + [pltpu.VMEM((tq, w), BF16)] * ATTN_PROB_SLOTS
            + [pltpu.VMEM((ATTN_STAT_ROWS, w), F32)] * ATTN_PROB_SLOTS
            + [pltpu.VMEM((nq, 1, w), F32),
               pltpu.VMEM((nq, DA_V_DIM + ATTN_SUM_ROWS, w), F32),
               pltpu.VMEM((tq, w), F32),
               pltpu.SMEM((nq,), F32),
               pltpu.SMEM((nq,), F32),
               pltpu.SMEM((nq,), F32),
               pltpu.SMEM((len(qb),), jnp.int32),
               pltpu.SMEM((len(qb),), jnp.int32)]
        ),
        compiler_params=_cparams(("arbitrary", "arbitrary")),
        name="diff_attention",
    )(slopes, jnp.asarray(qb), jnp.asarray(kb), qT, k,
      _alibi_key_features(s), vT, g, lamv, subln.reshape(DA_V_DIM, 1))


def _ret_kernel(lg_ref, q_ref, k_ref, v_ref, g_ref, y_ref, state_ref, decay_ref, qdec_ref,
                kdec_ref, *, blk):
    n = pl.program_id(1)
    dk, dv = RET_QK_DIM, RET_V_DIM

    @pl.when(n == 0)
    def _():
        state_ref[...] = jnp.zeros(state_ref.shape, F32)
        ti = lax.broadcasted_iota(jnp.int32, (blk, blk), 0)
        ui = lax.broadcasted_iota(jnp.int32, (blk, blk), 1)
        dist = jnp.abs(ti - ui).astype(F32)
        allowed = (ui >> CHUNK_SHIFT) <= (ti >> CHUNK_SHIFT)
        t = lax.broadcasted_iota(jnp.int32, (blk, dk), 0).astype(F32)
        for h in range(RET_HEADS):
            decay_ref[h] = jnp.where(allowed, jnp.exp(lg_ref[h] * dist), 0.0)
            qdec_ref[h] = jnp.exp(lg_ref[h] * t)
            kdec_ref[h] = jnp.exp(lg_ref[h] * (blk - t)) * dk ** -0.5

    for h in range(RET_HEADS):
        lg = lg_ref[h]
        q = q_ref[0, :, h * dk:(h + 1) * dk]
        k = k_ref[0, :, h * dk:(h + 1) * dk] * jnp.asarray(dk ** -0.5, BF16)
        v = v_ref[0, :, h * dv:(h + 1) * dv]
        qd = (q.astype(F32) * qdec_ref[h]).astype(BF16)
        kd = (k_ref[0, :, h * dk:(h + 1) * dk].astype(F32) * kdec_ref[h]).astype(BF16)

        a = lax.dot_general(q, k, (((1,), (1,)), ((), ())), preferred_element_type=F32)
        a = (a * decay_ref[h]).astype(BF16)
        out = jnp.dot(a, v, preferred_element_type=F32)
        out = out + jnp.dot(qd, state_ref[h].astype(BF16), preferred_element_type=F32)
        kv = lax.dot_general(kd, v, (((0,), (0,)), ((), ())), preferred_element_type=F32)
        block_decay = jnp.exp(lg * jnp.full((1, 1), float(blk), F32))
        state_ref[h] = block_decay * state_ref[h] + kv

        ms = jnp.mean(out * out, axis=-1, keepdims=True)
        out = out * lax.rsqrt(ms + HEAD_NORM_EPS)
        g = g_ref[0, :, h * dv:(h + 1) * dv].astype(F32)
        y_ref[0, :, h * dv:(h + 1) * dv] = (g * jax.nn.sigmoid(g) * out).astype(BF16)


def _retention(q, k, v, g, log_gammas, blk):
    b, s, qk_w = q.shape
    v_w = v.shape[-1]
    return pl.pallas_call(
        functools.partial(_ret_kernel, blk=blk),
        grid=(b, s // blk),
        in_specs=[
            pl.BlockSpec(memory_space=pltpu.SMEM),
            pl.BlockSpec((1, blk, qk_w), lambda bi, i: (bi, i, 0)),
            pl.BlockSpec((1, blk, qk_w), lambda bi, i: (bi, i, 0)),
            pl.BlockSpec((1, blk, v_w), lambda bi, i: (bi, i, 0)),
            pl.BlockSpec((1, blk, v_w), lambda bi, i: (bi, i, 0)),
        ],
        out_specs=pl.BlockSpec((1, blk, v_w), lambda bi, i: (bi, i, 0)),
        out_shape=jax.ShapeDtypeStruct((b, s, v_w), BF16),
        scratch_shapes=[
            pltpu.VMEM((RET_HEADS, RET_QK_DIM, RET_V_DIM), F32),
            pltpu.VMEM((RET_HEADS, blk, blk), F32),
            pltpu.VMEM((RET_HEADS, blk, RET_QK_DIM), F32),
            pltpu.VMEM((RET_HEADS, blk, RET_QK_DIM), F32),
        ],
        compiler_params=_cparams(("arbitrary", "arbitrary")),
        name="retention",
    )(log_gammas, q, k, v, g)


def kernel(x, c, ada_w, ada_b, pre_gain, post_gain, da_w_in, da_w_out, da_lambda_q1, da_lambda_k1,
           da_lambda_q2, da_lambda_k2, da_subln_gain, ret_w_in, ret_w_out):
    b, s, d = x.shape
    mod = _modulation(c, ada_w, ada_b)

    attn_tile = 512
    qk_w = 2 * DA_HEADS * DA_HEAD_DIM
    v_w = DA_HEADS * DA_V_DIM
    q_scale = LOG2E * DA_HEAD_DIM ** -0.5
    da_plan = (("headT", 0, DA_HEADS, 2 * DA_HEAD_DIM, q_scale),
               ("head", qk_w, DA_HEADS, 2 * DA_HEAD_DIM, 1.0),
               ("headT", 2 * qk_w, DA_HEADS, DA_V_DIM, 1.0),
               ("tok", 2 * qk_w + v_w, None, v_w, 1.0))
    mod0 = mod[0].reshape(b, 1, 3 * d)
    qT, k, vT, g = _in_projection(x, mod0, pre_gain[0:1], da_w_in[0].astype(BF16), da_plan,
                                  tm=attn_tile, slab=256)
    lam_init = 0.8 - 0.6 * math.exp(-0.3 * 0)
    slopes = jnp.asarray([2.0 ** (-8.0 * (hh + 1) / DA_HEADS) for hh in range(DA_HEADS)], F32)
    lamv = jnp.stack([da_lambda_q1[0], da_lambda_k1[0], da_lambda_q2[0], da_lambda_k2[0]])
    lamv = jnp.pad(lamv, ((0, ATTN_LAMBDA_ROWS - 4), (0, LANES - DA_HEAD_DIM)))
    y = _diff_attention(qT, k, vT, g, slopes, lamv, da_subln_gain[0], lam_init)
    x = _out_projection(y, da_w_out[0].astype(BF16), x, mod0, post_gain[0:1], tm=512)

    rqk = RET_HEADS * RET_QK_DIM
    rv = RET_HEADS * RET_V_DIM
    ret_plan = (("tok", 0, None, rqk, 1.0), ("tok", rqk, None, rqk, 1.0),
                ("tok", 2 * rqk, None, rv, 1.0), ("tok", 2 * rqk + rv, None, rv, 1.0))
    mod1 = mod[1].reshape(b, 1, 3 * d)
    q, k, v, g = _in_projection(x, mod1, pre_gain[1:2], ret_w_in[0].astype(BF16), ret_plan,
                                tm=256, slab=256)
    gammas = 1.0 - 2.0 ** (-5.0 - jnp.arange(RET_HEADS, dtype=F32))
    y = _retention(q, k, v, g, jnp.log(gammas), blk=256)
    return _out_projection(y, ret_w_out[0].astype(BF16), x, mod1, post_gain[1:2], tm=512)
```

```python
import functools
import math

import jax
import jax.numpy as jnp
import numpy as np
from jax import lax
from jax.experimental import pallas as pl
from jax.experimental.pallas import tpu as pltpu

CHUNK = 64
CHUNK_SHIFT = 6
DA_HEADS = 8
DA_HEAD_DIM = 64
DA_V_DIM = 128
RET_HEADS = 4
RET_QK_DIM = 256
RET_V_DIM = 512
NORM_EPS = 1e-6
HEAD_NORM_EPS = 1e-5
MASK_VALUE = -1e30
LOG2E = 1.4426950408889634

V7X_VMEM_LIMIT_BYTES = 56 * 1024 * 1024
LANES = 128
SUBLANES = 8
BF16_EXACT_INT = 256

BF16 = jnp.bfloat16
F32 = jnp.float32


def _cparams(sem, vmem=V7X_VMEM_LIMIT_BYTES, flags=None):
    return pltpu.CompilerParams(dimension_semantics=sem, vmem_limit_bytes=vmem, flags=flags)


def _mod_kernel(c_ref, w_ref, b_ref, o_ref):
    c = c_ref[...]
    cond = c * jax.nn.sigmoid(c)
    o_ref[0] = jnp.dot(cond, w_ref[0], preferred_element_type=F32) + b_ref[0]


def _modulation(c, ada_w, ada_b):
    depth, d, d3 = ada_w.shape
    b = c.shape[0]
    n_tiles = d3 // d
    return pl.pallas_call(
        _mod_kernel,
        grid=(depth, n_tiles),
        in_specs=[
            pl.BlockSpec((b, d), lambda l, j: (0, 0)),
            pl.BlockSpec((1, d, d), lambda l, j: (l, 0, j)),
            pl.BlockSpec((1, 1, d), lambda l, j: (l, 0, j)),
        ],
        out_specs=pl.BlockSpec((1, b, d), lambda l, j: (l, 0, j)),
        out_shape=jax.ShapeDtypeStruct((depth, b, d3), F32),
        compiler_params=_cparams(("arbitrary", "arbitrary")),
        name="modulation",
    )(c, ada_w, ada_b.reshape(depth, 1, d3))


def _inproj_kernel(x_ref, mod_ref, gain_ref, w_ref, *refs, d_model, plan, slab):
    out_refs = refs[:len(plan)]
    hb_ref = refs[len(plan)]
    x = x_ref[0]
    ms = jnp.mean(x * x, axis=-1, keepdims=True)
    hn = x * lax.rsqrt(ms + NORM_EPS) * gain_ref[...]
    shift = mod_ref[0, :, 0:d_model]
    scale = mod_ref[0, :, d_model:2 * d_model]
    hb_ref[...] = (hn * (1.0 + scale) + shift).astype(BF16)
    for (kind, c0, n_heads, width, out_scale), o_ref in zip(plan, out_refs):
        total = width if kind == "tok" else n_heads * width
        for s0 in range(0, total, slab):
            r = jnp.dot(hb_ref[...], w_ref[:, c0 + s0:c0 + s0 + slab],
                        preferred_element_type=F32)
            if out_scale != 1.0:
                r = r * out_scale
            if kind == "tok":
                o_ref[0, :, s0:s0 + slab] = r.astype(BF16)
            elif kind == "head" and width >= slab:
                hh, off = divmod(s0, width)
                o_ref[0, hh, :, off:off + slab] = r.astype(BF16)
            elif kind == "head":
                for p in range(slab // width):
                    o_ref[0, s0 // width + p] = r[:, p * width:(p + 1) * width].astype(BF16)
            else:
                for p in range(slab // width):
                    o_ref[0, s0 // width + p, 0] = r[:, p * width:(p + 1) * width].T.astype(BF16)


def _in_projection(x, mod, gain, w_bf16, plan, tm, slab):
    b, s, d = x.shape
    n = w_bf16.shape[1]
    out_shapes, out_specs = [], []
    for (kind, c0, n_heads, width, _) in plan:
        if kind == "tok":
            out_shapes.append(jax.ShapeDtypeStruct((b, s, width), BF16))
            out_specs.append(pl.BlockSpec((1, tm, width), lambda bi, i: (bi, i, 0)))
        elif kind == "head":
            out_shapes.append(jax.ShapeDtypeStruct((b, n_heads, s, width), BF16))
            out_specs.append(pl.BlockSpec((1, n_heads, tm, width), lambda bi, i: (bi, 0, i, 0)))
        else:
            assert slab % width == 0
            out_shapes.append(jax.ShapeDtypeStruct((b, n_heads, s // tm, width, tm), BF16))
            out_specs.append(pl.BlockSpec((1, n_heads, 1, width, tm),
                                          lambda bi, i: (bi, 0, i, 0, 0)))
    return pl.pallas_call(
        functools.partial(_inproj_kernel, d_model=d, plan=plan, slab=slab),
        grid=(b, s // tm),
        in_specs=[
            pl.BlockSpec((1, tm, d), lambda bi, i: (bi, i, 0)),
            pl.BlockSpec((1, 1, 3 * d), lambda bi, i: (bi, 0, 0)),
            pl.BlockSpec((1, d), lambda bi, i: (0, 0)),
            pl.BlockSpec((d, n), lambda bi, i: (0, 0)),
        ],
        out_specs=out_specs,
        out_shape=out_shapes,
        scratch_shapes=[pltpu.VMEM((tm, d), BF16)],
        compiler_params=_cparams(("arbitrary", "arbitrary")),
        name="in_projection",
    )(x, mod, gain, w_bf16)


def _outproj_kernel(y_ref, w_ref, x_ref, mod_ref, gain_ref, o_ref, *, d_model):
    t = jnp.dot(y_ref[0], w_ref[...], preferred_element_type=F32)
    ms = jnp.mean(t * t, axis=-1, keepdims=True)
    n = t * lax.rsqrt(ms + NORM_EPS) * gain_ref[...]
    gate = mod_ref[0, :, 2 * d_model:3 * d_model]
    o_ref[0] = x_ref[0] + gate * n


def _out_projection(y, w_bf16, x, mod, gain, tm):
    b, s, d = x.shape
    k = y.shape[-1]
    return pl.pallas_call(
        functools.partial(_outproj_kernel, d_model=d),
        grid=(b, s // tm),
        in_specs=[
            pl.BlockSpec((1, tm, k), lambda bi, i: (bi, i, 0)),
            pl.BlockSpec((k, d), lambda bi, i: (0, 0)),
            pl.BlockSpec((1, tm, d), lambda bi, i: (bi, i, 0)),
            pl.BlockSpec((1, 1, 3 * d), lambda bi, i: (bi, 0, 0)),
            pl.BlockSpec((1, d), lambda bi, i: (0, 0)),
        ],
        out_specs=pl.BlockSpec((1, tm, d), lambda bi, i: (bi, i, 0)),
        out_shape=jax.ShapeDtypeStruct((b, s, d), F32),
        compiler_params=_cparams(("arbitrary", "arbitrary")),
        name="out_projection",
    )(y, w_bf16, x, mod, gain)


N_ALIBI_FEATURES = 6
ATTN_ROW_CHUNK = 16
ATTN_SUM_ROWS = 16
ATTN_SCORE_SLOTS = 4
ATTN_PROB_SLOTS = 2
ATTN_DIAG_STEPS = 8
ATTN_PAST_STEPS = 8
ATTN_PIPE_DEPTH = 3
ATTN_LAMBDA_ROWS = 32
ATTN_STAT_ROWS = 4
ATTN_ZERO_EXP2 = 150.0
ATTN_BOUND_SLACK = 1.0


def _alibi_key_features(s):
    pos = jnp.arange(s, dtype=jnp.int32)[:, None]
    col = jnp.arange(LANES, dtype=jnp.int32)[None, :]
    lo = (pos % BF16_EXACT_INT).astype(F32)
    hi = (pos // BF16_EXACT_INT).astype(F32)
    feat = jnp.where(col < 3, lo, jnp.where(col < N_ALIBI_FEATURES, hi, 0.0))
    return feat.astype(BF16)


def _attn_schedule(nq):
    pairs = [(qi, qi) for qi in range(nq)] + [(qi, j) for qi in range(nq) for j in range(qi)]
    qb, kb = zip(*pairs)
    return np.asarray(qb, np.int32), np.asarray(kb, np.int32)


def _attn_kernel(slope_ref, qb_ref, kb_ref, qT_ref, k_ref, kf_ref, vT_ref, g_ref,
                 lamv_ref, subln_ref, y_ref,
                 qs_ref, feat_ref, s0_ref, s1_ref, s2_ref, s3_ref, mc0_ref, mc1_ref, mc2_ref,
                 mc3_ref, p0_ref, p1_ref, al0_ref, al1_ref, m_ref, acc_ref, corr_ref,
                 qn_ref, kn_ref, thr_ref, ql_ref, kl_ref,
                 *, tq, nq, lam_init):
    h = pl.program_id(1)
    tk = tq
    w = 2 * tq
    rc = ATTN_ROW_CHUNK
    n_chunks = tk // rc
    n_pos = nq * (nq + 1) // 2
    s_refs = (s0_ref, s1_ref, s2_ref, s3_ref)
    mc_refs = (mc0_ref, mc1_ref, mc2_ref, mc3_ref)
    p_refs, al_refs = (p0_ref, p1_ref), (al0_ref, al1_ref)
    n_past = n_pos - nq
    assert nq % ATTN_DIAG_STEPS == 0 and nq >= ATTN_DIAG_STEPS
    assert ATTN_DIAG_STEPS % ATTN_SCORE_SLOTS == 0 and ATTN_PAST_STEPS % ATTN_SCORE_SLOTS == 0
    assert n_past % ATTN_PAST_STEPS == 0

    slope2 = slope_ref[h] * LOG2E
    fr = lax.broadcasted_iota(jnp.int32, (LANES, w), 0)
    x = jnp.full((LANES, w), slope2, F32)
    hi = x.astype(BF16).astype(F32)
    mid = (x - hi).astype(BF16).astype(F32)
    lo = x - hi - mid
    part = jnp.where((fr == 0) | (fr == 3), hi, jnp.where((fr == 1) | (fr == 4), mid, lo))
    feat = jnp.where(fr < 3, part, jnp.where(fr < N_ALIBI_FEATURES, part * BF16_EXACT_INT, 0.0))
    feat_ref[...] = feat.astype(BF16)
    kk = lax.broadcasted_iota(jnp.int32, (tk, w), 0)
    qq = lax.broadcasted_iota(jnp.int32, (tk, w), 1)
    qq = jnp.where(qq >= tq, qq - tq, qq)
    ahead = (kk - qq).astype(F32)
    same_chunk = (kk >> CHUNK_SHIFT) == (qq >> CHUNK_SHIFT)
    corr_ref[...] = jnp.where(kk <= qq, 0.0,
                              jnp.where(same_chunk, -2.0 * slope2 * ahead, MASK_VALUE))

    def init_query_block(qi, carry):
        qT = qT_ref[0, 0, qi]
        row = lax.broadcasted_iota(jnp.int32, qT.shape, 0)
        zero = jnp.zeros_like(qT)
        qs_ref[qi, :, 0:tq] = jnp.where(row < DA_HEAD_DIM, qT, zero)
        qs_ref[qi, :, tq:w] = jnp.where(row >= DA_HEAD_DIM, qT, zero)
        qf = qT.astype(F32)
        q_norm2 = jnp.max(jnp.sum(qf * qf, axis=0, keepdims=True), axis=1, keepdims=True)
        qn_ref[qi] = jnp.sqrt(q_norm2)[0, 0]
        kf32 = k_ref[0, 0, pl.ds(pl.multiple_of(qi * tk, tk), tk), :].astype(F32)
        k_norm2 = jnp.max(jnp.sum(kf32 * kf32, axis=1, keepdims=True), axis=0, keepdims=True)
        kn_ref[qi] = jnp.sqrt(k_norm2)[0, 0]
        return carry

    lax.fori_loop(0, nq, init_query_block, 0)
    ones_rows = jnp.ones((ATTN_SUM_ROWS, tk), BF16)

    def qk(pair, t, s_slot, diag):
        qi, kj = pair(t)
        start = kj * tk if isinstance(kj, int) else pl.multiple_of(kj * tk, tk)
        ka = jnp.concatenate([k_ref[0, 0, pl.ds(start, tk), :], kf_ref[pl.ds(start, tk), :]],
                             axis=1)
        qa = jnp.concatenate([qs_ref[qi], feat_ref[...]], axis=0)
        s = jnp.dot(ka, qa, preferred_element_type=F32)
        if diag:
            s = s + corr_ref[...]
        s_refs[s_slot][...] = s
        mc_refs[s_slot][0:1, :] = jnp.max(s, axis=0, keepdims=True)

    def sm(pair, t, s_slot, p_slot, first):
        qi, _ = pair(t)
        if first:
            m_next = mc_refs[s_slot][0:1, :]
        else:
            m_prev = m_ref[qi]
            m_next = jnp.maximum(m_prev, mc_refs[s_slot][0:1, :])
            al_refs[p_slot][0:1, :] = jnp.exp2(m_prev - m_next)
        m_ref[qi] = m_next
        m_rows = jnp.broadcast_to(m_next, (rc, w))
        for c in range(n_chunks):
            p = jnp.exp2(s_refs[s_slot][c * rc:(c + 1) * rc, :] - m_rows)
            p_refs[p_slot][c * rc:(c + 1) * rc, :] = p.astype(BF16)

    def pv(pair, t, p_slot, first):
        qi, kj = pair(t)
        v_aug = jnp.concatenate([vT_ref[0, 0, kj], ones_rows], axis=0)
        upd = jnp.dot(v_aug, p_refs[p_slot][...], preferred_element_type=F32)
        if first:
            acc_ref[qi] = upd
        else:
            acc_ref[qi] = al_refs[p_slot][0:1, :] * acc_ref[qi] + upd

    def run_pipeline(pair, n, steps, diag):
        def time_step(tau, u, do_qk=True, do_sm=True, do_pv=True):
            if do_qk:
                qk(pair, tau, u, diag)
            if do_sm:
                sm(pair, tau - 2, (u + 2) % ATTN_SCORE_SLOTS, u % ATTN_PROB_SLOTS, diag)
            if do_pv:
                pv(pair, tau - 3, (u + 1) % ATTN_PROB_SLOTS, diag)

        time_step(0, 0, do_sm=False, do_pv=False)
        time_step(1, 1, do_sm=False, do_pv=False)
        time_step(2, 2, do_pv=False)

        def body(i, carry):
            for u in range(steps):
                time_step(ATTN_PIPE_DEPTH + steps * i + u, (ATTN_PIPE_DEPTH + u) % ATTN_SCORE_SLOTS)
            return carry

        lax.fori_loop(0, n // steps - 1, body, 0)
        for u in range(steps - ATTN_PIPE_DEPTH):
            time_step(n - (steps - ATTN_PIPE_DEPTH) + u, (ATTN_PIPE_DEPTH + u) % ATTN_SCORE_SLOTS)
        time_step(n, 0, do_qk=False)
        time_step(n + 1, 1, do_qk=False)
        time_step(n + 2, 2, do_qk=False, do_sm=False)

    run_pipeline(lambda t: (t, t), nq, ATTN_DIAG_STEPS, diag=True)

    for qi in range(nq):
        thr_ref[qi] = jnp.min(m_ref[qi]) - (ATTN_ZERO_EXP2 + ATTN_BOUND_SLACK)

    def choose(i, carry):
        count, pad_q, pad_k = carry
        qi, kj = qb_ref[nq + i], kb_ref[nq + i]
        last_key = ((kj + 1) * tk - 1).astype(F32)
        bound = qn_ref[qi] * kn_ref[kj] * 1.001 + slope2 * last_key
        keep = bound >= thr_ref[qi]

        @pl.when(keep)
        def _():
            ql_ref[count] = qi
            kl_ref[count] = kj

        return (count + keep.astype(jnp.int32), jnp.where(keep, pad_q, qi),
                jnp.where(keep, pad_k, kj))

    n_keep, pad_q, pad_k = lax.fori_loop(0, n_past, choose,
                                         (jnp.int32(0), jnp.int32(0), jnp.int32(0)))
    n_pad = (-n_keep) & (ATTN_PAST_STEPS - 1)
    for r in range(ATTN_PAST_STEPS - 1):
        @pl.when(r < n_pad)
        def _():
            ql_ref[n_keep + r] = pad_q
            kl_ref[n_keep + r] = pad_k

    n_visit = n_keep + n_pad

    @pl.when(n_visit > 0)
    def _():
        run_pipeline(lambda t: (ql_ref[t], kl_ref[t]), n_visit, ATTN_PAST_STEPS, diag=False)

    lv = lamv_ref[...]
    lam = (jnp.exp(jnp.sum(lv[0:1] * lv[1:2], axis=-1, keepdims=True))
           - jnp.exp(jnp.sum(lv[2:3] * lv[3:4], axis=-1, keepdims=True)) + lam_init)

    def finish_query_block(qi, carry):
        o = acc_ref[qi, 0:DA_V_DIM, :] / acc_ref[qi, DA_V_DIM:DA_V_DIM + 1, :]
        out = o[:, 0:tq] - lam * o[:, tq:w]
        ms = jnp.mean(out * out, axis=0, keepdims=True)
        out = out * lax.rsqrt(ms + HEAD_NORM_EPS) * subln_ref[...] * (1.0 - lam_init)
        rows = pl.ds(pl.multiple_of(qi * tq, tq), tq)
        g = g_ref[0, rows, :].astype(F32)
        y_ref[0, rows, :] = (g * jax.nn.sigmoid(g) * out.T).astype(BF16)
        return carry

    lax.fori_loop(0, nq, finish_query_block, 0)


def _diff_attention(qT, k, vT, g, slopes, lamv, subln, lam_init):
    b, nh, nq, dk, tq = qT.shape
    s = k.shape[2]
    w = 2 * tq
    qb, kb = _attn_schedule(nq)
    kernel = functools.partial(_attn_kernel, tq=tq, nq=nq, lam_init=lam_init)
    smem = pl.BlockSpec(memory_space=pltpu.SMEM)
    once = pl.Buffered(1)
    return pl.pallas_call(
        kernel,
        grid=(b, nh),
        in_specs=[
            smem, smem, smem,
            pl.BlockSpec((1, 1, nq, dk, tq), lambda bi, h: (bi, h, 0, 0, 0), pipeline_mode=once),
            pl.BlockSpec((1, 1, s, dk), lambda bi, h: (bi, h, 0, 0)),
            pl.BlockSpec((s, LANES), lambda bi, h: (0, 0), pipeline_mode=once),
            pl.BlockSpec((1, 1, nq, DA_V_DIM, tq), lambda bi, h: (bi, h, 0, 0, 0)),
            pl.BlockSpec((1, s, DA_V_DIM), lambda bi, h: (bi, 0, h)),
            pl.BlockSpec((ATTN_LAMBDA_ROWS, LANES), lambda bi, h: (0, 0)),
            pl.BlockSpec((DA_V_DIM, 1), lambda bi, h: (0, 0)),
        ],
        out_specs=pl.BlockSpec((1, s, DA_V_DIM), lambda bi, h: (bi, 0, h)),
        out_shape=jax.ShapeDtypeStruct((b, s, nh * DA_V_DIM), BF16),
        scratch_shapes=(
            [pltpu.VMEM((nq, LANES, w), BF16),
             pltpu.VMEM((LANES, w), BF16)]
            + [pltpu.VMEM((tq, w), F32)] * ATTN_SCORE_SLOTS
            + [pltpu.VMEM((ATTN_STAT_ROWS, w), F32)] * ATTN_SCORE_SLOTS
            + [pltpu.VMEM((tq, w), BF16)] * ATTN_PROB_SLOTS
            + [pltpu.VMEM((ATTN_STAT_ROWS, w), F32)] * ATTN_PROB_SLOTS
            + [pltpu.VMEM((nq, 1, w), F32),
               pltpu.VMEM((nq, DA_V_DIM + ATTN_SUM_ROWS, w), F32),
               pltpu.VMEM((tq, w), F32),
               pltpu.SMEM((nq,), F32),
               pltpu.SMEM((nq,), F32),
               pltpu.SMEM((nq,), F32),
               pltpu.SMEM((len(qb),), jnp.int32),
               pltpu.SMEM((len(qb),), jnp.int32)]
        ),
        compiler_params=_cparams(("arbitrary", "arbitrary")),
        name="diff_attention",
    )(slopes, jnp.asarray(qb), jnp.asarray(kb), qT, k,
      _alibi_key_features(s), vT, g, lamv, subln.reshape(DA_V_DIM, 1))


def _ret_kernel(lg_ref, q_ref, k_ref, v_ref, g_ref, y_ref, state_ref, decay_ref, qdec_ref,
                kdec_ref, *, blk):
    n = pl.program_id(1)
    dk, dv = RET_QK_DIM, RET_V_DIM

    @pl.when(n == 0)
    def _():
        state_ref[...] = jnp.zeros(state_ref.shape, F32)
        ti = lax.broadcasted_iota(jnp.int32, (blk, blk), 0)
        ui = lax.broadcasted_iota(jnp.int32, (blk, blk), 1)
        dist = jnp.abs(ti - ui).astype(F32)
        allowed = (ui >> CHUNK_SHIFT) <= (ti >> CHUNK_SHIFT)
        t = lax.broadcasted_iota(jnp.int32, (blk, dk), 0).astype(F32)
        for h in range(RET_HEADS):
            decay_ref[h] = jnp.where(allowed, jnp.exp(lg_ref[h] * dist), 0.0)
            qdec_ref[h] = jnp.exp(lg_ref[h] * t)
            kdec_ref[h] = jnp.exp(lg_ref[h] * (blk - t)) * dk ** -0.5

    for h in range(RET_HEADS):
        lg = lg_ref[h]
        q = q_ref[0, :, h * dk:(h + 1) * dk]
        k = k_ref[0, :, h * dk:(h + 1) * dk] * jnp.asarray(dk ** -0.5, BF16)
        v = v_ref[0, :, h * dv:(h + 1) * dv]
        qd = (q.astype(F32) * qdec_ref[h]).astype(BF16)
        kd = (k_ref[0, :, h * dk:(h + 1) * dk].astype(F32) * kdec_ref[h]).astype(BF16)

        a = lax.dot_general(q, k, (((1,), (1,)), ((), ())), preferred_element_type=F32)
        a = (a * decay_ref[h]).astype(BF16)
        out = jnp.dot(a, v, preferred_element_type=F32)
        out = out + jnp.dot(qd, state_ref[h].astype(BF16), preferred_element_type=F32)
        kv = lax.dot_general(kd, v, (((0,), (0,)), ((), ())), preferred_element_type=F32)
        block_decay = jnp.exp(lg * jnp.full((1, 1), float(blk), F32))
        state_ref[h] = block_decay * state_ref[h] + kv

        ms = jnp.mean(out * out, axis=-1, keepdims=True)
        out = out * lax.rsqrt(ms + HEAD_NORM_EPS)
        g = g_ref[0, :, h * dv:(h + 1) * dv].astype(F32)
        y_ref[0, :, h * dv:(h + 1) * dv] = (g * jax.nn.sigmoid(g) * out).astype(BF16)


def _retention(q, k, v, g, log_gammas, blk):
    b, s, qk_w = q.shape
    v_w = v.shape[-1]
    return pl.pallas_call(
        functools.partial(_ret_kernel, blk=blk),
        grid=(b, s // blk),
        in_specs=[
            pl.BlockSpec(memory_space=pltpu.SMEM),
            pl.BlockSpec((1, blk, qk_w), lambda bi, i: (bi, i, 0)),
            pl.BlockSpec((1, blk, qk_w), lambda bi, i: (bi, i, 0)),
            pl.BlockSpec((1, blk, v_w), lambda bi, i: (bi, i, 0)),
            pl.BlockSpec((1, blk, v_w), lambda bi, i: (bi, i, 0)),
        ],
        out_specs=pl.BlockSpec((1, blk, v_w), lambda bi, i: (bi, i, 0)),
        out_shape=jax.ShapeDtypeStruct((b, s, v_w), BF16),
        scratch_shapes=[
            pltpu.VMEM((RET_HEADS, RET_QK_DIM, RET_V_DIM), F32),
            pltpu.VMEM((RET_HEADS, blk, blk), F32),
            pltpu.VMEM((RET_HEADS, blk, RET_QK_DIM), F32),
            pltpu.VMEM((RET_HEADS, blk, RET_QK_DIM), F32),
        ],
        compiler_params=_cparams(("arbitrary", "arbitrary")),
        name="retention",
    )(log_gammas, q, k, v, g)


def kernel(x, c, ada_w, ada_b, pre_gain, post_gain, da_w_in, da_w_out, da_lambda_q1, da_lambda_k1,
           da_lambda_q2, da_lambda_k2, da_subln_gain, ret_w_in, ret_w_out):
    b, s, d = x.shape
    mod = _modulation(c, ada_w, ada_b)

    attn_tile = 512
    qk_w = 2 * DA_HEADS * DA_HEAD_DIM
    v_w = DA_HEADS * DA_V_DIM
    q_scale = LOG2E * DA_HEAD_DIM ** -0.5
    da_plan = (("headT", 0, DA_HEADS, 2 * DA_HEAD_DIM, q_scale),
               ("head", qk_w, DA_HEADS, 2 * DA_HEAD_DIM, 1.0),
               ("headT", 2 * qk_w, DA_HEADS, DA_V_DIM, 1.0),
               ("tok", 2 * qk_w + v_w, None, v_w, 1.0))
    mod0 = mod[0].reshape(b, 1, 3 * d)
    qT, k, vT, g = _in_projection(x, mod0, pre_gain[0:1], da_w_in[0].astype(BF16), da_plan,
                                  tm=attn_tile, slab=256)
    lam_init = 0.8 - 0.6 * math.exp(-0.3 * 0)
    slopes = jnp.asarray([2.0 ** (-8.0 * (hh + 1) / DA_HEADS) for hh in range(DA_HEADS)], F32)
    lamv = jnp.stack([da_lambda_q1[0], da_lambda_k1[0], da_lambda_q2[0], da_lambda_k2[0]])
    lamv = jnp.pad(lamv, ((0, ATTN_LAMBDA_ROWS - 4), (0, LANES - DA_HEAD_DIM)))
    y = _diff_attention(qT, k, vT, g, slopes, lamv, da_subln_gain[0], lam_init)
    x = _out_projection(y, da_w_out[0].astype(BF16), x, mod0, post_gain[0:1], tm=512)

    rqk = RET_HEADS * RET_QK_DIM
    rv = RET_HEADS * RET_V_DIM
    ret_plan = (("tok", 0, None, rqk, 1.0), ("tok", rqk, None, rqk, 1.0),
                ("tok", 2 * rqk, None, rv, 1.0), ("tok", 2 * rqk + rv, None, rv, 1.0))
    mod1 = mod[1].reshape(b, 1, 3 * d)
    q, k, v, g = _in_projection(x, mod1, pre_gain[1:2], ret_w_in[0].astype(BF16), ret_plan,
                                tm=256, slab=256)
    gammas = 1.0 - 2.0 ** (-5.0 - jnp.arange(RET_HEADS, dtype=F32))
    y = _retention(q, k, v, g, jnp.log(gammas), blk=256)
    return _out_projection(y, ret_w_out[0].astype(BF16), x, mod1, post_gain[1:2], tm=512)
```

```python
import functools
import math

import jax
import jax.numpy as jnp
import numpy as np
from jax import lax
from jax.experimental import pallas as pl
from jax.experimental.pallas import tpu as pltpu

CHUNK = 64
CHUNK_SHIFT = 6
DA_HEADS = 8
DA_HEAD_DIM = 64
DA_V_DIM = 128
RET_HEADS = 4
RET_QK_DIM = 256
RET_V_DIM = 512
NORM_EPS = 1e-6
HEAD_NORM_EPS = 1e-5
MASK_VALUE = -1e30
LOG2E = 1.4426950408889634

V7X_VMEM_LIMIT_BYTES = 56 * 1024 * 1024
LANES = 128
SUBLANES = 8
BF16_EXACT_INT = 256

BF16 = jnp.bfloat16
F32 = jnp.float32


def _cparams(sem, vmem=V7X_VMEM_LIMIT_BYTES, flags=None):
    return pltpu.CompilerParams(dimension_semantics=sem, vmem_limit_bytes=vmem, flags=flags)


def _mod_kernel(c_ref, w_ref, b_ref, o_ref):
    c = c_ref[...]
    cond = c * jax.nn.sigmoid(c)
    o_ref[0] = jnp.dot(cond, w_ref[0], preferred_element_type=F32) + b_ref[0]


def _modulation(c, ada_w, ada_b):
    depth, d, d3 = ada_w.shape
    b = c.shape[0]
    n_tiles = d3 // d
    return pl.pallas_call(
        _mod_kernel,
        grid=(depth, n_tiles),
        in_specs=[
            pl.BlockSpec((b, d), lambda l, j: (0, 0)),
            pl.BlockSpec((1, d, d), lambda l, j: (l, 0, j)),
            pl.BlockSpec((1, 1, d), lambda l, j: (l, 0, j)),
        ],
        out_specs=pl.BlockSpec((1, b, d), lambda l, j: (l, 0, j)),
        out_shape=jax.ShapeDtypeStruct((depth, b, d3), F32),
        compiler_params=_cparams(("arbitrary", "arbitrary")),
        name="modulation",
    )(c, ada_w, ada_b.reshape(depth, 1, d3))


def _inproj_kernel(x_ref, mod_ref, gain_ref, w_ref, *refs, d_model, plan, slab):
    out_refs = refs[:len(plan)]
    hb_ref = refs[len(plan)]
    x = x_ref[0]
    ms = jnp.mean(x * x, axis=-1, keepdims=True)
    hn = x * lax.rsqrt(ms + NORM_EPS) * gain_ref[...]
    shift = mod_ref[0, :, 0:d_model]
    scale = mod_ref[0, :, d_model:2 * d_model]
    hb_ref[...] = (hn * (1.0 + scale) + shift).astype(BF16)
    for (kind, c0, n_heads, width, out_scale), o_ref in zip(plan, out_refs):
        total = width if kind == "tok" else n_heads * width
        for s0 in range(0, total, slab):
            r = jnp.dot(hb_ref[...], w_ref[:, c0 + s0:c0 + s0 + slab],
                        preferred_element_type=F32)
            if out_scale != 1.0:
                r = r * out_scale
            if kind == "tok":
                o_ref[0, :, s0:s0 + slab] = r.astype(BF16)
            elif kind == "head" and width >= slab:
                hh, off = divmod(s0, width)
                o_ref[0, hh, :, off:off + slab] = r.astype(BF16)
            elif kind == "head":
                for p in range(slab // width):
                    o_ref[0, s0 // width + p] = r[:, p * width:(p + 1) * width].astype(BF16)
            else:
                for p in range(slab // width):
                    o_ref[0, s0 // width + p, 0] = r[:, p * width:(p + 1) * width].T.astype(BF16)


def _in_projection(x, mod, gain, w_bf16, plan, tm, slab):
    b, s, d = x.shape
    n = w_bf16.shape[1]
    out_shapes, out_specs = [], []
    for (kind, c0, n_heads, width, _) in plan:
        if kind == "tok":
            out_shapes.append(jax.ShapeDtypeStruct((b, s, width), BF16))
            out_specs.append(pl.BlockSpec((1, tm, width), lambda bi, i: (bi, i, 0)))
        elif kind == "head":
            out_shapes.append(jax.ShapeDtypeStruct((b, n_heads, s, width), BF16))
            out_specs.append(pl.BlockSpec((1, n_heads, tm, width), lambda bi, i: (bi, 0, i, 0)))
        else:
            assert slab % width == 0
            out_shapes.append(jax.ShapeDtypeStruct((b, n_heads, s // tm, width, tm), BF16))
            out_specs.append(pl.BlockSpec((1, n_heads, 1, width, tm),
                                          lambda bi, i: (bi, 0, i, 0, 0)))
    return pl.pallas_call(
        functools.partial(_inproj_kernel, d_model=d, plan=plan, slab=slab),
        grid=(b, s // tm),
        in_specs=[
            pl.BlockSpec((1, tm, d), lambda bi, i: (bi, i, 0)),
            pl.BlockSpec((1, 1, 3 * d), lambda bi, i: (bi, 0, 0)),
            pl.BlockSpec((1, d), lambda bi, i: (0, 0)),
            pl.BlockSpec((d, n), lambda bi, i: (0, 0)),
        ],
        out_specs=out_specs,
        out_shape=out_shapes,
        scratch_shapes=[pltpu.VMEM((tm, d), BF16)],
        compiler_params=_cparams(("arbitrary", "arbitrary")),
        name="in_projection",
    )(x, mod, gain, w_bf16)


def _outproj_kernel(y_ref, w_ref, x_ref, mod_ref, gain_ref, o_ref, *, d_model):
    t = jnp.dot(y_ref[0], w_ref[...], preferred_element_type=F32)
    ms = jnp.mean(t * t, axis=-1, keepdims=True)
    n = t * lax.rsqrt(ms + NORM_EPS) * gain_ref[...]
    gate = mod_ref[0, :, 2 * d_model:3 * d_model]
    o_ref[0] = x_ref[0] + gate * n


def _out_projection(y, w_bf16, x, mod, gain, tm):
    b, s, d = x.shape
    k = y.shape[-1]
    return pl.pallas_call(
        functools.partial(_outproj_kernel, d_model=d),
        grid=(b, s // tm),
        in_specs=[
            pl.BlockSpec((1, tm, k), lambda bi, i: (bi, i, 0)),
            pl.BlockSpec((k, d), lambda bi, i: (0, 0)),
            pl.BlockSpec((1, tm, d), lambda bi, i: (bi, i, 0)),
            pl.BlockSpec((1, 1, 3 * d), lambda bi, i: (bi, 0, 0)),
            pl.BlockSpec((1, d), lambda bi, i: (0, 0)),
        ],
        out_specs=pl.BlockSpec((1, tm, d), lambda bi, i: (bi, i, 0)),
        out_shape=jax.ShapeDtypeStruct((b, s, d), F32),
        compiler_params=_cparams(("arbitrary", "arbitrary")),
        name="out_projection",
    )(y, w_bf16, x, mod, gain)


N_ALIBI_FEATURES = 6
ATTN_ROW_CHUNK = 16
ATTN_SUM_ROWS = 16
ATTN_SCORE_SLOTS = 4
ATTN_PROB_SLOTS = 2
ATTN_DIAG_STEPS = 8
ATTN_PAST_STEPS = 8
ATTN_PIPE_DEPTH = 3
ATTN_LAMBDA_ROWS = 32
ATTN_STAT_ROWS = 4
ATTN_ZERO_EXP2 = 150.0
ATTN_BOUND_SLACK = 1.0


def _alibi_key_features(s):
    pos = jnp.arange(s, dtype=jnp.int32)[:, None]
    col = jnp.arange(LANES, dtype=jnp.int32)[None, :]
    lo = (pos % BF16_EXACT_INT).astype(F32)
    hi = (pos // BF16_EXACT_INT).astype(F32)
    feat = jnp.where(col < 3, lo, jnp.where(col < N_ALIBI_FEATURES, hi, 0.0))
    return feat.astype(BF16)


def _attn_schedule(nq):
    pairs = [(qi, qi) for qi in range(nq)] + [(qi, j) for qi in range(nq) for j in range(qi)]
    qb, kb = zip(*pairs)
    return np.asarray(qb, np.int32), np.asarray(kb, np.int32)


def _attn_kernel(slope_ref, qb_ref, kb_ref, qT_ref, k_ref, kf_ref, vT_ref, g_ref,
                 lamv_ref, subln_ref, y_ref,
                 qs_ref, feat_ref, s0_ref, s1_ref, s2_ref, s3_ref, mc0_ref, mc1_ref, mc2_ref,
                 mc3_ref, p0_ref, p1_ref, al0_ref, al1_ref, m_ref, acc_ref, corr_ref,
                 qn_ref, kn_ref, thr_ref, ql_ref, kl_ref,
                 *, tq, nq, lam_init):
    h = pl.program_id(1)
    tk = tq
    w = 2 * tq
    rc = ATTN_ROW_CHUNK
    n_chunks = tk // rc
    n_pos = nq * (nq + 1) // 2
    s_refs = (s0_ref, s1_ref, s2_ref, s3_ref)
    mc_refs = (mc0_ref, mc1_ref, mc2_ref, mc3_ref)
    p_refs, al_refs = (p0_ref, p1_ref), (al0_ref, al1_ref)
    n_past = n_pos - nq
    assert nq % ATTN_DIAG_STEPS == 0 and nq >= ATTN_DIAG_STEPS
    assert ATTN_DIAG_STEPS % ATTN_SCORE_SLOTS == 0 and ATTN_PAST_STEPS % ATTN_SCORE_SLOTS == 0
    assert n_past % ATTN_PAST_STEPS == 0

    slope2 = slope_ref[h] * LOG2E
    fr = lax.broadcasted_iota(jnp.int32, (LANES, w), 0)
    x = jnp.full((LANES, w), slope2, F32)
    hi = x.astype(BF16).astype(F32)
    mid = (x - hi).astype(BF16).astype(F32)
    lo = x - hi - mid
    part = jnp.where((fr == 0) | (fr == 3), hi, jnp.where((fr == 1) | (fr == 4), mid, lo))
    feat = jnp.where(fr < 3, part, jnp.where(fr < N_ALIBI_FEATURES, part * BF16_EXACT_INT, 0.0))
    feat_ref[...] = feat.astype(BF16)
    kk = lax.broadcasted_iota(jnp.int32, (tk, w), 0)
    qq = lax.broadcasted_iota(jnp.int32, (tk, w), 1)
    qq = jnp.where(qq >= tq, qq - tq, qq)
    ahead = (kk - qq).astype(F32)
    same_chunk = (kk >> CHUNK_SHIFT) == (qq >> CHUNK_SHIFT)
    corr_ref[...] = jnp.where(kk <= qq, 0.0,
                              jnp.where(same_chunk, -2.0 * slope2 * ahead, MASK_VALUE))

    def init_query_block(qi, carry):
        qT = qT_ref[0, 0, qi]
        row = lax.broadcasted_iota(jnp.int32, qT.shape, 0)
        zero = jnp.zeros_like(qT)
        qs_ref[qi, :, 0:tq] = jnp.where(row < DA_HEAD_DIM, qT, zero)
        qs_ref[qi, :, tq:w] = jnp.where(row >= DA_HEAD_DIM, qT, zero)
        qf = qT.astype(F32)
        q_norm2 = jnp.max(jnp.sum(qf * qf, axis=0, keepdims=True), axis=1, keepdims=True)
        qn_ref[qi] = jnp.sqrt(q_norm2)[0, 0]
        kf32 = k_ref[0, 0, pl.ds(pl.multiple_of(qi * tk, tk), tk), :].astype(F32)
        k_norm2 = jnp.max(jnp.sum(kf32 * kf32, axis=1, keepdims=True), axis=0, keepdims=True)
        kn_ref[qi] = jnp.sqrt(k_norm2)[0, 0]
        return carry

    lax.fori_loop(0, nq, init_query_block, 0)
    ones_rows = jnp.ones((ATTN_SUM_ROWS, tk), BF16)

    def qk(pair, t, s_slot, diag):
        qi, kj = pair(t)
        start = kj * tk if isinstance(kj, int) else pl.multiple_of(kj * tk, tk)
        ka = jnp.concatenate([k_ref[0, 0, pl.ds(start, tk), :], kf_ref[pl.ds(start, tk), :]],
                             axis=1)
        qa = jnp.concatenate([qs_ref[qi], feat_ref[...]], axis=0)
        s = jnp.dot(ka, qa, preferred_element_type=F32)
        if diag:
            s = s + corr_ref[...]
        s_refs[s_slot][...] = s
        mc_refs[s_slot][0:1, :] = jnp.max(s, axis=0, keepdims=True)

    def sm(pair, t, s_slot, p_slot, first):
        qi, _ = pair(t)
        if first:
            m_next = mc_refs[s_slot][0:1, :]
        else:
            m_prev = m_ref[qi]
            m_next = jnp.maximum(m_prev, mc_refs[s_slot][0:1, :])
            al_refs[p_slot][0:1, :] = jnp.exp2(m_prev - m_next)
        m_ref[qi] = m_next
        m_rows = jnp.broadcast_to(m_next, (rc, w))
        for c in range(n_chunks):
            p = jnp.exp2(s_refs[s_slot][c * rc:(c + 1) * rc, :] - m_rows)
            p_refs[p_slot][c * rc:(c + 1) * rc, :] = p.astype(BF16)

    def pv(pair, t, p_slot, first):
        qi, kj = pair(t)
        v_aug = jnp.concatenate([vT_ref[0, 0, kj], ones_rows], axis=0)
        upd = jnp.dot(v_aug, p_refs[p_slot][...], preferred_element_type=F32)
        if first:
            acc_ref[qi] = upd
        else:
            acc_ref[qi] = al_refs[p_slot][0:1, :] * acc_ref[qi] + upd

    def run_pipeline(pair, n, steps, diag):
        def time_step(tau, u, do_qk=True, do_sm=True, do_pv=True):
            if do_qk:
                qk(pair, tau, u, diag)
            if do_sm:
                sm(pair, tau - 2, (u + 2) % ATTN_SCORE_SLOTS, u % ATTN_PROB_SLOTS, diag)
            if do_pv:
                pv(pair, tau - 3, (u + 1) % ATTN_PROB_SLOTS, diag)

        time_step(0, 0, do_sm=False, do_pv=False)
        time_step(1, 1, do_sm=False, do_pv=False)
        time_step(2, 2, do_pv=False)

        def body(i, carry):
            for u in range(steps):
                time_step(ATTN_PIPE_DEPTH + steps * i + u, (ATTN_PIPE_DEPTH + u) % ATTN_SCORE_SLOTS)
            return carry

        lax.fori_loop(0, n // steps - 1, body, 0)
        for u in range(steps - ATTN_PIPE_DEPTH):
            time_step(n - (steps - ATTN_PIPE_DEPTH) + u, (ATTN_PIPE_DEPTH + u) % ATTN_SCORE_SLOTS)
        time_step(n, 0, do_qk=False)
        time_step(n + 1, 1, do_qk=False)
        time_step(n + 2, 2, do_qk=False, do_sm=False)

    run_pipeline(lambda t: (t, t), nq, ATTN_DIAG_STEPS, diag=True)

    for qi in range(nq):
        thr_ref[qi] = jnp.min(m_ref[qi]) - (ATTN_ZERO_EXP2 + ATTN_BOUND_SLACK)

    def choose(i, carry):
        count, pad_q, pad_k = carry
        qi, kj = qb_ref[nq + i], kb_ref[nq + i]
        last_key = ((kj + 1) * tk - 1).astype(F32)
        bound = qn_ref[qi] * kn_ref[kj] * 1.001 + slope2 * last_key
        keep = bound >= thr_ref[qi]

        @pl.when(keep)
        def _():
            ql_ref[count] = qi
            kl_ref[count] = kj

        return (count + keep.astype(jnp.int32), jnp.where(keep, pad_q, qi),
                jnp.where(keep, pad_k, kj))

    n_keep, pad_q, pad_k = lax.fori_loop(0, n_past, choose,
                                         (jnp.int32(0), jnp.int32(0), jnp.int32(0)))
    n_pad = (-n_keep) & (ATTN_PAST_STEPS - 1)
    for r in range(ATTN_PAST_STEPS - 1):
        @pl.when(r < n_pad)
        def _():
            ql_ref[n_keep + r] = pad_q
            kl_ref[n_keep + r] = pad_k

    n_visit = n_keep + n_pad

    @pl.when(n_visit > 0)
    def _():
        run_pipeline(lambda t: (ql_ref[t], kl_ref[t]), n_visit, ATTN_PAST_STEPS, diag=False)

    lv = lamv_ref[...]
    lam = (jnp.exp(jnp.sum(lv[0:1] * lv[1:2], axis=-1, keepdims=True))
           - jnp.exp(jnp.sum(lv[2:3] * lv[3:4], axis=-1, keepdims=True)) + lam_init)

    def finish_query_block(qi, carry):
        o = acc_ref[qi, 0:DA_V_DIM, :] / acc_ref[qi, DA_V_DIM:DA_V_DIM + 1, :]
        out = o[:, 0:tq] - lam * o[:, tq:w]
        ms = jnp.mean(out * out, axis=0, keepdims=True)
        out = out * lax.rsqrt(ms + HEAD_NORM_EPS) * subln_ref[...] * (1.0 - lam_init)
        rows = pl.ds(pl.multiple_of(qi * tq, tq), tq)
        g = g_ref[0, rows, :].astype(F32)
        y_ref[0, rows, :] = (g * jax.nn.sigmoid(g) * out.T).astype(BF16)
        return carry

    lax.fori_loop(0, nq, finish_query_block, 0)


def _diff_attention(qT, k, vT, g, slopes, lamv, subln, lam_init):
    b, nh, nq, dk, tq = qT.shape
    s = k.shape[2]
    w = 2 * tq
    qb, kb = _attn_schedule(nq)
    kernel = functools.partial(_attn_kernel, tq=tq, nq=nq, lam_init=lam_init)
    smem = pl.BlockSpec(memory_space=pltpu.SMEM)
    once = pl.Buffered(1)
    return pl.pallas_call(
        kernel,
        grid=(b, nh),
        in_specs=[
            smem, smem, smem,
            pl.BlockSpec((1, 1, nq, dk, tq), lambda bi, h: (bi, h, 0, 0, 0), pipeline_mode=once),
            pl.BlockSpec((1, 1, s, dk), lambda bi, h: (bi, h, 0, 0)),
            pl.BlockSpec((s, LANES), lambda bi, h: (0, 0), pipeline_mode=once),
            pl.BlockSpec((1, 1, nq, DA_V_DIM, tq), lambda bi, h: (bi, h, 0, 0, 0)),
            pl.BlockSpec((1, s, DA_V_DIM), lambda bi, h: (bi, 0, h)),
            pl.BlockSpec((ATTN_LAMBDA_ROWS, LANES), lambda bi, h: (0, 0)),
            pl.BlockSpec((DA_V_DIM, 1), lambda bi, h: (0, 0)),
        ],
        out_specs=pl.BlockSpec((1, s, DA_V_DIM), lambda bi, h: (bi, 0, h)),
        out_shape=jax.ShapeDtypeStruct((b, s, nh * DA_V_DIM), BF16),
        scratch_shapes=(
            [pltpu.VMEM((nq, LANES, w), BF16),
             pltpu.VMEM((LANES, w), BF16)]
            + [pltpu.VMEM((tq, w), F32)] * ATTN_SCORE_SLOTS
            + [pltpu.VMEM((ATTN_STAT_ROWS, w), F32)] * ATTN_SCORE_SLOTS
            + [pltpu.VMEM((tq, w), BF16)] * ATTN_PROB_SLOTS
            + [pltpu.VMEM((ATTN_STAT_ROWS, w), F32)] * ATTN_PROB_SLOTS
            + [pltpu.VMEM((nq, 1, w), F32),
               pltpu.VMEM((nq, DA_V_DIM + ATTN_SUM_ROWS, w), F32),
               pltpu.VMEM((tq, w), F32),
               pltpu.SMEM((nq,), F32),
               pltpu.SMEM((nq,), F32),
               pltpu.SMEM((nq,), F32),
               pltpu.SMEM((len(qb),), jnp.int32),
               pltpu.SMEM((len(qb),), jnp.int32)]
        ),
        compiler_params=_cparams(("arbitrary", "arbitrary")),
        name="diff_attention",
    )(slopes, jnp.asarray(qb), jnp.asarray(kb), qT, k,
      _alibi_key_features(s), vT, g, lamv, subln.reshape(DA_V_DIM, 1))


def _ret_kernel(lg_ref, q_ref, k_ref, v_ref, g_ref, y_ref, state_ref, decay_ref, qdec_ref,
                kdec_ref, *, blk):
    n = pl.program_id(1)
    dk, dv = RET_QK_DIM, RET_V_DIM

    @pl.when(n == 0)
    def _():
        state_ref[...] = jnp.zeros(state_ref.shape, F32)
        ti = lax.broadcasted_iota(jnp.int32, (blk, blk), 0)
        ui = lax.broadcasted_iota(jnp.int32, (blk, blk), 1)
        dist = jnp.abs(ti - ui).astype(F32)
        allowed = (ui >> CHUNK_SHIFT) <= (ti >> CHUNK_SHIFT)
        t = lax.broadcasted_iota(jnp.int32, (blk, dk), 0).astype(F32)
        for h in range(RET_HEADS):
            decay_ref[h] = jnp.where(allowed, jnp.exp(lg_ref[h] * dist), 0.0) * dk ** -0.5
            qdec_ref[h] = jnp.exp(lg_ref[h] * t)
            kdec_ref[h] = jnp.exp(lg_ref[h] * (blk - t)) * dk ** -0.5

    for h in range(RET_HEADS):
        lg = lg_ref[h]
        q = q_ref[0, :, h * dk:(h + 1) * dk]
        k = k_ref[0, :, h * dk:(h + 1) * dk]
        v = v_ref[0, :, h * dv:(h + 1) * dv]
        qd = (q.astype(F32) * qdec_ref[h]).astype(BF16)
        kd = (k.astype(F32) * kdec_ref[h]).astype(BF16)

        a = lax.dot_general(q, k, (((1,), (1,)), ((), ())), preferred_element_type=F32)
        a = (a * decay_ref[h]).astype(BF16)
        out = jnp.dot(a, v, preferred_element_type=F32)
        out = out + jnp.dot(qd, state_ref[h].astype(BF16), preferred_element_type=F32)
        kv = lax.dot_general(kd, v, (((0,), (0,)), ((), ())), preferred_element_type=F32)
        block_decay = jnp.exp(lg * jnp.full((1, 1), float(blk), F32))
        state_ref[h] = block_decay * state_ref[h] + kv

        ss = jnp.sum(out * out, axis=-1, keepdims=True)
        out = out * lax.rsqrt(ss * (1.0 / dv) + HEAD_NORM_EPS)
        g = g_ref[0, :, h * dv:(h + 1) * dv].astype(F32)
        y_ref[0, :, h * dv:(h + 1) * dv] = (g * jax.nn.sigmoid(g) * out).astype(BF16)


def _retention(q, k, v, g, log_gammas, blk):
    b, s, qk_w = q.shape
    v_w = v.shape[-1]
    return pl.pallas_call(
        functools.partial(_ret_kernel, blk=blk),
        grid=(b, s // blk),
        in_specs=[
            pl.BlockSpec(memory_space=pltpu.SMEM),
            pl.BlockSpec((1, blk, qk_w), lambda bi, i: (bi, i, 0)),
            pl.BlockSpec((1, blk, qk_w), lambda bi, i: (bi, i, 0)),
            pl.BlockSpec((1, blk, v_w), lambda bi, i: (bi, i, 0)),
            pl.BlockSpec((1, blk, v_w), lambda bi, i: (bi, i, 0)),
        ],
        out_specs=pl.BlockSpec((1, blk, v_w), lambda bi, i: (bi, i, 0)),
        out_shape=jax.ShapeDtypeStruct((b, s, v_w), BF16),
        scratch_shapes=[
            pltpu.VMEM((RET_HEADS, RET_QK_DIM, RET_V_DIM), F32),
            pltpu.VMEM((RET_HEADS, blk, blk), F32),
            pltpu.VMEM((RET_HEADS, blk, RET_QK_DIM), F32),
            pltpu.VMEM((RET_HEADS, blk, RET_QK_DIM), F32),
        ],
        compiler_params=_cparams(("arbitrary", "arbitrary")),
        name="retention",
    )(log_gammas, q, k, v, g)


def kernel(x, c, ada_w, ada_b, pre_gain, post_gain, da_w_in, da_w_out, da_lambda_q1, da_lambda_k1,
           da_lambda_q2, da_lambda_k2, da_subln_gain, ret_w_in, ret_w_out):
    b, s, d = x.shape
    mod = _modulation(c, ada_w, ada_b)

    attn_tile = 512
    qk_w = 2 * DA_HEADS * DA_HEAD_DIM
    v_w = DA_HEADS * DA_V_DIM
    q_scale = LOG2E * DA_HEAD_DIM ** -0.5
    da_plan = (("headT", 0, DA_HEADS, 2 * DA_HEAD_DIM, q_scale),
               ("head", qk_w, DA_HEADS, 2 * DA_HEAD_DIM, 1.0),
               ("headT", 2 * qk_w, DA_HEADS, DA_V_DIM, 1.0),
               ("tok", 2 * qk_w + v_w, None, v_w, 1.0))
    mod0 = mod[0].reshape(b, 1, 3 * d)
    qT, k, vT, g = _in_projection(x, mod0, pre_gain[0:1], da_w_in[0].astype(BF16), da_plan,
                                  tm=attn_tile, slab=256)
    lam_init = 0.8 - 0.6 * math.exp(-0.3 * 0)
    slopes = jnp.asarray([2.0 ** (-8.0 * (hh + 1) / DA_HEADS) for hh in range(DA_HEADS)], F32)
    lamv = jnp.stack([da_lambda_q1[0], da_lambda_k1[0], da_lambda_q2[0], da_lambda_k2[0]])
    lamv = jnp.pad(lamv, ((0, ATTN_LAMBDA_ROWS - 4), (0, LANES - DA_HEAD_DIM)))
    y = _diff_attention(qT, k, vT, g, slopes, lamv, da_subln_gain[0], lam_init)
    x = _out_projection(y, da_w_out[0].astype(BF16), x, mod0, post_gain[0:1], tm=512)

    rqk = RET_HEADS * RET_QK_DIM
    rv = RET_HEADS * RET_V_DIM
    ret_plan = (("tok", 0, None, rqk, 1.0), ("tok", rqk, None, rqk, 1.0),
                ("tok", 2 * rqk, None, rv, 1.0), ("tok", 2 * rqk + rv, None, rv, 1.0))
    mod1 = mod[1].reshape(b, 1, 3 * d)
    q, k, v, g = _in_projection(x, mod1, pre_gain[1:2], ret_w_in[0].astype(BF16), ret_plan,
                                tm=512, slab=256)
    gammas = 1.0 - 2.0 ** (-5.0 - jnp.arange(RET_HEADS, dtype=F32))
    y = _retention(q, k, v, g, jnp.log(gammas), blk=256)
    return _out_projection(y, ret_w_out[0].astype(BF16), x, mod1, post_gain[1:2], tm=512)
```

```python
import functools
import math

import jax
import jax.numpy as jnp
import numpy as np
from jax import lax
from jax.experimental import pallas as pl
from jax.experimental.pallas import tpu as pltpu

CHUNK = 64
CHUNK_SHIFT = 6
DA_HEADS = 8
DA_HEAD_DIM = 64
DA_V_DIM = 128
RET_HEADS = 4
RET_QK_DIM = 256
RET_V_DIM = 512
NORM_EPS = 1e-6
HEAD_NORM_EPS = 1e-5
MASK_VALUE = -1e30
LOG2E = 1.4426950408889634

V7X_VMEM_LIMIT_BYTES = 56 * 1024 * 1024
LANES = 128
SUBLANES = 8
BF16_EXACT_INT = 256

BF16 = jnp.bfloat16
F32 = jnp.float32


def _cparams(sem, vmem=V7X_VMEM_LIMIT_BYTES, flags=None):
    return pltpu.CompilerParams(dimension_semantics=sem, vmem_limit_bytes=vmem, flags=flags)


def _mod_kernel(c_ref, w_ref, b_ref, o_ref):
    c = c_ref[...]
    cond = c * jax.nn.sigmoid(c)
    o_ref[0] = jnp.dot(cond, w_ref[0], preferred_element_type=F32) + b_ref[0]


def _modulation(c, ada_w, ada_b):
    depth, d, d3 = ada_w.shape
    b = c.shape[0]
    n_tiles = d3 // d
    return pl.pallas_call(
        _mod_kernel,
        grid=(depth, n_tiles),
        in_specs=[
            pl.BlockSpec((b, d), lambda l, j: (0, 0)),
            pl.BlockSpec((1, d, d), lambda l, j: (l, 0, j)),
            pl.BlockSpec((1, 1, d), lambda l, j: (l, 0, j)),
        ],
        out_specs=pl.BlockSpec((1, b, d), lambda l, j: (l, 0, j)),
        out_shape=jax.ShapeDtypeStruct((depth, b, d3), F32),
        compiler_params=_cparams(("arbitrary", "arbitrary")),
        name="modulation",
    )(c, ada_w, ada_b.reshape(depth, 1, d3))


def _inproj_kernel(x_ref, mod_ref, gain_ref, w_ref, *refs, d_model, plan, slab, n_norm_rows):
    out_refs = refs[:len(plan)]
    norm_ref = refs[len(plan)] if n_norm_rows else None
    hb_ref = refs[-1]
    x = x_ref[0]
    ms = jnp.mean(x * x, axis=-1, keepdims=True)
    hn = x * lax.rsqrt(ms + NORM_EPS) * gain_ref[...]
    shift = mod_ref[0, :, 0:d_model]
    scale = mod_ref[0, :, d_model:2 * d_model]
    hb_ref[...] = (hn * (1.0 + scale) + shift).astype(BF16)
    for (kind, c0, n_heads, width, out_scale, norm_row), o_ref in zip(plan, out_refs):
        total = width if kind == "tok" else n_heads * width
        for s0 in range(0, total, slab):
            r = jnp.dot(hb_ref[...], w_ref[:, c0 + s0:c0 + s0 + slab],
                        preferred_element_type=F32)
            if out_scale != 1.0:
                r = r * out_scale
            if kind == "tok":
                o_ref[0, :, s0:s0 + slab] = r.astype(BF16)
            elif kind == "head" and width >= slab:
                hh, off = divmod(s0, width)
                o_ref[0, hh, :, off:off + slab] = r.astype(BF16)
            else:
                for p in range(slab // width):
                    head = s0 // width + p
                    piece = r[:, p * width:(p + 1) * width]
                    if kind == "head":
                        o_ref[0, head] = piece.astype(BF16)
                    else:
                        o_ref[0, head, 0] = piece.T.astype(BF16)
                    if norm_row is not None:
                        norm2 = jnp.max(jnp.sum(piece * piece, axis=1, keepdims=True),
                                        axis=0, keepdims=True)
                        norm_ref[0, 0, norm_row + head:norm_row + head + 1, :] = (
                            jnp.broadcast_to(jnp.sqrt(norm2), (1, LANES)))


def _in_projection(x, mod, gain, w_bf16, plan, tm, slab):
    b, s, d = x.shape
    n = w_bf16.shape[1]
    out_shapes, out_specs = [], []
    n_norm_rows = sum(e[2] for e in plan if e[5] is not None)
    for (kind, c0, n_heads, width, _, _) in plan:
        if kind == "tok":
            out_shapes.append(jax.ShapeDtypeStruct((b, s, width), BF16))
            out_specs.append(pl.BlockSpec((1, tm, width), lambda bi, i: (bi, i, 0)))
        elif kind == "head":
            out_shapes.append(jax.ShapeDtypeStruct((b, n_heads, s, width), BF16))
            out_specs.append(pl.BlockSpec((1, n_heads, tm, width), lambda bi, i: (bi, 0, i, 0)))
        else:
            assert slab % width == 0
            out_shapes.append(jax.ShapeDtypeStruct((b, n_heads, s // tm, width, tm), BF16))
            out_specs.append(pl.BlockSpec((1, n_heads, 1, width, tm),
                                          lambda bi, i: (bi, 0, i, 0, 0)))
    if n_norm_rows:
        out_shapes.append(jax.ShapeDtypeStruct((b, s // tm, n_norm_rows, LANES), F32))
        out_specs.append(pl.BlockSpec((1, 1, n_norm_rows, LANES), lambda bi, i: (bi, i, 0, 0)))
    return pl.pallas_call(
        functools.partial(_inproj_kernel, d_model=d, plan=plan, slab=slab,
                          n_norm_rows=n_norm_rows),
        grid=(b, s // tm),
        in_specs=[
            pl.BlockSpec((1, tm, d), lambda bi, i: (bi, i, 0)),
            pl.BlockSpec((1, 1, 3 * d), lambda bi, i: (bi, 0, 0)),
            pl.BlockSpec((1, d), lambda bi, i: (0, 0)),
            pl.BlockSpec((d, n), lambda bi, i: (0, 0)),
        ],
        out_specs=out_specs,
        out_shape=out_shapes,
        scratch_shapes=[pltpu.VMEM((tm, d), BF16)],
        compiler_params=_cparams(("arbitrary", "arbitrary")),
        name="in_projection",
    )(x, mod, gain, w_bf16)


def _outproj_kernel(y_ref, w_ref, x_ref, mod_ref, gain_ref, o_ref, *, d_model):
    t = jnp.dot(y_ref[0], w_ref[...], preferred_element_type=F32)
    ms = jnp.mean(t * t, axis=-1, keepdims=True)
    n = t * lax.rsqrt(ms + NORM_EPS) * gain_ref[...]
    gate = mod_ref[0, :, 2 * d_model:3 * d_model]
    o_ref[0] = x_ref[0] + gate * n


def _out_projection(y, w_bf16, x, mod, gain, tm):
    b, s, d = x.shape
    k = y.shape[-1]
    return pl.pallas_call(
        functools.partial(_outproj_kernel, d_model=d),
        grid=(b, s // tm),
        in_specs=[
            pl.BlockSpec((1, tm, k), lambda bi, i: (bi, i, 0)),
            pl.BlockSpec((k, d), lambda bi, i: (0, 0)),
            pl.BlockSpec((1, tm, d), lambda bi, i: (bi, i, 0)),
            pl.BlockSpec((1, 1, 3 * d), lambda bi, i: (bi, 0, 0)),
            pl.BlockSpec((1, d), lambda bi, i: (0, 0)),
        ],
        out_specs=pl.BlockSpec((1, tm, d), lambda bi, i: (bi, i, 0)),
        out_shape=jax.ShapeDtypeStruct((b, s, d), F32),
        compiler_params=_cparams(("arbitrary", "arbitrary")),
        name="out_projection",
    )(y, w_bf16, x, mod, gain)


N_ALIBI_FEATURES = 6
ATTN_ROW_CHUNK = 16
ATTN_SUM_ROWS = 16
ATTN_SCORE_SLOTS = 4
ATTN_PROB_SLOTS = 2
ATTN_DIAG_STEPS = 8
ATTN_PAST_STEPS = 8
ATTN_PIPE_DEPTH = 3
ATTN_LAMBDA_ROWS = 32
ATTN_STAT_ROWS = 4
ATTN_ZERO_EXP2 = 150.0
ATTN_BOUND_SLACK = 1.0
ATTN_NORM_SLACK = 1.01


def _alibi_key_features(s):
    pos = jnp.arange(s, dtype=jnp.int32)[:, None]
    col = jnp.arange(LANES, dtype=jnp.int32)[None, :]
    lo = (pos % BF16_EXACT_INT).astype(F32)
    hi = (pos // BF16_EXACT_INT).astype(F32)
    feat = jnp.where(col < 3, lo, jnp.where(col < N_ALIBI_FEATURES, hi, 0.0))
    return feat.astype(BF16)


def _attn_schedule(nq):
    pairs = [(qi, qi) for qi in range(nq)] + [(qi, j) for qi in range(nq) for j in range(qi)]
    qb, kb = zip(*pairs)
    return np.asarray(qb, np.int32), np.asarray(kb, np.int32)


def _attn_kernel(slope_ref, qb_ref, kb_ref, qn_ref, kn_ref, qT_ref, k_ref, kf_ref, vT_ref, g_ref,
                 lamv_ref, subln_ref, y_ref,
                 qs_ref, feat_ref, s0_ref, s1_ref, s2_ref, s3_ref, mc0_ref, mc1_ref, mc2_ref,
                 mc3_ref, p0_ref, p1_ref, al0_ref, al1_ref, m_ref, acc_ref, corr_ref,
                 thr_ref, ql_ref, kl_ref,
                 *, tq, nq, lam_init):
    h = pl.program_id(1)
    tk = tq
    w = 2 * tq
    rc = ATTN_ROW_CHUNK
    n_chunks = tk // rc
    n_pos = nq * (nq + 1) // 2
    s_refs = (s0_ref, s1_ref, s2_ref, s3_ref)
    mc_refs = (mc0_ref, mc1_ref, mc2_ref, mc3_ref)
    p_refs, al_refs = (p0_ref, p1_ref), (al0_ref, al1_ref)
    n_past = n_pos - nq
    assert nq % ATTN_DIAG_STEPS == 0 and nq >= ATTN_DIAG_STEPS
    assert ATTN_DIAG_STEPS % ATTN_SCORE_SLOTS == 0 and ATTN_PAST_STEPS % ATTN_SCORE_SLOTS == 0
    assert n_past % ATTN_PAST_STEPS == 0

    slope2 = slope_ref[h] * LOG2E
    fr = lax.broadcasted_iota(jnp.int32, (LANES, w), 0)
    x = jnp.full((LANES, w), slope2, F32)
    hi = x.astype(BF16).astype(F32)
    mid = (x - hi).astype(BF16).astype(F32)
    lo = x - hi - mid
    part = jnp.where((fr == 0) | (fr == 3), hi, jnp.where((fr == 1) | (fr == 4), mid, lo))
    feat = jnp.where(fr < 3, part, jnp.where(fr < N_ALIBI_FEATURES, part * BF16_EXACT_INT, 0.0))
    feat_ref[...] = feat.astype(BF16)
    kk = lax.broadcasted_iota(jnp.int32, (tk, w), 0)
    qq = lax.broadcasted_iota(jnp.int32, (tk, w), 1)
    qq = jnp.where(qq >= tq, qq - tq, qq)
    ahead = (kk - qq).astype(F32)
    same_chunk = (kk >> CHUNK_SHIFT) == (qq >> CHUNK_SHIFT)
    corr_ref[...] = jnp.where(kk <= qq, 0.0,
                              jnp.where(same_chunk, -2.0 * slope2 * ahead, MASK_VALUE))

    def init_query_block(qi, carry):
        qT = qT_ref[0, 0, qi]
        row = lax.broadcasted_iota(jnp.int32, qT.shape, 0)
        zero = jnp.zeros_like(qT)
        qs_ref[qi, :, 0:tq] = jnp.where(row < DA_HEAD_DIM, qT, zero)
        qs_ref[qi, :, tq:w] = jnp.where(row >= DA_HEAD_DIM, qT, zero)
        return carry

    lax.fori_loop(0, nq, init_query_block, 0)
    ones_rows = jnp.ones((ATTN_SUM_ROWS, tk), BF16)

    def qk(pair, t, s_slot, diag):
        qi, kj = pair(t)
        start = kj * tk if isinstance(kj, int) else pl.multiple_of(kj * tk, tk)
        ka = jnp.concatenate([k_ref[0, 0, pl.ds(start, tk), :], kf_ref[pl.ds(start, tk), :]],
                             axis=1)
        qa = jnp.concatenate([qs_ref[qi], feat_ref[...]], axis=0)
        s = jnp.dot(ka, qa, preferred_element_type=F32)
        if diag:
            s = s + corr_ref[...]
        s_refs[s_slot][...] = s
        mc_refs[s_slot][0:1, :] = jnp.max(s, axis=0, keepdims=True)

    def sm(pair, t, s_slot, p_slot, first):
        qi, _ = pair(t)
        if first:
            m_next = mc_refs[s_slot][0:1, :]
        else:
            m_prev = m_ref[qi]
            m_next = jnp.maximum(m_prev, mc_refs[s_slot][0:1, :])
            al_refs[p_slot][0:1, :] = jnp.exp2(m_prev - m_next)
        m_ref[qi] = m_next
        m_rows = jnp.broadcast_to(m_next, (rc, w))
        for c in range(n_chunks):
            p = jnp.exp2(s_refs[s_slot][c * rc:(c + 1) * rc, :] - m_rows)
            p_refs[p_slot][c * rc:(c + 1) * rc, :] = p.astype(BF16)

    def pv(pair, t, p_slot, first):
        qi, kj = pair(t)
        v_aug = jnp.concatenate([vT_ref[0, 0, kj], ones_rows], axis=0)
        upd = jnp.dot(v_aug, p_refs[p_slot][...], preferred_element_type=F32)
        if first:
            acc_ref[qi] = upd
        else:
            acc_ref[qi] = al_refs[p_slot][0:1, :] * acc_ref[qi] + upd

    def run_pipeline(pair, n, steps, diag):
        def time_step(tau, u, do_qk=True, do_sm=True, do_pv=True):
            if do_qk:
                qk(pair, tau, u, diag)
            if do_sm:
                sm(pair, tau - 2, (u + 2) % ATTN_SCORE_SLOTS, u % ATTN_PROB_SLOTS, diag)
            if do_pv:
                pv(pair, tau - 3, (u + 1) % ATTN_PROB_SLOTS, diag)

        time_step(0, 0, do_sm=False, do_pv=False)
        time_step(1, 1, do_sm=False, do_pv=False)
        time_step(2, 2, do_pv=False)

        def body(i, carry):
            for u in range(steps):
                time_step(ATTN_PIPE_DEPTH + steps * i + u, (ATTN_PIPE_DEPTH + u) % ATTN_SCORE_SLOTS)
            return carry

        lax.fori_loop(0, n // steps - 1, body, 0)
        for u in range(steps - ATTN_PIPE_DEPTH):
            time_step(n - (steps - ATTN_PIPE_DEPTH) + u, (ATTN_PIPE_DEPTH + u) % ATTN_SCORE_SLOTS)
        time_step(n, 0, do_qk=False)
        time_step(n + 1, 1, do_qk=False)
        time_step(n + 2, 2, do_qk=False, do_sm=False)

    run_pipeline(lambda t: (t, t), nq, ATTN_DIAG_STEPS, diag=True)

    norm_base = (pl.program_id(0) * pl.num_programs(1) + h) * nq
    for qi in range(nq):
        thr_ref[qi] = jnp.min(m_ref[qi]) - (ATTN_ZERO_EXP2 + ATTN_BOUND_SLACK)

    def choose(i, carry):
        count, pad_q, pad_k = carry
        qi, kj = qb_ref[nq + i], kb_ref[nq + i]
        last_key = ((kj + 1) * tk - 1).astype(F32)
        bound = (qn_ref[norm_base + qi] * kn_ref[norm_base + kj] * ATTN_NORM_SLACK
                 + slope2 * last_key)
        keep = bound >= thr_ref[qi]

        @pl.when(keep)
        def _():
            ql_ref[count] = qi
            kl_ref[count] = kj

        return (count + keep.astype(jnp.int32), jnp.where(keep, pad_q, qi),
                jnp.where(keep, pad_k, kj))

    n_keep, pad_q, pad_k = lax.fori_loop(0, n_past, choose,
                                         (jnp.int32(0), jnp.int32(0), jnp.int32(0)))
    n_pad = (-n_keep) & (ATTN_PAST_STEPS - 1)
    for r in range(ATTN_PAST_STEPS - 1):
        @pl.when(r < n_pad)
        def _():
            ql_ref[n_keep + r] = pad_q
            kl_ref[n_keep + r] = pad_k

    n_visit = n_keep + n_pad

    @pl.when(n_visit > 0)
    def _():
        run_pipeline(lambda t: (ql_ref[t], kl_ref[t]), n_visit, ATTN_PAST_STEPS, diag=False)

    lv = lamv_ref[...]
    lam = (jnp.exp(jnp.sum(lv[0:1] * lv[1:2], axis=-1, keepdims=True))
           - jnp.exp(jnp.sum(lv[2:3] * lv[3:4], axis=-1, keepdims=True)) + lam_init)

    def finish_query_block(qi, carry):
        o = acc_ref[qi, 0:DA_V_DIM, :] / acc_ref[qi, DA_V_DIM:DA_V_DIM + 1, :]
        out = o[:, 0:tq] - lam * o[:, tq:w]
        ms = jnp.mean(out * out, axis=0, keepdims=True)
        out = out * lax.rsqrt(ms + HEAD_NORM_EPS) * subln_ref[...] * (1.0 - lam_init)
        rows = pl.ds(pl.multiple_of(qi * tq, tq), tq)
        g = g_ref[0, rows, :].astype(F32)
        y_ref[0, rows, :] = (g * jax.nn.sigmoid(g) * out.T).astype(BF16)
        return carry

    lax.fori_loop(0, nq, finish_query_block, 0)


def _diff_attention(qT, k, vT, g, q_norms, k_norms, slopes, lamv, subln, lam_init):
    b, nh, nq, dk, tq = qT.shape
    s = k.shape[2]
    w = 2 * tq
    qb, kb = _attn_schedule(nq)
    kernel = functools.partial(_attn_kernel, tq=tq, nq=nq, lam_init=lam_init)
    smem = pl.BlockSpec(memory_space=pltpu.SMEM)
    once = pl.Buffered(1)
    return pl.pallas_call(
        kernel,
        grid=(b, nh),
        in_specs=[
            smem, smem, smem, smem, smem,
            pl.BlockSpec((1, 1, nq, dk, tq), lambda bi, h: (bi, h, 0, 0, 0), pipeline_mode=once),
            pl.BlockSpec((1, 1, s, dk), lambda bi, h: (bi, h, 0, 0)),
            pl.BlockSpec((s, LANES), lambda bi, h: (0, 0), pipeline_mode=once),
            pl.BlockSpec((1, 1, nq, DA_V_DIM, tq), lambda bi, h: (bi, h, 0, 0, 0)),
            pl.BlockSpec((1, s, DA_V_DIM), lambda bi, h: (bi, 0, h)),
            pl.BlockSpec((ATTN_LAMBDA_ROWS, LANES), lambda bi, h: (0, 0)),
            pl.BlockSpec((DA_V_DIM, 1), lambda bi, h: (0, 0)),
        ],
        out_specs=pl.BlockSpec((1, s, DA_V_DIM), lambda bi, h: (bi, 0, h)),
        out_shape=jax.ShapeDtypeStruct((b, s, nh * DA_V_DIM), BF16),
        scratch_shapes=(
            [pltpu.VMEM((nq, LANES, w), BF16),
             pltpu.VMEM((LANES, w), BF16)]
            + [pltpu.VMEM((tq, w), F32)] * ATTN_SCORE_SLOTS
            + [pltpu.VMEM((ATTN_STAT_ROWS, w), F32)] * ATTN_SCORE_SLOTS
            + [pltpu.VMEM((tq, w), BF16)] * ATTN_PROB_SLOTS
            + [pltpu.VMEM((ATTN_STAT_ROWS, w), F32)] * ATTN_PROB_SLOTS
            + [pltpu.VMEM((nq, 1, w), F32),
               pltpu.VMEM((nq, DA_V_DIM + ATTN_SUM_ROWS, w), F32),
               pltpu.VMEM((tq, w), F32),
               pltpu.SMEM((nq,), F32),
               pltpu.SMEM((len(qb),), jnp.int32),
               pltpu.SMEM((len(qb),), jnp.int32)]
        ),
        compiler_params=_cparams(("arbitrary", "arbitrary")),
        name="diff_attention",
    )(slopes, jnp.asarray(qb), jnp.asarray(kb), q_norms.reshape(-1), k_norms.reshape(-1), qT, k,
      _alibi_key_features(s), vT, g, lamv, subln.reshape(DA_V_DIM, 1))


def _ret_kernel(lg_ref, q_ref, k_ref, v_ref, g_ref, y_ref, state_ref, decay_ref, qdec_ref,
                kdec_ref, *, blk):
    n = pl.program_id(1)
    dk, dv = RET_QK_DIM, RET_V_DIM

    @pl.when(n == 0)
    def _():
        state_ref[...] = jnp.zeros(state_ref.shape, F32)
        ti = lax.broadcasted_iota(jnp.int32, (blk, blk), 0)
        ui = lax.broadcasted_iota(jnp.int32, (blk, blk), 1)
        dist = jnp.abs(ti - ui).astype(F32)
        allowed = (ui >> CHUNK_SHIFT) <= (ti >> CHUNK_SHIFT)
        t = lax.broadcasted_iota(jnp.int32, (blk, dk), 0).astype(F32)
        for h in range(RET_HEADS):
            decay_ref[h] = jnp.where(allowed, jnp.exp(lg_ref[h] * dist), 0.0) * dk ** -0.5
            qdec_ref[h] = jnp.exp(lg_ref[h] * t)
            kdec_ref[h] = jnp.exp(lg_ref[h] * (blk - t)) * dk ** -0.5

    for h in range(RET_HEADS):
        lg = lg_ref[h]
        q = q_ref[0, :, h * dk:(h + 1) * dk]
        k = k_ref[0, :, h * dk:(h + 1) * dk]
        v = v_ref[0, :, h * dv:(h + 1) * dv]
        qd = (q.astype(F32) * qdec_ref[h]).astype(BF16)
        kd = (k.astype(F32) * kdec_ref[h]).astype(BF16)

        a = lax.dot_general(q, k, (((1,), (1,)), ((), ())), preferred_element_type=F32)
        a = (a * decay_ref[h]).astype(BF16)
        out = jnp.dot(a, v, preferred_element_type=F32)
        out = out + jnp.dot(qd, state_ref[h].astype(BF16), preferred_element_type=F32)
        kv = lax.dot_general(kd, v, (((0,), (0,)), ((), ())), preferred_element_type=F32)
        block_decay = jnp.exp(lg * jnp.full((1, 1), float(blk), F32))
        state_ref[h] = block_decay * state_ref[h] + kv

        ss = jnp.sum(out * out, axis=-1, keepdims=True)
        out = out * lax.rsqrt(ss * (1.0 / dv) + HEAD_NORM_EPS)
        g = g_ref[0, :, h * dv:(h + 1) * dv].astype(F32)
        y_ref[0, :, h * dv:(h + 1) * dv] = (g * jax.nn.sigmoid(g) * out).astype(BF16)


def _retention(q, k, v, g, log_gammas, blk):
    b, s, qk_w = q.shape
    v_w = v.shape[-1]
    return pl.pallas_call(
        functools.partial(_ret_kernel, blk=blk),
        grid=(b, s // blk),
        in_specs=[
            pl.BlockSpec(memory_space=pltpu.SMEM),
            pl.BlockSpec((1, blk, qk_w), lambda bi, i: (bi, i, 0)),
            pl.BlockSpec((1, blk, qk_w), lambda bi, i: (bi, i, 0)),
            pl.BlockSpec((1, blk, v_w), lambda bi, i: (bi, i, 0)),
            pl.BlockSpec((1, blk, v_w), lambda bi, i: (bi, i, 0)),
        ],
        out_specs=pl.BlockSpec((1, blk, v_w), lambda bi, i: (bi, i, 0)),
        out_shape=jax.ShapeDtypeStruct((b, s, v_w), BF16),
        scratch_shapes=[
            pltpu.VMEM((RET_HEADS, RET_QK_DIM, RET_V_DIM), F32),
            pltpu.VMEM((RET_HEADS, blk, blk), F32),
            pltpu.VMEM((RET_HEADS, blk, RET_QK_DIM), F32),
            pltpu.VMEM((RET_HEADS, blk, RET_QK_DIM), F32),
        ],
        compiler_params=_cparams(("arbitrary", "arbitrary")),
        name="retention",
    )(log_gammas, q, k, v, g)


def kernel(x, c, ada_w, ada_b, pre_gain, post_gain, da_w_in, da_w_out, da_lambda_q1, da_lambda_k1,
           da_lambda_q2, da_lambda_k2, da_subln_gain, ret_w_in, ret_w_out):
    b, s, d = x.shape
    mod = _modulation(c, ada_w, ada_b)

    attn_tile = 512
    qk_w = 2 * DA_HEADS * DA_HEAD_DIM
    v_w = DA_HEADS * DA_V_DIM
    q_scale = LOG2E * DA_HEAD_DIM ** -0.5
    da_plan = (("headT", 0, DA_HEADS, 2 * DA_HEAD_DIM, q_scale, 0),
               ("head", qk_w, DA_HEADS, 2 * DA_HEAD_DIM, 1.0, DA_HEADS),
               ("headT", 2 * qk_w, DA_HEADS, DA_V_DIM, 1.0, None),
               ("tok", 2 * qk_w + v_w, None, v_w, 1.0, None))
    mod0 = mod[0].reshape(b, 1, 3 * d)
    qT, k, vT, g, norms = _in_projection(x, mod0, pre_gain[0:1], da_w_in[0].astype(BF16),
                                         da_plan, tm=attn_tile, slab=256)
    q_norms = jnp.transpose(norms[:, :, 0:DA_HEADS, 0], (0, 2, 1))
    k_norms = jnp.transpose(norms[:, :, DA_HEADS:2 * DA_HEADS, 0], (0, 2, 1))
    lam_init = 0.8 - 0.6 * math.exp(-0.3 * 0)
    slopes = jnp.asarray([2.0 ** (-8.0 * (hh + 1) / DA_HEADS) for hh in range(DA_HEADS)], F32)
    lamv = jnp.stack([da_lambda_q1[0], da_lambda_k1[0], da_lambda_q2[0], da_lambda_k2[0]])
    lamv = jnp.pad(lamv, ((0, ATTN_LAMBDA_ROWS - 4), (0, LANES - DA_HEAD_DIM)))
    y = _diff_attention(qT, k, vT, g, q_norms, k_norms, slopes, lamv, da_subln_gain[0], lam_init)
    x = _out_projection(y, da_w_out[0].astype(BF16), x, mod0, post_gain[0:1], tm=512)

    rqk = RET_HEADS * RET_QK_DIM
    rv = RET_HEADS * RET_V_DIM
    ret_plan = (("tok", 0, None, rqk, 1.0, None), ("tok", rqk, None, rqk, 1.0, None),
                ("tok", 2 * rqk, None, rv, 1.0, None), ("tok", 2 * rqk + rv, None, rv, 1.0, None))
    mod1 = mod[1].reshape(b, 1, 3 * d)
    q, k, v, g = _in_projection(x, mod1, pre_gain[1:2], ret_w_in[0].astype(BF16), ret_plan,
                                tm=512, slab=256)
    gammas = 1.0 - 2.0 ** (-5.0 - jnp.arange(RET_HEADS, dtype=F32))
    y = _retention(q, k, v, g, jnp.log(gammas), blk=256)
    return _out_projection(y, ret_w_out[0].astype(BF16), x, mod1, post_gain[1:2], tm=512)
```

```python
import functools
import math

import jax
import jax.numpy as jnp
import numpy as np
from jax import lax
from jax.experimental import pallas as pl
from jax.experimental.pallas import tpu as pltpu

CHUNK = 64
CHUNK_SHIFT = 6
DA_HEADS = 8
DA_HEAD_DIM = 64
DA_V_DIM = 128
RET_HEADS = 4
RET_QK_DIM = 256
RET_V_DIM = 512
NORM_EPS = 1e-6
HEAD_NORM_EPS = 1e-5
MASK_VALUE = -1e30
LOG2E = 1.4426950408889634

V7X_VMEM_LIMIT_BYTES = 56 * 1024 * 1024
LANES = 128
SUBLANES = 8
BF16_EXACT_INT = 256

BF16 = jnp.bfloat16
F32 = jnp.float32


def _cparams(sem, vmem=V7X_VMEM_LIMIT_BYTES, flags=None):
    return pltpu.CompilerParams(dimension_semantics=sem, vmem_limit_bytes=vmem, flags=flags)


def _mod_kernel(c_ref, w_ref, b_ref, o_ref):
    c = c_ref[...]
    cond = c * jax.nn.sigmoid(c)
    o_ref[0] = jnp.dot(cond, w_ref[0], preferred_element_type=F32) + b_ref[0]


def _modulation(c, ada_w, ada_b):
    depth, d, d3 = ada_w.shape
    b = c.shape[0]
    n_tiles = d3 // d
    return pl.pallas_call(
        _mod_kernel,
        grid=(depth, n_tiles),
        in_specs=[
            pl.BlockSpec((b, d), lambda l, j: (0, 0)),
            pl.BlockSpec((1, d, d), lambda l, j: (l, 0, j)),
            pl.BlockSpec((1, 1, d), lambda l, j: (l, 0, j)),
        ],
        out_specs=pl.BlockSpec((1, b, d), lambda l, j: (l, 0, j)),
        out_shape=jax.ShapeDtypeStruct((depth, b, d3), F32),
        compiler_params=_cparams(("arbitrary", "arbitrary")),
        name="modulation",
    )(c, ada_w, ada_b.reshape(depth, 1, d3))


def _inproj_kernel(x_ref, mod_ref, gain_ref, w_ref, *refs, d_model, plan, slab, n_norm_rows):
    out_refs = refs[:len(plan)]
    norm_ref = refs[len(plan)] if n_norm_rows else None
    hb_ref = refs[-1]
    x = x_ref[0]
    ms = jnp.mean(x * x, axis=-1, keepdims=True)
    hn = x * lax.rsqrt(ms + NORM_EPS) * gain_ref[...]
    shift = mod_ref[0, :, 0:d_model]
    scale = mod_ref[0, :, d_model:2 * d_model]
    hb_ref[...] = (hn * (1.0 + scale) + shift).astype(BF16)
    for (kind, c0, n_heads, width, out_scale, norm_row), o_ref in zip(plan, out_refs):
        total = width if kind == "tok" else n_heads * width
        for s0 in range(0, total, slab):
            r = jnp.dot(hb_ref[...], w_ref[:, c0 + s0:c0 + s0 + slab],
                        preferred_element_type=F32)
            if out_scale != 1.0:
                r = r * out_scale
            if kind == "tok":
                o_ref[0, :, s0:s0 + slab] = r.astype(BF16)
            elif kind == "head" and width >= slab:
                hh, off = divmod(s0, width)
                o_ref[0, hh, :, off:off + slab] = r.astype(BF16)
            else:
                for p in range(slab // width):
                    head = s0 // width + p
                    piece = r[:, p * width:(p + 1) * width]
                    if kind == "head":
                        o_ref[0, head] = piece.astype(BF16)
                    else:
                        o_ref[0, head, 0] = piece.T.astype(BF16)
                    if norm_row is not None:
                        norm2 = jnp.max(jnp.sum(piece * piece, axis=1, keepdims=True),
                                        axis=0, keepdims=True)
                        norm_ref[0, 0, norm_row + head:norm_row + head + 1, :] = (
                            jnp.broadcast_to(jnp.sqrt(norm2), (1, LANES)))


def _in_projection(x, mod, gain, w_bf16, plan, tm, slab):
    b, s, d = x.shape
    n = w_bf16.shape[1]
    out_shapes, out_specs = [], []
    n_norm_rows = sum(e[2] for e in plan if e[5] is not None)
    for (kind, c0, n_heads, width, _, _) in plan:
        if kind == "tok":
            out_shapes.append(jax.ShapeDtypeStruct((b, s, width), BF16))
            out_specs.append(pl.BlockSpec((1, tm, width), lambda bi, i: (bi, i, 0)))
        elif kind == "head":
            out_shapes.append(jax.ShapeDtypeStruct((b, n_heads, s, width), BF16))
            out_specs.append(pl.BlockSpec((1, n_heads, tm, width), lambda bi, i: (bi, 0, i, 0)))
        else:
            assert slab % width == 0
            out_shapes.append(jax.ShapeDtypeStruct((b, n_heads, s // tm, width, tm), BF16))
            out_specs.append(pl.BlockSpec((1, n_heads, 1, width, tm),
                                          lambda bi, i: (bi, 0, i, 0, 0)))
    if n_norm_rows:
        out_shapes.append(jax.ShapeDtypeStruct((b, s // tm, n_norm_rows, LANES), F32))
        out_specs.append(pl.BlockSpec((1, 1, n_norm_rows, LANES), lambda bi, i: (bi, i, 0, 0)))
    return pl.pallas_call(
        functools.partial(_inproj_kernel, d_model=d, plan=plan, slab=slab,
                          n_norm_rows=n_norm_rows),
        grid=(b, s // tm),
        in_specs=[
            pl.BlockSpec((1, tm, d), lambda bi, i: (bi, i, 0)),
            pl.BlockSpec((1, 1, 3 * d), lambda bi, i: (bi, 0, 0)),
            pl.BlockSpec((1, d), lambda bi, i: (0, 0)),
            pl.BlockSpec((d, n), lambda bi, i: (0, 0)),
        ],
        out_specs=out_specs,
        out_shape=out_shapes,
        scratch_shapes=[pltpu.VMEM((tm, d), BF16)],
        compiler_params=_cparams(("arbitrary", "arbitrary")),
        name="in_projection",
    )(x, mod, gain, w_bf16)


def _outproj_kernel(y_ref, w_ref, x_ref, mod_ref, gain_ref, o_ref, *, d_model):
    t = jnp.dot(y_ref[0], w_ref[...], preferred_element_type=F32)
    ms = jnp.mean(t * t, axis=-1, keepdims=True)
    n = t * lax.rsqrt(ms + NORM_EPS) * gain_ref[...]
    gate = mod_ref[0, :, 2 * d_model:3 * d_model]
    o_ref[0] = x_ref[0] + gate * n


def _out_projection(y, w_bf16, x, mod, gain, tm):
    b, s, d = x.shape
    k = y.shape[-1]
    return pl.pallas_call(
        functools.partial(_outproj_kernel, d_model=d),
        grid=(b, s // tm),
        in_specs=[
            pl.BlockSpec((1, tm, k), lambda bi, i: (bi, i, 0)),
            pl.BlockSpec((k, d), lambda bi, i: (0, 0)),
            pl.BlockSpec((1, tm, d), lambda bi, i: (bi, i, 0)),
            pl.BlockSpec((1, 1, 3 * d), lambda bi, i: (bi, 0, 0)),
            pl.BlockSpec((1, d), lambda bi, i: (0, 0)),
        ],
        out_specs=pl.BlockSpec((1, tm, d), lambda bi, i: (bi, i, 0)),
        out_shape=jax.ShapeDtypeStruct((b, s, d), F32),
        compiler_params=_cparams(("arbitrary", "arbitrary")),
        name="out_projection",
    )(y, w_bf16, x, mod, gain)


N_ALIBI_FEATURES = 6
ATTN_ROW_CHUNK = 16
ATTN_SUM_ROWS = 16
ATTN_SCORE_SLOTS = 4
ATTN_PROB_SLOTS = 2
ATTN_DIAG_STEPS = 8
ATTN_PAST_STEPS = 8
ATTN_PIPE_DEPTH = 3
ATTN_LAMBDA_ROWS = 32
ATTN_STAT_ROWS = 4
ATTN_ZERO_EXP2 = 150.0
ATTN_BOUND_SLACK = 1.0
ATTN_NORM_SLACK = 1.01


def _alibi_key_features(s):
    pos = jnp.arange(s, dtype=jnp.int32)[:, None]
    col = jnp.arange(LANES, dtype=jnp.int32)[None, :]
    lo = (pos % BF16_EXACT_INT).astype(F32)
    hi = (pos // BF16_EXACT_INT).astype(F32)
    feat = jnp.where(col < 3, lo, jnp.where(col < N_ALIBI_FEATURES, hi, 0.0))
    return feat.astype(BF16)


def _attn_schedule(nq):
    pairs = [(qi, qi) for qi in range(nq)] + [(qi, j) for qi in range(nq) for j in range(qi)]
    qb, kb = zip(*pairs)
    return np.asarray(qb, np.int32), np.asarray(kb, np.int32)


def _attn_kernel(slope_ref, qb_ref, kb_ref, qn_ref, kn_ref, qT_ref, k_ref, kf_ref, vT_ref, g_ref,
                 lamv_ref, subln_ref, y_ref,
                 qs_ref, feat_ref, s0_ref, s1_ref, s2_ref, s3_ref, mc0_ref, mc1_ref, mc2_ref,
                 mc3_ref, p0_ref, p1_ref, al0_ref, al1_ref, m_ref, acc_ref, corr_ref,
                 thr_ref, ql_ref, kl_ref,
                 *, tq, nq, lam_init):
    h = pl.program_id(1)
    tk = tq
    w = 2 * tq
    rc = ATTN_ROW_CHUNK
    n_chunks = tk // rc
    n_pos = nq * (nq + 1) // 2
    s_refs = (s0_ref, s1_ref, s2_ref, s3_ref)
    mc_refs = (mc0_ref, mc1_ref, mc2_ref, mc3_ref)
    p_refs, al_refs = (p0_ref, p1_ref), (al0_ref, al1_ref)
    n_past = n_pos - nq
    assert nq % ATTN_DIAG_STEPS == 0 and nq >= ATTN_DIAG_STEPS
    assert ATTN_DIAG_STEPS % ATTN_SCORE_SLOTS == 0 and ATTN_PAST_STEPS % ATTN_SCORE_SLOTS == 0
    assert n_past % ATTN_PAST_STEPS == 0

    slope2 = slope_ref[h] * LOG2E
    fr = lax.broadcasted_iota(jnp.int32, (LANES, w), 0)
    x = jnp.full((LANES, w), slope2, F32)
    hi = x.astype(BF16).astype(F32)
    mid = (x - hi).astype(BF16).astype(F32)
    lo = x - hi - mid
    part = jnp.where((fr == 0) | (fr == 3), hi, jnp.where((fr == 1) | (fr == 4), mid, lo))
    feat = jnp.where(fr < 3, part, jnp.where(fr < N_ALIBI_FEATURES, part * BF16_EXACT_INT, 0.0))
    feat_ref[...] = feat.astype(BF16)
    kk = lax.broadcasted_iota(jnp.int32, (tk, w), 0)
    qq = lax.broadcasted_iota(jnp.int32, (tk, w), 1)
    qq = jnp.where(qq >= tq, qq - tq, qq)
    ahead = (kk - qq).astype(F32)
    same_chunk = (kk >> CHUNK_SHIFT) == (qq >> CHUNK_SHIFT)
    corr_ref[...] = jnp.where(kk <= qq, 0.0,
                              jnp.where(same_chunk, -2.0 * slope2 * ahead, MASK_VALUE))

    def init_query_block(qi, carry):
        qT = qT_ref[0, 0, qi]
        row = lax.broadcasted_iota(jnp.int32, qT.shape, 0)
        zero = jnp.zeros_like(qT)
        qs_ref[qi, :, 0:tq] = jnp.where(row < DA_HEAD_DIM, qT, zero)
        qs_ref[qi, :, tq:w] = jnp.where(row >= DA_HEAD_DIM, qT, zero)
        return carry

    lax.fori_loop(0, nq, init_query_block, 0)
    ones_rows = jnp.ones((ATTN_SUM_ROWS, tk), BF16)

    def qk(pair, t, s_slot, diag):
        qi, kj = pair(t)
        start = kj * tk if isinstance(kj, int) else pl.multiple_of(kj * tk, tk)
        ka = jnp.concatenate([k_ref[0, 0, pl.ds(start, tk), :], kf_ref[pl.ds(start, tk), :]],
                             axis=1)
        qa = jnp.concatenate([qs_ref[qi], feat_ref[...]], axis=0)
        s = jnp.dot(ka, qa, preferred_element_type=F32)
        if diag:
            s = s + corr_ref[...]
        s_refs[s_slot][...] = s
        mc_refs[s_slot][0:1, :] = jnp.max(s, axis=0, keepdims=True)

    def sm(pair, t, s_slot, p_slot, first):
        qi, _ = pair(t)
        if first:
            m_next = mc_refs[s_slot][0:1, :]
        else:
            m_prev = m_ref[qi]
            m_next = jnp.maximum(m_prev, mc_refs[s_slot][0:1, :])
            al_refs[p_slot][0:1, :] = jnp.exp2(m_prev - m_next)
        m_ref[qi] = m_next
        m_rows = jnp.broadcast_to(m_next, (rc, w))
        for c in range(n_chunks):
            p = jnp.exp2(s_refs[s_slot][c * rc:(c + 1) * rc, :] - m_rows)
            p_refs[p_slot][c * rc:(c + 1) * rc, :] = p.astype(BF16)

    def pv(pair, t, p_slot, first):
        qi, kj = pair(t)
        v_aug = jnp.concatenate([vT_ref[0, 0, kj], ones_rows], axis=0)
        upd = jnp.dot(v_aug, p_refs[p_slot][...], preferred_element_type=F32)
        if first:
            acc_ref[qi] = upd
        else:
            acc_ref[qi] = al_refs[p_slot][0:1, :] * acc_ref[qi] + upd

    def run_pipeline(pair, n, steps, diag):
        def time_step(tau, u, do_qk=True, do_sm=True, do_pv=True):
            if do_qk:
                qk(pair, tau, u, diag)
            if do_sm:
                sm(pair, tau - 2, (u + 2) % ATTN_SCORE_SLOTS, u % ATTN_PROB_SLOTS, diag)
            if do_pv:
                pv(pair, tau - 3, (u + 1) % ATTN_PROB_SLOTS, diag)

        time_step(0, 0, do_sm=False, do_pv=False)
        time_step(1, 1, do_sm=False, do_pv=False)
        time_step(2, 2, do_pv=False)

        def body(i, carry):
            for u in range(steps):
                time_step(ATTN_PIPE_DEPTH + steps * i + u, (ATTN_PIPE_DEPTH + u) % ATTN_SCORE_SLOTS)
            return carry

        lax.fori_loop(0, n // steps - 1, body, 0)
        for u in range(steps - ATTN_PIPE_DEPTH):
            time_step(n - (steps - ATTN_PIPE_DEPTH) + u, (ATTN_PIPE_DEPTH + u) % ATTN_SCORE_SLOTS)
        time_step(n, 0, do_qk=False)
        time_step(n + 1, 1, do_qk=False)
        time_step(n + 2, 2, do_qk=False, do_sm=False)

    run_pipeline(lambda t: (t, t), nq, ATTN_DIAG_STEPS, diag=True)

    norm_base = (pl.program_id(0) * pl.num_programs(1) + h) * nq
    for qi in range(nq):
        thr_ref[qi] = jnp.min(m_ref[qi]) - (ATTN_ZERO_EXP2 + ATTN_BOUND_SLACK)

    def choose(i, carry):
        count, pad_q, pad_k = carry
        qi, kj = qb_ref[nq + i], kb_ref[nq + i]
        last_key = ((kj + 1) * tk - 1).astype(F32)
        bound = (qn_ref[norm_base + qi] * kn_ref[norm_base + kj] * ATTN_NORM_SLACK
                 + slope2 * last_key)
        keep = bound >= thr_ref[qi]
        ql_ref[count] = qi
        kl_ref[count] = kj
        return (count + keep.astype(jnp.int32), jnp.where(keep, pad_q, qi),
                jnp.where(keep, pad_k, kj))

    n_keep, pad_q, pad_k = lax.fori_loop(0, n_past, choose,
                                         (jnp.int32(0), jnp.int32(0), jnp.int32(0)))
    n_pad = (-n_keep) & (ATTN_PAST_STEPS - 1)
    for r in range(ATTN_PAST_STEPS - 1):
        @pl.when(r < n_pad)
        def _():
            ql_ref[n_keep + r] = pad_q
            kl_ref[n_keep + r] = pad_k

    n_visit = n_keep + n_pad

    @pl.when(n_visit > 0)
    def _():
        run_pipeline(lambda t: (ql_ref[t], kl_ref[t]), n_visit, ATTN_PAST_STEPS, diag=False)

    lv = lamv_ref[...]
    lam = (jnp.exp(jnp.sum(lv[0:1] * lv[1:2], axis=-1, keepdims=True))
           - jnp.exp(jnp.sum(lv[2:3] * lv[3:4], axis=-1, keepdims=True)) + lam_init)

    def finish_query_block(qi, carry):
        o = acc_ref[qi, 0:DA_V_DIM, :] * (1.0 / acc_ref[qi, DA_V_DIM:DA_V_DIM + 1, :])
        out = o[:, 0:tq] - lam * o[:, tq:w]
        ms = jnp.mean(out * out, axis=0, keepdims=True)
        out = out * lax.rsqrt(ms + HEAD_NORM_EPS) * subln_ref[...] * (1.0 - lam_init)
        rows = pl.ds(pl.multiple_of(qi * tq, tq), tq)
        g = g_ref[0, rows, :].astype(F32)
        y_ref[0, rows, :] = (g * jax.nn.sigmoid(g) * out.T).astype(BF16)
        return carry

    lax.fori_loop(0, nq, finish_query_block, 0)


def _diff_attention(qT, k, vT, g, q_norms, k_norms, slopes, lamv, subln, lam_init):
    b, nh, nq, dk, tq = qT.shape
    s = k.shape[2]
    w = 2 * tq
    qb, kb = _attn_schedule(nq)
    kernel = functools.partial(_attn_kernel, tq=tq, nq=nq, lam_init=lam_init)
    smem = pl.BlockSpec(memory_space=pltpu.SMEM)
    once = pl.Buffered(1)
    return pl.pallas_call(
        kernel,
        grid=(b, nh),
        in_specs=[
            smem, smem, smem, smem, smem,
            pl.BlockSpec((1, 1, nq, dk, tq), lambda bi, h: (bi, h, 0, 0, 0), pipeline_mode=once),
            pl.BlockSpec((1, 1, s, dk), lambda bi, h: (bi, h, 0, 0)),
            pl.BlockSpec((s, LANES), lambda bi, h: (0, 0), pipeline_mode=once),
            pl.BlockSpec((1, 1, nq, DA_V_DIM, tq), lambda bi, h: (bi, h, 0, 0, 0)),
            pl.BlockSpec((1, s, DA_V_DIM), lambda bi, h: (bi, 0, h)),
            pl.BlockSpec((ATTN_LAMBDA_ROWS, LANES), lambda bi, h: (0, 0)),
            pl.BlockSpec((DA_V_DIM, 1), lambda bi, h: (0, 0)),
        ],
        out_specs=pl.BlockSpec((1, s, DA_V_DIM), lambda bi, h: (bi, 0, h)),
        out_shape=jax.ShapeDtypeStruct((b, s, nh * DA_V_DIM), BF16),
        scratch_shapes=(
            [pltpu.VMEM((nq, LANES, w), BF16),
             pltpu.VMEM((LANES, w), BF16)]
            + [pltpu.VMEM((tq, w), F32)] * ATTN_SCORE_SLOTS
            + [pltpu.VMEM((ATTN_STAT_ROWS, w), F32)] * ATTN_SCORE_SLOTS
            + [pltpu.VMEM((tq, w), BF16)] * ATTN_PROB_SLOTS
            + [pltpu.VMEM((ATTN_STAT_ROWS, w), F32)] * ATTN_PROB_SLOTS
            + [pltpu.VMEM((nq, 1, w), F32),
               pltpu.VMEM((nq, DA_V_DIM + ATTN_SUM_ROWS, w), F32),
               pltpu.VMEM((tq, w), F32),
               pltpu.SMEM((nq,), F32),
               pltpu.SMEM((len(qb),), jnp.int32),
               pltpu.SMEM((len(qb),), jnp.int32)]
        ),
        compiler_params=_cparams(("arbitrary", "arbitrary")),
        name="diff_attention",
    )(slopes, jnp.asarray(qb), jnp.asarray(kb), q_norms.reshape(-1), k_norms.reshape(-1), qT, k,
      _alibi_key_features(s), vT, g, lamv, subln.reshape(DA_V_DIM, 1))


def _ret_kernel(lg_ref, q_ref, k_ref, v_ref, g_ref, y_ref, state_ref, decay_ref, qdec_ref,
                kdec_ref, *, blk):
    n = pl.program_id(1)
    dk, dv = RET_QK_DIM, RET_V_DIM

    @pl.when(n == 0)
    def _():
        state_ref[...] = jnp.zeros(state_ref.shape, F32)
        ti = lax.broadcasted_iota(jnp.int32, (blk, blk), 0)
        ui = lax.broadcasted_iota(jnp.int32, (blk, blk), 1)
        dist = jnp.abs(ti - ui).astype(F32)
        allowed = (ui >> CHUNK_SHIFT) <= (ti >> CHUNK_SHIFT)
        t = lax.broadcasted_iota(jnp.int32, (blk, dk), 0).astype(F32)
        for h in range(RET_HEADS):
            decay_ref[h] = jnp.where(allowed, jnp.exp(lg_ref[h] * dist), 0.0) * dk ** -0.5
            qdec_ref[h] = jnp.exp(lg_ref[h] * t)
            kdec_ref[h] = jnp.exp(lg_ref[h] * (blk - t)) * dk ** -0.5

    for h in range(RET_HEADS):
        lg = lg_ref[h]
        q = q_ref[0, :, h * dk:(h + 1) * dk]
        k = k_ref[0, :, h * dk:(h + 1) * dk]
        v = v_ref[0, :, h * dv:(h + 1) * dv]
        qd = (q.astype(F32) * qdec_ref[h]).astype(BF16)
        kd = (k.astype(F32) * kdec_ref[h]).astype(BF16)

        a = lax.dot_general(q, k, (((1,), (1,)), ((), ())), preferred_element_type=F32)
        a = (a * decay_ref[h]).astype(BF16)
        out = jnp.dot(a, v, preferred_element_type=F32)
        out = out + jnp.dot(qd, state_ref[h].astype(BF16), preferred_element_type=F32)
        kv = lax.dot_general(kd, v, (((0,), (0,)), ((), ())), preferred_element_type=F32)
        block_decay = jnp.exp(lg * jnp.full((1, 1), float(blk), F32))
        state_ref[h] = block_decay * state_ref[h] + kv

        ss = jnp.sum(out * out, axis=-1, keepdims=True)
        out = out * lax.rsqrt(ss * (1.0 / dv) + HEAD_NORM_EPS)
        g = g_ref[0, :, h * dv:(h + 1) * dv].astype(F32)
        y_ref[0, :, h * dv:(h + 1) * dv] = (g * jax.nn.sigmoid(g) * out).astype(BF16)


def _retention(q, k, v, g, log_gammas, blk):
    b, s, qk_w = q.shape
    v_w = v.shape[-1]
    return pl.pallas_call(
        functools.partial(_ret_kernel, blk=blk),
        grid=(b, s // blk),
        in_specs=[
            pl.BlockSpec(memory_space=pltpu.SMEM),
            pl.BlockSpec((1, blk, qk_w), lambda bi, i: (bi, i, 0)),
            pl.BlockSpec((1, blk, qk_w), lambda bi, i: (bi, i, 0)),
            pl.BlockSpec((1, blk, v_w), lambda bi, i: (bi, i, 0)),
            pl.BlockSpec((1, blk, v_w), lambda bi, i: (bi, i, 0)),
        ],
        out_specs=pl.BlockSpec((1, blk, v_w), lambda bi, i: (bi, i, 0)),
        out_shape=jax.ShapeDtypeStruct((b, s, v_w), BF16),
        scratch_shapes=[
            pltpu.VMEM((RET_HEADS, RET_QK_DIM, RET_V_DIM), F32),
            pltpu.VMEM((RET_HEADS, blk, blk), F32),
            pltpu.VMEM((RET_HEADS, blk, RET_QK_DIM), F32),
            pltpu.VMEM((RET_HEADS, blk, RET_QK_DIM), F32),
        ],
        compiler_params=_cparams(("arbitrary", "arbitrary")),
        name="retention",
    )(log_gammas, q, k, v, g)


def kernel(x, c, ada_w, ada_b, pre_gain, post_gain, da_w_in, da_w_out, da_lambda_q1, da_lambda_k1,
           da_lambda_q2, da_lambda_k2, da_subln_gain, ret_w_in, ret_w_out):
    b, s, d = x.shape
    mod = _modulation(c, ada_w, ada_b)

    attn_tile = 512
    qk_w = 2 * DA_HEADS * DA_HEAD_DIM
    v_w = DA_HEADS * DA_V_DIM
    q_scale = LOG2E * DA_HEAD_DIM ** -0.5
    da_plan = (("headT", 0, DA_HEADS, 2 * DA_HEAD_DIM, q_scale, 0),
               ("head", qk_w, DA_HEADS, 2 * DA_HEAD_DIM, 1.0, DA_HEADS),
               ("headT", 2 * qk_w, DA_HEADS, DA_V_DIM, 1.0, None),
               ("tok", 2 * qk_w + v_w, None, v_w, 1.0, None))
    mod0 = mod[0].reshape(b, 1, 3 * d)
    qT, k, vT, g, norms = _in_projection(x, mod0, pre_gain[0:1], da_w_in[0].astype(BF16),
                                         da_plan, tm=attn_tile, slab=256)
    q_norms = jnp.transpose(norms[:, :, 0:DA_HEADS, 0], (0, 2, 1))
    k_norms = jnp.transpose(norms[:, :, DA_HEADS:2 * DA_HEADS, 0], (0, 2, 1))
    lam_init = 0.8 - 0.6 * math.exp(-0.3 * 0)
    slopes = jnp.asarray([2.0 ** (-8.0 * (hh + 1) / DA_HEADS) for hh in range(DA_HEADS)], F32)
    lamv = jnp.stack([da_lambda_q1[0], da_lambda_k1[0], da_lambda_q2[0], da_lambda_k2[0]])
    lamv = jnp.pad(lamv, ((0, ATTN_LAMBDA_ROWS - 4), (0, LANES - DA_HEAD_DIM)))
    y = _diff_attention(qT, k, vT, g, q_norms, k_norms, slopes, lamv, da_subln_gain[0], lam_init)
    x = _out_projection(y, da_w_out[0].astype(BF16), x, mod0, post_gain[0:1], tm=512)

    rqk = RET_HEADS * RET_QK_DIM
    rv = RET_HEADS * RET_V_DIM
    ret_plan = (("tok", 0, None, rqk, 1.0, None), ("tok", rqk, None, rqk, 1.0, None),
                ("tok", 2 * rqk, None, rv, 1.0, None), ("tok", 2 * rqk + rv, None, rv, 1.0, None))
    mod1 = mod[1].reshape(b, 1, 3 * d)
    q, k, v, g = _in_projection(x, mod1, pre_gain[1:2], ret_w_in[0].astype(BF16), ret_plan,
                                tm=512, slab=256)
    gammas = 1.0 - 2.0 ** (-5.0 - jnp.arange(RET_HEADS, dtype=F32))
    y = _retention(q, k, v, g, jnp.log(gammas), blk=256)
    return _out_projection(y, ret_w_out[0].astype(BF16), x, mod1, post_gain[1:2], tm=512)
```

```python
import functools
import math

import jax
import jax.numpy as jnp
import numpy as np
from jax import lax
from jax.experimental import pallas as pl
from jax.experimental.pallas import tpu as pltpu

CHUNK = 64
CHUNK_SHIFT = 6
DA_HEADS = 8
DA_HEAD_DIM = 64
DA_V_DIM = 128
RET_HEADS = 4
RET_QK_DIM = 256
RET_V_DIM = 512
NORM_EPS = 1e-6
HEAD_NORM_EPS = 1e-5
MASK_VALUE = -1e30
LOG2E = 1.4426950408889634

V7X_VMEM_LIMIT_BYTES = 56 * 1024 * 1024
LANES = 128
SUBLANES = 8
BF16_EXACT_INT = 256

BF16 = jnp.bfloat16
F32 = jnp.float32


def _cparams(sem, vmem=V7X_VMEM_LIMIT_BYTES, flags=None):
    return pltpu.CompilerParams(dimension_semantics=sem, vmem_limit_bytes=vmem, flags=flags)


def _mod_kernel(c_ref, w_ref, b_ref, o_ref):
    c = c_ref[...]
    cond = c * jax.nn.sigmoid(c)
    o_ref[0] = jnp.dot(cond, w_ref[0], preferred_element_type=F32) + b_ref[0]


def _modulation(c, ada_w, ada_b):
    depth, d, d3 = ada_w.shape
    b = c.shape[0]
    n_tiles = d3 // d
    return pl.pallas_call(
        _mod_kernel,
        grid=(depth, n_tiles),
        in_specs=[
            pl.BlockSpec((b, d), lambda l, j: (0, 0)),
            pl.BlockSpec((1, d, d), lambda l, j: (l, 0, j)),
            pl.BlockSpec((1, 1, d), lambda l, j: (l, 0, j)),
        ],
        out_specs=pl.BlockSpec((1, b, d), lambda l, j: (l, 0, j)),
        out_shape=jax.ShapeDtypeStruct((depth, b, d3), F32),
        compiler_params=_cparams(("arbitrary", "arbitrary")),
        name="modulation",
    )(c, ada_w, ada_b.reshape(depth, 1, d3))


def _inproj_kernel(x_ref, mod_ref, gain_ref, w_ref, *refs, d_model, plan, slab, n_norm_rows):
    out_refs = refs[:len(plan)]
    norm_ref = refs[len(plan)] if n_norm_rows else None
    hb_ref = refs[-1]
    x = x_ref[0]
    ms = jnp.mean(x * x, axis=-1, keepdims=True)
    hn = x * lax.rsqrt(ms + NORM_EPS) * gain_ref[...]
    shift = mod_ref[0, :, 0:d_model]
    scale = mod_ref[0, :, d_model:2 * d_model]
    hb_ref[...] = (hn * (1.0 + scale) + shift).astype(BF16)
    for (kind, c0, n_heads, width, out_scale, norm_row), o_ref in zip(plan, out_refs):
        total = width if kind == "tok" else n_heads * width
        for s0 in range(0, total, slab):
            r = jnp.dot(hb_ref[...], w_ref[:, c0 + s0:c0 + s0 + slab],
                        preferred_element_type=F32)
            if out_scale != 1.0:
                r = r * out_scale
            if kind == "tok":
                o_ref[0, :, s0:s0 + slab] = r.astype(BF16)
            elif kind == "head" and width >= slab:
                hh, off = divmod(s0, width)
                o_ref[0, hh, :, off:off + slab] = r.astype(BF16)
            else:
                for p in range(slab // width):
                    head = s0 // width + p
                    piece = r[:, p * width:(p + 1) * width]
                    if kind == "head":
                        o_ref[0, head] = piece.astype(BF16)
                    else:
                        o_ref[0, head, 0] = piece.T.astype(BF16)
                    if norm_row is not None:
                        norm2 = jnp.max(jnp.sum(piece * piece, axis=1, keepdims=True),
                                        axis=0, keepdims=True)
                        norm_ref[0, 0, norm_row + head:norm_row + head + 1, :] = (
                            jnp.broadcast_to(jnp.sqrt(norm2), (1, LANES)))


def _in_projection(x, mod, gain, w_bf16, plan, tm, slab):
    b, s, d = x.shape
    n = w_bf16.shape[1]
    out_shapes, out_specs = [], []
    n_norm_rows = sum(e[2] for e in plan if e[5] is not None)
    for (kind, c0, n_heads, width, _, _) in plan:
        if kind == "tok":
            out_shapes.append(jax.ShapeDtypeStruct((b, s, width), BF16))
            out_specs.append(pl.BlockSpec((1, tm, width), lambda bi, i: (bi, i, 0)))
        elif kind == "head":
            out_shapes.append(jax.ShapeDtypeStruct((b, n_heads, s, width), BF16))
            out_specs.append(pl.BlockSpec((1, n_heads, tm, width), lambda bi, i: (bi, 0, i, 0)))
        else:
            assert slab % width == 0
            out_shapes.append(jax.ShapeDtypeStruct((b, n_heads, s // tm, width, tm), BF16))
            out_specs.append(pl.BlockSpec((1, n_heads, 1, width, tm),
                                          lambda bi, i: (bi, 0, i, 0, 0)))
    if n_norm_rows:
        out_shapes.append(jax.ShapeDtypeStruct((b, s // tm, n_norm_rows, LANES), F32))
        out_specs.append(pl.BlockSpec((1, 1, n_norm_rows, LANES), lambda bi, i: (bi, i, 0, 0)))
    return pl.pallas_call(
        functools.partial(_inproj_kernel, d_model=d, plan=plan, slab=slab,
                          n_norm_rows=n_norm_rows),
        grid=(b, s // tm),
        in_specs=[
            pl.BlockSpec((1, tm, d), lambda bi, i: (bi, i, 0)),
            pl.BlockSpec((1, 1, 3 * d), lambda bi, i: (bi, 0, 0)),
            pl.BlockSpec((1, d), lambda bi, i: (0, 0)),
            pl.BlockSpec((d, n), lambda bi, i: (0, 0)),
        ],
        out_specs=out_specs,
        out_shape=out_shapes,
        scratch_shapes=[pltpu.VMEM((tm, d), BF16)],
        compiler_params=_cparams(("arbitrary", "arbitrary")),
        name="in_projection",
    )(x, mod, gain, w_bf16)


def _outproj_kernel(y_ref, w_ref, x_ref, mod_ref, gain_ref, o_ref, *, d_model):
    t = jnp.dot(y_ref[0], w_ref[...], preferred_element_type=F32)
    ms = jnp.mean(t * t, axis=-1, keepdims=True)
    n = t * lax.rsqrt(ms + NORM_EPS) * gain_ref[...]
    gate = mod_ref[0, :, 2 * d_model:3 * d_model]
    o_ref[0] = x_ref[0] + gate * n


def _out_projection(y, w_bf16, x, mod, gain, tm):
    b, s, d = x.shape
    k = y.shape[-1]
    return pl.pallas_call(
        functools.partial(_outproj_kernel, d_model=d),
        grid=(b, s // tm),
        in_specs=[
            pl.BlockSpec((1, tm, k), lambda bi, i: (bi, i, 0)),
            pl.BlockSpec((k, d), lambda bi, i: (0, 0)),
            pl.BlockSpec((1, tm, d), lambda bi, i: (bi, i, 0)),
            pl.BlockSpec((1, 1, 3 * d), lambda bi, i: (bi, 0, 0)),
            pl.BlockSpec((1, d), lambda bi, i: (0, 0)),
        ],
        out_specs=pl.BlockSpec((1, tm, d), lambda bi, i: (bi, i, 0)),
        out_shape=jax.ShapeDtypeStruct((b, s, d), F32),
        compiler_params=_cparams(("arbitrary", "arbitrary")),
        name="out_projection",
    )(y, w_bf16, x, mod, gain)


N_ALIBI_FEATURES = 6
ATTN_ROW_CHUNK = 16
ATTN_SUM_ROWS = 16
ATTN_SCORE_SLOTS = 4
ATTN_PROB_SLOTS = 2
ATTN_DIAG_STEPS = 8
ATTN_PAST_STEPS = 8
ATTN_PIPE_DEPTH = 3
ATTN_LAMBDA_ROWS = 32
ATTN_STAT_ROWS = 4
ATTN_ZERO_EXP2 = 150.0
ATTN_BOUND_SLACK = 1.0
ATTN_NORM_SLACK = 1.01


def _alibi_key_features(s):
    pos = jnp.arange(s, dtype=jnp.int32)[:, None]
    col = jnp.arange(LANES, dtype=jnp.int32)[None, :]
    lo = (pos % BF16_EXACT_INT).astype(F32)
    hi = (pos // BF16_EXACT_INT).astype(F32)
    feat = jnp.where(col < 3, lo, jnp.where(col < N_ALIBI_FEATURES, hi, 0.0))
    return feat.astype(BF16)


def _attn_schedule(nq):
    pairs = [(qi, qi) for qi in range(nq)] + [(qi, j) for qi in range(nq) for j in range(qi)]
    qb, kb = zip(*pairs)
    return np.asarray(qb, np.int32), np.asarray(kb, np.int32)


def _attn_kernel(slope_ref, qb_ref, kb_ref, qn_ref, kn_ref, qT_ref, k_ref, kf_ref, vT_ref, g_ref,
                 lamv_ref, subln_ref, y_ref, qs_ref, feat_ref, *scratch, tq, nq, lam_init):
    h = pl.program_id(1)
    tk = tq
    w = 2 * tq
    rc = ATTN_ROW_CHUNK
    n_chunks = tk // rc
    n_pos = nq * (nq + 1) // 2
    scratch = list(scratch)
    s_refs = [scratch.pop(0) for _ in range(ATTN_SCORE_SLOTS)]
    mc_refs = [scratch.pop(0) for _ in range(ATTN_SCORE_SLOTS)]
    p_refs = [scratch.pop(0) for _ in range(ATTN_PROB_SLOTS)]
    al_refs = [scratch.pop(0) for _ in range(ATTN_PROB_SLOTS)]
    m_ref, acc_ref, corr_ref, thr_ref, ql_ref, kl_ref = scratch
    n_past = n_pos - nq
    assert nq % ATTN_DIAG_STEPS == 0 and nq >= ATTN_DIAG_STEPS
    assert ATTN_DIAG_STEPS % ATTN_SCORE_SLOTS == 0 and ATTN_PAST_STEPS % ATTN_SCORE_SLOTS == 0
    assert n_past % ATTN_PAST_STEPS == 0

    slope2 = slope_ref[h] * LOG2E
    fr = lax.broadcasted_iota(jnp.int32, (LANES, w), 0)
    x = jnp.full((LANES, w), slope2, F32)
    hi = x.astype(BF16).astype(F32)
    mid = (x - hi).astype(BF16).astype(F32)
    lo = x - hi - mid
    part = jnp.where((fr == 0) | (fr == 3), hi, jnp.where((fr == 1) | (fr == 4), mid, lo))
    feat = jnp.where(fr < 3, part, jnp.where(fr < N_ALIBI_FEATURES, part * BF16_EXACT_INT, 0.0))
    feat_ref[...] = feat.astype(BF16)
    kk = lax.broadcasted_iota(jnp.int32, (tk, w), 0)
    qq = lax.broadcasted_iota(jnp.int32, (tk, w), 1)
    qq = jnp.where(qq >= tq, qq - tq, qq)
    ahead = (kk - qq).astype(F32)
    same_chunk = (kk >> CHUNK_SHIFT) == (qq >> CHUNK_SHIFT)
    corr_ref[...] = jnp.where(kk <= qq, 0.0,
                              jnp.where(same_chunk, -2.0 * slope2 * ahead, MASK_VALUE))

    def init_query_block(qi, carry):
        qT = qT_ref[0, 0, qi]
        row = lax.broadcasted_iota(jnp.int32, qT.shape, 0)
        zero = jnp.zeros_like(qT)
        qs_ref[qi, :, 0:tq] = jnp.where(row < DA_HEAD_DIM, qT, zero)
        qs_ref[qi, :, tq:w] = jnp.where(row >= DA_HEAD_DIM, qT, zero)
        return carry

    lax.fori_loop(0, nq, init_query_block, 0)
    ones_rows = jnp.ones((ATTN_SUM_ROWS, tk), BF16)

    def qk(pair, t, s_slot, diag):
        qi, kj = pair(t)
        start = kj * tk if isinstance(kj, int) else pl.multiple_of(kj * tk, tk)
        ka = jnp.concatenate([k_ref[0, 0, pl.ds(start, tk), :], kf_ref[pl.ds(start, tk), :]],
                             axis=1)
        qa = jnp.concatenate([qs_ref[qi], feat_ref[...]], axis=0)
        s = jnp.dot(ka, qa, preferred_element_type=F32)
        if diag:
            s = s + corr_ref[...]
        s_refs[s_slot][...] = s
        mc_refs[s_slot][0:1, :] = jnp.max(s, axis=0, keepdims=True)

    def sm(pair, t, s_slot, p_slot, first):
        qi, _ = pair(t)
        if first:
            m_next = mc_refs[s_slot][0:1, :]
        else:
            m_prev = m_ref[qi]
            m_next = jnp.maximum(m_prev, mc_refs[s_slot][0:1, :])
            al_refs[p_slot][0:1, :] = jnp.exp2(m_prev - m_next)
        m_ref[qi] = m_next
        m_rows = jnp.broadcast_to(m_next, (rc, w))
        for c in range(n_chunks):
            p = jnp.exp2(s_refs[s_slot][c * rc:(c + 1) * rc, :] - m_rows)
            p_refs[p_slot][c * rc:(c + 1) * rc, :] = p.astype(BF16)

    def pv(pair, t, p_slot, first):
        qi, kj = pair(t)
        v_aug = jnp.concatenate([vT_ref[0, 0, kj], ones_rows], axis=0)
        upd = jnp.dot(v_aug, p_refs[p_slot][...], preferred_element_type=F32)
        if first:
            acc_ref[qi] = upd
        else:
            acc_ref[qi] = al_refs[p_slot][0:1, :] * acc_ref[qi] + upd

    def run_pipeline(pair, n, steps, diag):
        depth = ATTN_PIPE_DEPTH

        def time_step(tau, u, do_qk=True, do_sm=True, do_pv=True):
            if do_qk:
                qk(pair, tau, u, diag)
            if do_sm:
                sm(pair, tau - (depth - 1), (u - (depth - 1)) % ATTN_SCORE_SLOTS,
                   (u - (depth - 1)) % ATTN_PROB_SLOTS, diag)
            if do_pv:
                pv(pair, tau - depth, (u - depth) % ATTN_PROB_SLOTS, diag)

        for tau in range(depth):
            time_step(tau, tau % ATTN_SCORE_SLOTS, do_sm=tau >= depth - 1, do_pv=False)

        def body(i, carry):
            for u in range(steps):
                time_step(depth + steps * i + u, (depth + u) % ATTN_SCORE_SLOTS)
            return carry

        lax.fori_loop(0, n // steps - 1, body, 0)
        for u in range(steps - depth):
            time_step(n - (steps - depth) + u, (depth + u) % ATTN_SCORE_SLOTS)
        for d in range(depth):
            time_step(n + d, d % ATTN_SCORE_SLOTS, do_qk=False, do_sm=d < depth - 1)

    run_pipeline(lambda t: (t, t), nq, ATTN_DIAG_STEPS, diag=True)

    norm_base = (pl.program_id(0) * pl.num_programs(1) + h) * nq
    m_min = jnp.min(m_ref[...].reshape(nq, w), axis=1, keepdims=True)
    for qi in range(nq):
        thr_ref[qi] = m_min[qi, 0] - (ATTN_ZERO_EXP2 + ATTN_BOUND_SLACK)

    def choose(i, carry):
        count, pad_q, pad_k = carry
        qi, kj = qb_ref[nq + i], kb_ref[nq + i]
        last_key = ((kj + 1) * tk - 1).astype(F32)
        bound = (qn_ref[norm_base + qi] * kn_ref[norm_base + kj] * ATTN_NORM_SLACK
                 + slope2 * last_key)
        keep = bound >= thr_ref[qi]
        ql_ref[count] = qi
        kl_ref[count] = kj
        return (count + keep.astype(jnp.int32), jnp.where(keep, pad_q, qi),
                jnp.where(keep, pad_k, kj))

    n_keep, pad_q, pad_k = lax.fori_loop(0, n_past, choose,
                                         (jnp.int32(0), jnp.int32(0), jnp.int32(0)))
    n_pad = (-n_keep) & (ATTN_PAST_STEPS - 1)
    for r in range(ATTN_PAST_STEPS - 1):
        @pl.when(r < n_pad)
        def _():
            ql_ref[n_keep + r] = pad_q
            kl_ref[n_keep + r] = pad_k

    n_visit = n_keep + n_pad

    @pl.when(n_visit > 0)
    def _():
        run_pipeline(lambda t: (ql_ref[t], kl_ref[t]), n_visit, ATTN_PAST_STEPS, diag=False)

    lv = lamv_ref[...]
    lam = (jnp.exp(jnp.sum(lv[0:1] * lv[1:2], axis=-1, keepdims=True))
           - jnp.exp(jnp.sum(lv[2:3] * lv[3:4], axis=-1, keepdims=True)) + lam_init)

    def finish_query_block(qi, carry):
        o = acc_ref[qi, 0:DA_V_DIM, :] * (1.0 / acc_ref[qi, DA_V_DIM:DA_V_DIM + 1, :])
        out = o[:, 0:tq] - lam * o[:, tq:w]
        ms = jnp.mean(out * out, axis=0, keepdims=True)
        out = out * lax.rsqrt(ms + HEAD_NORM_EPS) * subln_ref[...] * (1.0 - lam_init)
        rows = pl.ds(pl.multiple_of(qi * tq, tq), tq)
        g = g_ref[0, rows, :].astype(F32)
        y_ref[0, rows, :] = (g * jax.nn.sigmoid(g) * out.T).astype(BF16)
        return carry

    def finish_two(i, carry):
        finish_query_block(2 * i, carry)
        return finish_query_block(2 * i + 1, carry)

    lax.fori_loop(0, nq // 2, finish_two, 0)


def _diff_attention(qT, k, vT, g, q_norms, k_norms, slopes, lamv, subln, lam_init):
    b, nh, nq, dk, tq = qT.shape
    s = k.shape[2]
    w = 2 * tq
    qb, kb = _attn_schedule(nq)
    kernel = functools.partial(_attn_kernel, tq=tq, nq=nq, lam_init=lam_init)
    smem = pl.BlockSpec(memory_space=pltpu.SMEM)
    once = pl.Buffered(1)
    return pl.pallas_call(
        kernel,
        grid=(b, nh),
        in_specs=[
            smem, smem, smem, smem, smem,
            pl.BlockSpec((1, 1, nq, dk, tq), lambda bi, h: (bi, h, 0, 0, 0), pipeline_mode=once),
            pl.BlockSpec((1, 1, s, dk), lambda bi, h: (bi, h, 0, 0)),
            pl.BlockSpec((s, LANES), lambda bi, h: (0, 0), pipeline_mode=once),
            pl.BlockSpec((1, 1, nq, DA_V_DIM, tq), lambda bi, h: (bi, h, 0, 0, 0)),
            pl.BlockSpec((1, s, DA_V_DIM), lambda bi, h: (bi, 0, h)),
            pl.BlockSpec((ATTN_LAMBDA_ROWS, LANES), lambda bi, h: (0, 0)),
            pl.BlockSpec((DA_V_DIM, 1), lambda bi, h: (0, 0)),
        ],
        out_specs=pl.BlockSpec((1, s, DA_V_DIM), lambda bi, h: (bi, 0, h)),
        out_shape=jax.ShapeDtypeStruct((b, s, nh * DA_V_DIM), BF16),
        scratch_shapes=(
            [pltpu.VMEM((nq, LANES, w), BF16),
             pltpu.VMEM((LANES, w), BF16)]
            + [pltpu.VMEM((tq, w), F32)] * ATTN_SCORE_SLOTS
            + [pltpu.VMEM((ATTN_STAT_ROWS, w), F32)] * ATTN_SCORE_SLOTS
            + [pltpu.VMEM((tq, w), BF16)] * ATTN_PROB_SLOTS
            + [pltpu.VMEM((ATTN_STAT_ROWS, w), F32)] * ATTN_PROB_SLOTS
            + [pltpu.VMEM((nq, 1, w), F32),
               pltpu.VMEM((nq, DA_V_DIM + ATTN_SUM_ROWS, w), F32),
               pltpu.VMEM((tq, w), F32),
               pltpu.SMEM((nq,), F32),
               pltpu.SMEM((len(qb),), jnp.int32),
               pltpu.SMEM((len(qb),), jnp.int32)]
        ),
        compiler_params=_cparams(("arbitrary", "arbitrary")),
        name="diff_attention",
    )(slopes, jnp.asarray(qb), jnp.asarray(kb), q_norms.reshape(-1), k_norms.reshape(-1), qT, k,
      _alibi_key_features(s), vT, g, lamv, subln.reshape(DA_V_DIM, 1))


def _ret_kernel(lg_ref, q_ref, k_ref, v_ref, g_ref, y_ref, state_ref, decay_ref, qdec_ref,
                kdec_ref, *, blk):
    n = pl.program_id(1)
    dk, dv = RET_QK_DIM, RET_V_DIM

    @pl.when(n == 0)
    def _():
        state_ref[...] = jnp.zeros(state_ref.shape, F32)
        ti = lax.broadcasted_iota(jnp.int32, (blk, blk), 0)
        ui = lax.broadcasted_iota(jnp.int32, (blk, blk), 1)
        dist = jnp.abs(ti - ui).astype(F32)
        allowed = (ui >> CHUNK_SHIFT) <= (ti >> CHUNK_SHIFT)
        t = lax.broadcasted_iota(jnp.int32, (blk, dk), 0).astype(F32)
        for h in range(RET_HEADS):
            decay_ref[h] = jnp.where(allowed, jnp.exp(lg_ref[h] * dist), 0.0) * dk ** -0.5
            qdec_ref[h] = jnp.exp(lg_ref[h] * t)
            kdec_ref[h] = jnp.exp(lg_ref[h] * (blk - t)) * dk ** -0.5

    for h in range(RET_HEADS):
        lg = lg_ref[h]
        q = q_ref[0, :, h * dk:(h + 1) * dk]
        k = k_ref[0, :, h * dk:(h + 1) * dk]
        v = v_ref[0, :, h * dv:(h + 1) * dv]
        qd = (q.astype(F32) * qdec_ref[h]).astype(BF16)
        kd = (k.astype(F32) * kdec_ref[h]).astype(BF16)

        a = lax.dot_general(q, k, (((1,), (1,)), ((), ())), preferred_element_type=F32)
        a = (a * decay_ref[h]).astype(BF16)
        out = jnp.dot(a, v, preferred_element_type=F32)
        out = out + jnp.dot(qd, state_ref[h].astype(BF16), preferred_element_type=F32)
        kv = lax.dot_general(kd, v, (((0,), (0,)), ((), ())), preferred_element_type=F32)
        block_decay = jnp.exp(lg * jnp.full((1, 1), float(blk), F32))
        state_ref[h] = block_decay * state_ref[h] + kv

        ss = jnp.sum(out * out, axis=-1, keepdims=True)
        out = out * lax.rsqrt(ss * (1.0 / dv) + HEAD_NORM_EPS)
        g = g_ref[0, :, h * dv:(h + 1) * dv].astype(F32)
        y_ref[0, :, h * dv:(h + 1) * dv] = (g * jax.nn.sigmoid(g) * out).astype(BF16)


def _retention(q, k, v, g, log_gammas, blk):
    b, s, qk_w = q.shape
    v_w = v.shape[-1]
    return pl.pallas_call(
        functools.partial(_ret_kernel, blk=blk),
        grid=(b, s // blk),
        in_specs=[
            pl.BlockSpec(memory_space=pltpu.SMEM),
            pl.BlockSpec((1, blk, qk_w), lambda bi, i: (bi, i, 0)),
            pl.BlockSpec((1, blk, qk_w), lambda bi, i: (bi, i, 0)),
            pl.BlockSpec((1, blk, v_w), lambda bi, i: (bi, i, 0)),
            pl.BlockSpec((1, blk, v_w), lambda bi, i: (bi, i, 0)),
        ],
        out_specs=pl.BlockSpec((1, blk, v_w), lambda bi, i: (bi, i, 0)),
        out_shape=jax.ShapeDtypeStruct((b, s, v_w), BF16),
        scratch_shapes=[
            pltpu.VMEM((RET_HEADS, RET_QK_DIM, RET_V_DIM), F32),
            pltpu.VMEM((RET_HEADS, blk, blk), F32),
            pltpu.VMEM((RET_HEADS, blk, RET_QK_DIM), F32),
            pltpu.VMEM((RET_HEADS, blk, RET_QK_DIM), F32),
        ],
        compiler_params=_cparams(("arbitrary", "arbitrary")),
        name="retention",
    )(log_gammas, q, k, v, g)


def kernel(x, c, ada_w, ada_b, pre_gain, post_gain, da_w_in, da_w_out, da_lambda_q1, da_lambda_k1,
           da_lambda_q2, da_lambda_k2, da_subln_gain, ret_w_in, ret_w_out):
    b, s, d = x.shape
    mod = _modulation(c, ada_w, ada_b)

    attn_tile = 512
    qk_w = 2 * DA_HEADS * DA_HEAD_DIM
    v_w = DA_HEADS * DA_V_DIM
    q_scale = LOG2E * DA_HEAD_DIM ** -0.5
    da_plan = (("headT", 0, DA_HEADS, 2 * DA_HEAD_DIM, q_scale, 0),
               ("head", qk_w, DA_HEADS, 2 * DA_HEAD_DIM, 1.0, DA_HEADS),
               ("headT", 2 * qk_w, DA_HEADS, DA_V_DIM, 1.0, None),
               ("tok", 2 * qk_w + v_w, None, v_w, 1.0, None))
    mod0 = mod[0].reshape(b, 1, 3 * d)
    qT, k, vT, g, norms = _in_projection(x, mod0, pre_gain[0:1], da_w_in[0].astype(BF16),
                                         da_plan, tm=attn_tile, slab=256)
    q_norms = jnp.transpose(norms[:, :, 0:DA_HEADS, 0], (0, 2, 1))
    k_norms = jnp.transpose(norms[:, :, DA_HEADS:2 * DA_HEADS, 0], (0, 2, 1))
    lam_init = 0.8 - 0.6 * math.exp(-0.3 * 0)
    slopes = jnp.asarray([2.0 ** (-8.0 * (hh + 1) / DA_HEADS) for hh in range(DA_HEADS)], F32)
    lamv = jnp.stack([da_lambda_q1[0], da_lambda_k1[0], da_lambda_q2[0], da_lambda_k2[0]])
    lamv = jnp.pad(lamv, ((0, ATTN_LAMBDA_ROWS - 4), (0, LANES - DA_HEAD_DIM)))
    y = _diff_attention(qT, k, vT, g, q_norms, k_norms, slopes, lamv, da_subln_gain[0], lam_init)
    x = _out_projection(y, da_w_out[0].astype(BF16), x, mod0, post_gain[0:1], tm=512)

    rqk = RET_HEADS * RET_QK_DIM
    rv = RET_HEADS * RET_V_DIM
    ret_plan = (("tok", 0, None, rqk, 1.0, None), ("tok", rqk, None, rqk, 1.0, None),
                ("tok", 2 * rqk, None, rv, 1.0, None), ("tok", 2 * rqk + rv, None, rv, 1.0, None))
    mod1 = mod[1].reshape(b, 1, 3 * d)
    q, k, v, g = _in_projection(x, mod1, pre_gain[1:2], ret_w_in[0].astype(BF16), ret_plan,
                                tm=512, slab=256)
    gammas = 1.0 - 2.0 ** (-5.0 - jnp.arange(RET_HEADS, dtype=F32))
    y = _retention(q, k, v, g, jnp.log(gammas), blk=256)
    return _out_projection(y, ret_w_out[0].astype(BF16), x, mod1, post_gain[1:2], tm=512)
```

```python
import functools
import math

import jax
import jax.numpy as jnp
import numpy as np
from jax import lax
from jax.experimental import pallas as pl
from jax.experimental.pallas import tpu as pltpu

CHUNK = 64
CHUNK_SHIFT = 6
DA_HEADS = 8
DA_HEAD_DIM = 64
DA_V_DIM = 128
RET_HEADS = 4
RET_QK_DIM = 256
RET_V_DIM = 512
NORM_EPS = 1e-6
HEAD_NORM_EPS = 1e-5
MASK_VALUE = -1e30
LOG2E = 1.4426950408889634

V7X_VMEM_LIMIT_BYTES = 56 * 1024 * 1024
LANES = 128
SUBLANES = 8
BF16_EXACT_INT = 256

BF16 = jnp.bfloat16
F32 = jnp.float32


def _cparams(sem, vmem=V7X_VMEM_LIMIT_BYTES, flags=None):
    return pltpu.CompilerParams(dimension_semantics=sem, vmem_limit_bytes=vmem, flags=flags)


def _mod_kernel(c_ref, w_ref, b_ref, o_ref):
    c = c_ref[...]
    cond = c * jax.nn.sigmoid(c)
    o_ref[0] = jnp.dot(cond, w_ref[0], preferred_element_type=F32) + b_ref[0]


def _modulation(c, ada_w, ada_b):
    depth, d, d3 = ada_w.shape
    b = c.shape[0]
    n_tiles = d3 // d
    return pl.pallas_call(
        _mod_kernel,
        grid=(depth, n_tiles),
        in_specs=[
            pl.BlockSpec((b, d), lambda l, j: (0, 0)),
            pl.BlockSpec((1, d, d), lambda l, j: (l, 0, j)),
            pl.BlockSpec((1, 1, d), lambda l, j: (l, 0, j)),
        ],
        out_specs=pl.BlockSpec((1, b, d), lambda l, j: (l, 0, j)),
        out_shape=jax.ShapeDtypeStruct((depth, b, d3), F32),
        compiler_params=_cparams(("arbitrary", "arbitrary")),
        name="modulation",
    )(c, ada_w, ada_b.reshape(depth, 1, d3))


def _inproj_kernel(x_ref, mod_ref, gain_ref, w_ref, *refs, d_model, plan, slab, n_norm_rows):
    out_refs = refs[:len(plan)]
    norm_ref = refs[len(plan)] if n_norm_rows else None
    hb_ref = refs[-1]
    x = x_ref[0]
    ms = jnp.mean(x * x, axis=-1, keepdims=True)
    hn = x * lax.rsqrt(ms + NORM_EPS) * gain_ref[...]
    shift = mod_ref[0, :, 0:d_model]
    scale = mod_ref[0, :, d_model:2 * d_model]
    hb_ref[...] = (hn * (1.0 + scale) + shift).astype(BF16)
    for (kind, c0, n_heads, width, out_scale, norm_row), o_ref in zip(plan, out_refs):
        total = width if kind == "tok" else n_heads * width
        for s0 in range(0, total, slab):
            r = jnp.dot(hb_ref[...], w_ref[:, c0 + s0:c0 + s0 + slab],
                        preferred_element_type=F32)
            if out_scale != 1.0:
                r = r * out_scale
            if kind == "tok":
                o_ref[0, :, s0:s0 + slab] = r.astype(BF16)
            elif kind == "head" and width >= slab:
                hh, off = divmod(s0, width)
                o_ref[0, hh, :, off:off + slab] = r.astype(BF16)
            else:
                for p in range(slab // width):
                    head = s0 // width + p
                    piece = r[:, p * width:(p + 1) * width]
                    if kind == "head":
                        o_ref[0, head] = piece.astype(BF16)
                    else:
                        o_ref[0, head, 0] = piece.T.astype(BF16)
                    if norm_row is not None:
                        norm2 = jnp.max(jnp.sum(piece * piece, axis=1, keepdims=True),
                                        axis=0, keepdims=True)
                        norm_ref[0, 0, norm_row + head:norm_row + head + 1, :] = (
                            jnp.broadcast_to(jnp.sqrt(norm2), (1, LANES)))


def _in_projection(x, mod, gain, w_bf16, plan, tm, slab):
    b, s, d = x.shape
    n = w_bf16.shape[1]
    out_shapes, out_specs = [], []
    n_norm_rows = sum(e[2] for e in plan if e[5] is not None)
    for (kind, c0, n_heads, width, _, _) in plan:
        if kind == "tok":
            out_shapes.append(jax.ShapeDtypeStruct((b, s, width), BF16))
            out_specs.append(pl.BlockSpec((1, tm, width), lambda bi, i: (bi, i, 0)))
        elif kind == "head":
            out_shapes.append(jax.ShapeDtypeStruct((b, n_heads, s, width), BF16))
            out_specs.append(pl.BlockSpec((1, n_heads, tm, width), lambda bi, i: (bi, 0, i, 0)))
        else:
            assert slab % width == 0
            out_shapes.append(jax.ShapeDtypeStruct((b, n_heads, s // tm, width, tm), BF16))
            out_specs.append(pl.BlockSpec((1, n_heads, 1, width, tm),
                                          lambda bi, i: (bi, 0, i, 0, 0)))
    if n_norm_rows:
        out_shapes.append(jax.ShapeDtypeStruct((b, s // tm, n_norm_rows, LANES), F32))
        out_specs.append(pl.BlockSpec((1, 1, n_norm_rows, LANES), lambda bi, i: (bi, i, 0, 0)))
    return pl.pallas_call(
        functools.partial(_inproj_kernel, d_model=d, plan=plan, slab=slab,
                          n_norm_rows=n_norm_rows),
        grid=(b, s // tm),
        in_specs=[
            pl.BlockSpec((1, tm, d), lambda bi, i: (bi, i, 0)),
            pl.BlockSpec((1, 1, 3 * d), lambda bi, i: (bi, 0, 0)),
            pl.BlockSpec((1, d), lambda bi, i: (0, 0)),
            pl.BlockSpec((d, n), lambda bi, i: (0, 0)),
        ],
        out_specs=out_specs,
        out_shape=out_shapes,
        scratch_shapes=[pltpu.VMEM((tm, d), BF16)],
        compiler_params=_cparams(("arbitrary", "arbitrary")),
        name="in_projection",
    )(x, mod, gain, w_bf16)


def _outproj_kernel(y_ref, w_ref, x_ref, mod_ref, gain_ref, o_ref, *, d_model):
    t = jnp.dot(y_ref[0], w_ref[...], preferred_element_type=F32)
    ms = jnp.mean(t * t, axis=-1, keepdims=True)
    n = t * lax.rsqrt(ms + NORM_EPS) * gain_ref[...]
    gate = mod_ref[0, :, 2 * d_model:3 * d_model]
    o_ref[0] = x_ref[0] + gate * n


def _out_projection(y, w_bf16, x, mod, gain, tm):
    b, s, d = x.shape
    k = y.shape[-1]
    return pl.pallas_call(
        functools.partial(_outproj_kernel, d_model=d),
        grid=(b, s // tm),
        in_specs=[
            pl.BlockSpec((1, tm, k), lambda bi, i: (bi, i, 0)),
            pl.BlockSpec((k, d), lambda bi, i: (0, 0)),
            pl.BlockSpec((1, tm, d), lambda bi, i: (bi, i, 0)),
            pl.BlockSpec((1, 1, 3 * d), lambda bi, i: (bi, 0, 0)),
            pl.BlockSpec((1, d), lambda bi, i: (0, 0)),
        ],
        out_specs=pl.BlockSpec((1, tm, d), lambda bi, i: (bi, i, 0)),
        out_shape=jax.ShapeDtypeStruct((b, s, d), F32),
        compiler_params=_cparams(("arbitrary", "arbitrary")),
        name="out_projection",
    )(y, w_bf16, x, mod, gain)


N_ALIBI_FEATURES = 6
ATTN_ROW_CHUNK = 16
ATTN_SUM_ROWS = 16
ATTN_SCORE_SLOTS = 4
ATTN_PROB_SLOTS = 2
ATTN_DIAG_STEPS = 8
ATTN_PAST_STEPS = 8
ATTN_PIPE_DEPTH = 3
ATTN_LAMBDA_ROWS = 32
ATTN_STAT_ROWS = 4
ATTN_ZERO_EXP2 = 150.0
ATTN_BOUND_SLACK = 1.0
ATTN_NORM_SLACK = 1.01


def _alibi_key_features(s):
    pos = jnp.arange(s, dtype=jnp.int32)[:, None]
    col = jnp.arange(LANES, dtype=jnp.int32)[None, :]
    lo = (pos % BF16_EXACT_INT).astype(F32)
    hi = (pos // BF16_EXACT_INT).astype(F32)
    feat = jnp.where(col < 3, lo, jnp.where(col < N_ALIBI_FEATURES, hi, 0.0))
    return feat.astype(BF16)


def _attn_schedule(nq):
    pairs = [(qi, qi) for qi in range(nq)] + [(qi, j) for qi in range(nq) for j in range(qi)]
    qb, kb = zip(*pairs)
    return np.asarray(qb, np.int32), np.asarray(kb, np.int32)


def _attn_kernel(slope_ref, qb_ref, kb_ref, qn_ref, kn_ref, qT_ref, k_ref, kf_ref, vT_ref, g_ref,
                 lamv_ref, subln_ref, y_ref, qs_ref, feat_ref, *scratch, tq, nq, lam_init):
    h = pl.program_id(1)
    tk = tq
    w = 2 * tq
    rc = ATTN_ROW_CHUNK
    n_chunks = tk // rc
    n_pos = nq * (nq + 1) // 2
    scratch = list(scratch)
    s_refs = [scratch.pop(0) for _ in range(ATTN_SCORE_SLOTS)]
    mc_refs = [scratch.pop(0) for _ in range(ATTN_SCORE_SLOTS)]
    p_refs = [scratch.pop(0) for _ in range(ATTN_PROB_SLOTS)]
    al_refs = [scratch.pop(0) for _ in range(ATTN_PROB_SLOTS)]
    m_ref, acc_ref, corr_ref, thr_ref, ql_ref, kl_ref = scratch
    n_past = n_pos - nq
    assert nq % ATTN_DIAG_STEPS == 0 and nq >= ATTN_DIAG_STEPS
    assert ATTN_DIAG_STEPS % ATTN_SCORE_SLOTS == 0 and ATTN_PAST_STEPS % ATTN_SCORE_SLOTS == 0
    assert n_past % ATTN_PAST_STEPS == 0

    slope2 = slope_ref[h] * LOG2E
    fr = lax.broadcasted_iota(jnp.int32, (LANES, w), 0)
    x = jnp.full((LANES, w), slope2, F32)
    hi = x.astype(BF16).astype(F32)
    mid = (x - hi).astype(BF16).astype(F32)
    lo = x - hi - mid
    part = jnp.where((fr == 0) | (fr == 3), hi, jnp.where((fr == 1) | (fr == 4), mid, lo))
    feat = jnp.where(fr < 3, part, jnp.where(fr < N_ALIBI_FEATURES, part * BF16_EXACT_INT, 0.0))
    feat_ref[...] = feat.astype(BF16)
    kk = lax.broadcasted_iota(jnp.int32, (tk, w), 0)
    qq = lax.broadcasted_iota(jnp.int32, (tk, w), 1)
    qq = jnp.where(qq >= tq, qq - tq, qq)
    ahead = (kk - qq).astype(F32)
    same_chunk = (kk >> CHUNK_SHIFT) == (qq >> CHUNK_SHIFT)
    corr_ref[...] = jnp.where(kk <= qq, 0.0,
                              jnp.where(same_chunk, -2.0 * slope2 * ahead, MASK_VALUE))

    def init_query_block(qi, carry):
        qT = qT_ref[0, 0, qi]
        row = lax.broadcasted_iota(jnp.int32, qT.shape, 0)
        zero = jnp.zeros_like(qT)
        qs_ref[qi, :, 0:tq] = jnp.where(row < DA_HEAD_DIM, qT, zero)
        qs_ref[qi, :, tq:w] = jnp.where(row >= DA_HEAD_DIM, qT, zero)
        return carry

    lax.fori_loop(0, nq, init_query_block, 0)
    ones_rows = jnp.ones((ATTN_SUM_ROWS, tk), BF16)

    def qk(pair, t, s_slot, diag):
        qi, kj = pair(t)
        start = kj * tk if isinstance(kj, int) else pl.multiple_of(kj * tk, tk)
        ka = jnp.concatenate([k_ref[0, 0, pl.ds(start, tk), :], kf_ref[pl.ds(start, tk), :]],
                             axis=1)
        qa = jnp.concatenate([qs_ref[qi], feat_ref[...]], axis=0)
        s = jnp.dot(ka, qa, preferred_element_type=F32)
        if diag:
            s = s + corr_ref[...]
        s_refs[s_slot][...] = s
        mc_refs[s_slot][0:1, :] = jnp.max(s, axis=0, keepdims=True)

    def sm(pair, t, s_slot, p_slot, first):
        qi, _ = pair(t)
        if first:
            m_next = mc_refs[s_slot][0:1, :]
        else:
            m_prev = m_ref[qi]
            m_next = jnp.maximum(m_prev, mc_refs[s_slot][0:1, :])
            al_refs[p_slot][0:1, :] = jnp.exp2(m_prev - m_next)
        m_ref[qi] = m_next
        m_rows = jnp.broadcast_to(m_next, (rc, w))
        for c in range(n_chunks):
            p = jnp.exp2(s_refs[s_slot][c * rc:(c + 1) * rc, :] - m_rows)
            p_refs[p_slot][c * rc:(c + 1) * rc, :] = p.astype(BF16)

    def pv(pair, t, p_slot, first):
        qi, kj = pair(t)
        v_aug = jnp.concatenate([vT_ref[0, 0, kj], ones_rows], axis=0)
        upd = jnp.dot(v_aug, p_refs[p_slot][...], preferred_element_type=F32)
        if first:
            acc_ref[qi] = upd
        else:
            acc_ref[qi] = al_refs[p_slot][0:1, :] * acc_ref[qi] + upd

    def run_pipeline(pair, n, steps, diag):
        depth = ATTN_PIPE_DEPTH

        def time_step(tau, u, do_qk=True, do_sm=True, do_pv=True):
            if do_qk:
                qk(pair, tau, u, diag)
            if do_sm:
                sm(pair, tau - (depth - 1), (u - (depth - 1)) % ATTN_SCORE_SLOTS,
                   (u - (depth - 1)) % ATTN_PROB_SLOTS, diag)
            if do_pv:
                pv(pair, tau - depth, (u - depth) % ATTN_PROB_SLOTS, diag)

        for tau in range(depth):
            time_step(tau, tau % ATTN_SCORE_SLOTS, do_sm=tau >= depth - 1, do_pv=False)

        def body(i, carry):
            for u in range(steps):
                time_step(depth + steps * i + u, (depth + u) % ATTN_SCORE_SLOTS)
            return carry

        lax.fori_loop(0, n // steps - 1, body, 0)
        for u in range(steps - depth):
            time_step(n - (steps - depth) + u, (depth + u) % ATTN_SCORE_SLOTS)
        for d in range(depth):
            time_step(n + d, d % ATTN_SCORE_SLOTS, do_qk=False, do_sm=d < depth - 1)

    run_pipeline(lambda t: (t, t), nq, ATTN_DIAG_STEPS, diag=True)

    norm_base = (pl.program_id(0) * pl.num_programs(1) + h) * nq
    m_min = jnp.min(m_ref[...].reshape(nq, w), axis=1, keepdims=True)
    for qi in range(nq):
        thr_ref[qi] = m_min[qi, 0] - (ATTN_ZERO_EXP2 + ATTN_BOUND_SLACK)

    def choose(i, carry):
        count, pad_q, pad_k = carry
        qi, kj = qb_ref[nq + i], kb_ref[nq + i]
        last_key = ((kj + 1) * tk - 1).astype(F32)
        bound = (qn_ref[norm_base + qi] * kn_ref[norm_base + kj] * ATTN_NORM_SLACK
                 + slope2 * last_key)
        keep = bound >= thr_ref[qi]
        ql_ref[count] = qi
        kl_ref[count] = kj
        return (count + keep.astype(jnp.int32), jnp.where(keep, pad_q, qi),
                jnp.where(keep, pad_k, kj))

    n_keep, pad_q, pad_k = lax.fori_loop(0, n_past, choose,
                                         (jnp.int32(0), jnp.int32(0), jnp.int32(0)))
    n_pad = (-n_keep) & (ATTN_PAST_STEPS - 1)
    for r in range(ATTN_PAST_STEPS - 1):
        @pl.when(r < n_pad)
        def _():
            ql_ref[n_keep + r] = pad_q
            kl_ref[n_keep + r] = pad_k

    n_visit = n_keep + n_pad

    @pl.when(n_visit > 0)
    def _():
        run_pipeline(lambda t: (ql_ref[t], kl_ref[t]), n_visit, ATTN_PAST_STEPS, diag=False)

    lv = lamv_ref[...]
    lam = (jnp.exp(jnp.sum(lv[0:1] * lv[1:2], axis=-1, keepdims=True))
           - jnp.exp(jnp.sum(lv[2:3] * lv[3:4], axis=-1, keepdims=True)) + lam_init)

    def finish_query_block(qi, carry):
        o = acc_ref[qi, 0:DA_V_DIM, :] * (1.0 / acc_ref[qi, DA_V_DIM:DA_V_DIM + 1, :])
        out = o[:, 0:tq] - lam * o[:, tq:w]
        ms = jnp.mean(out * out, axis=0, keepdims=True)
        out = out * lax.rsqrt(ms + HEAD_NORM_EPS) * subln_ref[...] * (1.0 - lam_init)
        rows = pl.ds(pl.multiple_of(qi * tq, tq), tq)
        g = g_ref[0, rows, :].astype(F32)
        y_ref[0, rows, :] = (g * jax.nn.sigmoid(g) * out.T).astype(BF16)
        return carry

    def finish_two(i, carry):
        finish_query_block(2 * i, carry)
        return finish_query_block(2 * i + 1, carry)

    lax.fori_loop(0, nq // 2, finish_two, 0)


def _diff_attention(qT, k, vT, g, q_norms, k_norms, slopes, lamv, subln, lam_init):
    b, nh, nq, dk, tq = qT.shape
    s = k.shape[2]
    w = 2 * tq
    qb, kb = _attn_schedule(nq)
    kernel = functools.partial(_attn_kernel, tq=tq, nq=nq, lam_init=lam_init)
    smem = pl.BlockSpec(memory_space=pltpu.SMEM)
    once = pl.Buffered(1)
    return pl.pallas_call(
        kernel,
        grid=(b, nh),
        in_specs=[
            smem, smem, smem, smem, smem,
            pl.BlockSpec((1, 1, nq, dk, tq), lambda bi, h: (bi, h, 0, 0, 0), pipeline_mode=once),
            pl.BlockSpec((1, 1, s, dk), lambda bi, h: (bi, h, 0, 0)),
            pl.BlockSpec((s, LANES), lambda bi, h: (0, 0), pipeline_mode=once),
            pl.BlockSpec((1, 1, nq, DA_V_DIM, tq), lambda bi, h: (bi, h, 0, 0, 0)),
            pl.BlockSpec((1, s, DA_V_DIM), lambda bi, h: (bi, 0, h)),
            pl.BlockSpec((ATTN_LAMBDA_ROWS, LANES), lambda bi, h: (0, 0)),
            pl.BlockSpec((DA_V_DIM, 1), lambda bi, h: (0, 0)),
        ],
        out_specs=pl.BlockSpec((1, s, DA_V_DIM), lambda bi, h: (bi, 0, h)),
        out_shape=jax.ShapeDtypeStruct((b, s, nh * DA_V_DIM), BF16),
        scratch_shapes=(
            [pltpu.VMEM((nq, LANES, w), BF16),
             pltpu.VMEM((LANES, w), BF16)]
            + [pltpu.VMEM((tq, w), F32)] * ATTN_SCORE_SLOTS
            + [pltpu.VMEM((ATTN_STAT_ROWS, w), F32)] * ATTN_SCORE_SLOTS
            + [pltpu.VMEM((tq, w), BF16)] * ATTN_PROB_SLOTS
            + [pltpu.VMEM((ATTN_STAT_ROWS, w), F32)] * ATTN_PROB_SLOTS
            + [pltpu.VMEM((nq, 1, w), F32),
               pltpu.VMEM((nq, DA_V_DIM + ATTN_SUM_ROWS, w), F32),
               pltpu.VMEM((tq, w), F32),
               pltpu.SMEM((nq,), F32),
               pltpu.SMEM((len(qb),), jnp.int32),
               pltpu.SMEM((len(qb),), jnp.int32)]
        ),
        compiler_params=_cparams(("arbitrary", "arbitrary")),
        name="diff_attention",
    )(slopes, jnp.asarray(qb), jnp.asarray(kb), q_norms.reshape(-1), k_norms.reshape(-1), qT, k,
      _alibi_key_features(s), vT, g, lamv, subln.reshape(DA_V_DIM, 1))


def _ret_kernel(lg_ref, q_ref, k_ref, v_ref, g_ref, y_ref, state_ref, decay_ref, qdec_ref,
                kdec_ref, *, blk):
    n = pl.program_id(1)
    dk, dv = RET_QK_DIM, RET_V_DIM

    @pl.when(n == 0)
    def _():
        state_ref[...] = jnp.zeros(state_ref.shape, F32)
        ti = lax.broadcasted_iota(jnp.int32, (blk, blk), 0)
        ui = lax.broadcasted_iota(jnp.int32, (blk, blk), 1)
        dist = jnp.abs(ti - ui).astype(F32)
        allowed = (ui >> CHUNK_SHIFT) <= (ti >> CHUNK_SHIFT)
        t = lax.broadcasted_iota(jnp.int32, (blk, dk), 0).astype(F32)
        for h in range(RET_HEADS):
            decay_ref[h] = jnp.where(allowed, jnp.exp(lg_ref[h] * dist), 0.0) * dk ** -0.5
            qdec_ref[h] = jnp.exp(lg_ref[h] * t)
            kdec_ref[h] = jnp.exp(lg_ref[h] * (blk - t)) * dk ** -0.5

    for h in range(RET_HEADS):
        lg = lg_ref[h]
        q = q_ref[0, :, h * dk:(h + 1) * dk]
        k = k_ref[0, :, h * dk:(h + 1) * dk]
        v = v_ref[0, :, h * dv:(h + 1) * dv]
        qd = (q.astype(F32) * qdec_ref[h]).astype(BF16)
        kd = (k.astype(F32) * kdec_ref[h]).astype(BF16)

        a = lax.dot_general(q, k, (((1,), (1,)), ((), ())), preferred_element_type=F32)
        a = (a * decay_ref[h]).astype(BF16)
        out = jnp.dot(a, v, preferred_element_type=F32)
        out = out + jnp.dot(qd, state_ref[h].astype(BF16), preferred_element_type=F32)
        kv = lax.dot_general(kd, v, (((0,), (0,)), ((), ())), preferred_element_type=F32)
        block_decay = jnp.exp(lg * jnp.full((1, 1), float(blk), F32))
        state_ref[h] = block_decay * state_ref[h] + kv

        ss = jnp.sum(out * out, axis=-1, keepdims=True)
        out = out * lax.rsqrt(ss * (1.0 / dv) + HEAD_NORM_EPS)
        g = g_ref[0, :, h * dv:(h + 1) * dv].astype(F32)
        y_ref[0, :, h * dv:(h + 1) * dv] = (g * jax.nn.sigmoid(g) * out).astype(BF16)


def _retention(q, k, v, g, log_gammas, blk):
    b, s, qk_w = q.shape
    v_w = v.shape[-1]
    return pl.pallas_call(
        functools.partial(_ret_kernel, blk=blk),
        grid=(b, s // blk),
        in_specs=[
            pl.BlockSpec(memory_space=pltpu.SMEM),
            pl.BlockSpec((1, blk, qk_w), lambda bi, i: (bi, i, 0)),
            pl.BlockSpec((1, blk, qk_w), lambda bi, i: (bi, i, 0)),
            pl.BlockSpec((1, blk, v_w), lambda bi, i: (bi, i, 0)),
            pl.BlockSpec((1, blk, v_w), lambda bi, i: (bi, i, 0)),
        ],
        out_specs=pl.BlockSpec((1, blk, v_w), lambda bi, i: (bi, i, 0)),
        out_shape=jax.ShapeDtypeStruct((b, s, v_w), BF16),
        scratch_shapes=[
            pltpu.VMEM((RET_HEADS, RET_QK_DIM, RET_V_DIM), F32),
            pltpu.VMEM((RET_HEADS, blk, blk), F32),
            pltpu.VMEM((RET_HEADS, blk, RET_QK_DIM), F32),
            pltpu.VMEM((RET_HEADS, blk, RET_QK_DIM), F32),
        ],
        compiler_params=_cparams(("arbitrary", "arbitrary")),
        name="retention",
    )(log_gammas, q, k, v, g)


def kernel(x, c, ada_w, ada_b, pre_gain, post_gain, da_w_in, da_w_out, da_lambda_q1, da_lambda_k1,
           da_lambda_q2, da_lambda_k2, da_subln_gain, ret_w_in, ret_w_out):
    b, s, d = x.shape
    mod = _modulation(c, ada_w, ada_b)

    attn_tile = 512
    qk_w = 2 * DA_HEADS * DA_HEAD_DIM
    v_w = DA_HEADS * DA_V_DIM
    q_scale = LOG2E * DA_HEAD_DIM ** -0.5
    da_plan = (("headT", 0, DA_HEADS, 2 * DA_HEAD_DIM, q_scale, 0),
               ("head", qk_w, DA_HEADS, 2 * DA_HEAD_DIM, 1.0, DA_HEADS),
               ("headT", 2 * qk_w, DA_HEADS, DA_V_DIM, 1.0, None),
               ("tok", 2 * qk_w + v_w, None, v_w, 1.0, None))
    mod0 = mod[0].reshape(b, 1, 3 * d)
    qT, k, vT, g, norms = _in_projection(x, mod0, pre_gain[0:1], da_w_in[0].astype(BF16),
                                         da_plan, tm=attn_tile, slab=256)
    q_norms = jnp.transpose(norms[:, :, 0:DA_HEADS, 0], (0, 2, 1))
    k_norms = jnp.transpose(norms[:, :, DA_HEADS:2 * DA_HEADS, 0], (0, 2, 1))
    lam_init = 0.8 - 0.6 * math.exp(-0.3 * 0)
    slopes = jnp.asarray([2.0 ** (-8.0 * (hh + 1) / DA_HEADS) for hh in range(DA_HEADS)], F32)
    lamv = jnp.stack([da_lambda_q1[0], da_lambda_k1[0], da_lambda_q2[0], da_lambda_k2[0]])
    lamv = jnp.pad(lamv, ((0, ATTN_LAMBDA_ROWS - 4), (0, LANES - DA_HEAD_DIM)))
    y = _diff_attention(qT, k, vT, g, q_norms, k_norms, slopes, lamv, da_subln_gain[0], lam_init)
    x = _out_projection(y, da_w_out[0].astype(BF16), x, mod0, post_gain[0:1], tm=1024)

    rqk = RET_HEADS * RET_QK_DIM
    rv = RET_HEADS * RET_V_DIM
    ret_plan = (("tok", 0, None, rqk, 1.0, None), ("tok", rqk, None, rqk, 1.0, None),
                ("tok", 2 * rqk, None, rv, 1.0, None), ("tok", 2 * rqk + rv, None, rv, 1.0, None))
    mod1 = mod[1].reshape(b, 1, 3 * d)
    q, k, v, g = _in_projection(x, mod1, pre_gain[1:2], ret_w_in[0].astype(BF16), ret_plan,
                                tm=512, slab=256)
    gammas = 1.0 - 2.0 ** (-5.0 - jnp.arange(RET_HEADS, dtype=F32))
    y = _retention(q, k, v, g, jnp.log(gammas), blk=256)
    return _out_projection(y, ret_w_out[0].astype(BF16), x, mod1, post_gain[1:2], tm=1024)
```

```python
import functools
import math

import jax
import jax.numpy as jnp
import numpy as np
from jax import lax
from jax.experimental import pallas as pl
from jax.experimental.pallas import tpu as pltpu

CHUNK = 64
CHUNK_SHIFT = 6
DA_HEADS = 8
DA_HEAD_DIM = 64
DA_V_DIM = 128
RET_HEADS = 4
RET_QK_DIM = 256
RET_V_DIM = 512
NORM_EPS = 1e-6
HEAD_NORM_EPS = 1e-5
MASK_VALUE = -1e30
LOG2E = 1.4426950408889634

V7X_VMEM_LIMIT_BYTES = 56 * 1024 * 1024
LANES = 128
SUBLANES = 8
BF16_EXACT_INT = 256

BF16 = jnp.bfloat16
F32 = jnp.float32


def _cparams(sem, vmem=V7X_VMEM_LIMIT_BYTES, flags=None):
    return pltpu.CompilerParams(dimension_semantics=sem, vmem_limit_bytes=vmem, flags=flags)


def _mod_kernel(c_ref, w_ref, b_ref, o_ref):
    c = c_ref[...]
    cond = c * jax.nn.sigmoid(c)
    o_ref[0] = jnp.dot(cond, w_ref[0], preferred_element_type=F32) + b_ref[0]


def _modulation(c, ada_w, ada_b):
    depth, d, d3 = ada_w.shape
    b = c.shape[0]
    n_tiles = d3 // d
    return pl.pallas_call(
        _mod_kernel,
        grid=(depth, n_tiles),
        in_specs=[
            pl.BlockSpec((b, d), lambda l, j: (0, 0)),
            pl.BlockSpec((1, d, d), lambda l, j: (l, 0, j)),
            pl.BlockSpec((1, 1, d), lambda l, j: (l, 0, j)),
        ],
        out_specs=pl.BlockSpec((1, b, d), lambda l, j: (l, 0, j)),
        out_shape=jax.ShapeDtypeStruct((depth, b, d3), F32),
        compiler_params=_cparams(("arbitrary", "arbitrary")),
        name="modulation",
    )(c, ada_w, ada_b.reshape(depth, 1, d3))


def _inproj_kernel(x_ref, mod_ref, gain_ref, w_ref, *refs, d_model, plan, slab, n_norm_rows):
    out_refs = refs[:len(plan)]
    norm_ref = refs[len(plan)] if n_norm_rows else None
    hb_ref = refs[-1]
    x = x_ref[0]
    ms = jnp.mean(x * x, axis=-1, keepdims=True)
    hn = x * lax.rsqrt(ms + NORM_EPS) * gain_ref[...]
    shift = mod_ref[0, :, 0:d_model]
    scale = mod_ref[0, :, d_model:2 * d_model]
    hb_ref[...] = (hn * (1.0 + scale) + shift).astype(BF16)
    for (kind, c0, n_heads, width, out_scale, norm_row), o_ref in zip(plan, out_refs):
        total = width if kind == "tok" else n_heads * width
        for s0 in range(0, total, slab):
            r = jnp.dot(hb_ref[...], w_ref[:, c0 + s0:c0 + s0 + slab],
                        preferred_element_type=F32)
            if out_scale != 1.0:
                r = r * out_scale
            if kind == "tok":
                o_ref[0, :, s0:s0 + slab] = r.astype(BF16)
            elif kind == "head" and width >= slab:
                hh, off = divmod(s0, width)
                o_ref[0, hh, :, off:off + slab] = r.astype(BF16)
            else:
                for p in range(slab // width):
                    head = s0 // width + p
                    piece = r[:, p * width:(p + 1) * width]
                    if kind == "head":
                        o_ref[0, head] = piece.astype(BF16)
                    else:
                        o_ref[0, head, 0] = piece.T.astype(BF16)
                    if norm_row is not None:
                        norm2 = jnp.max(jnp.sum(piece * piece, axis=1, keepdims=True),
                                        axis=0, keepdims=True)
                        norm_ref[0, 0, norm_row + head:norm_row + head + 1, :] = (
                            jnp.broadcast_to(jnp.sqrt(norm2), (1, LANES)))


def _in_projection(x, mod, gain, w_bf16, plan, tm, slab):
    b, s, d = x.shape
    n = w_bf16.shape[1]
    out_shapes, out_specs = [], []
    n_norm_rows = sum(e[2] for e in plan if e[5] is not None)
    for (kind, c0, n_heads, width, _, _) in plan:
        if kind == "tok":
            out_shapes.append(jax.ShapeDtypeStruct((b, s, width), BF16))
            out_specs.append(pl.BlockSpec((1, tm, width), lambda bi, i: (bi, i, 0)))
        elif kind == "head":
            out_shapes.append(jax.ShapeDtypeStruct((b, n_heads, s, width), BF16))
            out_specs.append(pl.BlockSpec((1, n_heads, tm, width), lambda bi, i: (bi, 0, i, 0)))
        else:
            assert slab % width == 0
            out_shapes.append(jax.ShapeDtypeStruct((b, n_heads, s // tm, width, tm), BF16))
            out_specs.append(pl.BlockSpec((1, n_heads, 1, width, tm),
                                          lambda bi, i: (bi, 0, i, 0, 0)))
    if n_norm_rows:
        out_shapes.append(jax.ShapeDtypeStruct((b, s // tm, n_norm_rows, LANES), F32))
        out_specs.append(pl.BlockSpec((1, 1, n_norm_rows, LANES), lambda bi, i: (bi, i, 0, 0)))
    return pl.pallas_call(
        functools.partial(_inproj_kernel, d_model=d, plan=plan, slab=slab,
                          n_norm_rows=n_norm_rows),
        grid=(b, s // tm),
        in_specs=[
            pl.BlockSpec((1, tm, d), lambda bi, i: (bi, i, 0)),
            pl.BlockSpec((1, 1, 3 * d), lambda bi, i: (bi, 0, 0)),
            pl.BlockSpec((1, d), lambda bi, i: (0, 0)),
            pl.BlockSpec((d, n), lambda bi, i: (0, 0)),
        ],
        out_specs=out_specs,
        out_shape=out_shapes,
        scratch_shapes=[pltpu.VMEM((tm, d), BF16)],
        compiler_params=_cparams(("arbitrary", "arbitrary")),
        name="in_projection",
    )(x, mod, gain, w_bf16)


def _outproj_kernel(y_ref, w_ref, x_ref, mod_ref, gain_ref, o_ref, *, d_model):
    t = jnp.dot(y_ref[0], w_ref[...], preferred_element_type=F32)
    ms = jnp.mean(t * t, axis=-1, keepdims=True)
    n = t * lax.rsqrt(ms + NORM_EPS) * gain_ref[...]
    gate = mod_ref[0, :, 2 * d_model:3 * d_model]
    o_ref[0] = x_ref[0] + gate * n


def _out_projection(y, w_bf16, x, mod, gain, tm):
    b, s, d = x.shape
    k = y.shape[-1]
    return pl.pallas_call(
        functools.partial(_outproj_kernel, d_model=d),
        grid=(b, s // tm),
        in_specs=[
            pl.BlockSpec((1, tm, k), lambda bi, i: (bi, i, 0)),
            pl.BlockSpec((k, d), lambda bi, i: (0, 0)),
            pl.BlockSpec((1, tm, d), lambda bi, i: (bi, i, 0)),
            pl.BlockSpec((1, 1, 3 * d), lambda bi, i: (bi, 0, 0)),
            pl.BlockSpec((1, d), lambda bi, i: (0, 0)),
        ],
        out_specs=pl.BlockSpec((1, tm, d), lambda bi, i: (bi, i, 0)),
        out_shape=jax.ShapeDtypeStruct((b, s, d), F32),
        compiler_params=_cparams(("arbitrary", "arbitrary")),
        name="out_projection",
    )(y, w_bf16, x, mod, gain)


N_ALIBI_FEATURES = 6
ATTN_ROW_CHUNK = 16
ATTN_SUM_ROWS = 16
ATTN_SCORE_SLOTS = 4
ATTN_PROB_SLOTS = 2
ATTN_DIAG_STEPS = 8
ATTN_PAST_STEPS = 8
ATTN_PIPE_DEPTH = 3
ATTN_LAMBDA_ROWS = 32
ATTN_STAT_ROWS = 4
ATTN_ZERO_EXP2 = 150.0
ATTN_BOUND_SLACK = 1.0
ATTN_NORM_SLACK = 1.01


def _alibi_key_features(s):
    pos = jnp.arange(s, dtype=jnp.int32)[:, None]
    col = jnp.arange(LANES, dtype=jnp.int32)[None, :]
    lo = (pos % BF16_EXACT_INT).astype(F32)
    hi = (pos // BF16_EXACT_INT).astype(F32)
    feat = jnp.where(col < 3, lo, jnp.where(col < N_ALIBI_FEATURES, hi, 0.0))
    return feat.astype(BF16)


def _attn_schedule(nq):
    pairs = [(qi, qi) for qi in range(nq)] + [(qi, j) for qi in range(nq) for j in range(qi)]
    qb, kb = zip(*pairs)
    return np.asarray(qb, np.int32), np.asarray(kb, np.int32)


def _attn_kernel(slope_ref, qb_ref, kb_ref, qn_ref, kn_ref, qT_ref, k_ref, kf_ref, vT_ref, g_ref,
                 lamv_ref, subln_ref, y_ref, qs_ref, feat_ref, *scratch, tq, nq, lam_init):
    h = pl.program_id(1)
    tk = tq
    w = 2 * tq
    rc = ATTN_ROW_CHUNK
    n_chunks = tk // rc
    n_pos = nq * (nq + 1) // 2
    scratch = list(scratch)
    s_refs = [scratch.pop(0) for _ in range(ATTN_SCORE_SLOTS)]
    mc_refs = [scratch.pop(0) for _ in range(ATTN_SCORE_SLOTS)]
    p_refs = [scratch.pop(0) for _ in range(ATTN_PROB_SLOTS)]
    al_refs = [scratch.pop(0) for _ in range(ATTN_PROB_SLOTS)]
    m_ref, acc_ref, corr_ref, thr_ref, ql_ref, kl_ref = scratch
    n_past = n_pos - nq
    assert nq % ATTN_DIAG_STEPS == 0 and nq >= ATTN_DIAG_STEPS
    assert ATTN_DIAG_STEPS % ATTN_SCORE_SLOTS == 0 and ATTN_PAST_STEPS % ATTN_SCORE_SLOTS == 0
    assert n_past % ATTN_PAST_STEPS == 0

    slope2 = slope_ref[h] * LOG2E
    fr = lax.broadcasted_iota(jnp.int32, (LANES, w), 0)
    x = jnp.full((LANES, w), slope2, F32)
    hi = x.astype(BF16).astype(F32)
    mid = (x - hi).astype(BF16).astype(F32)
    lo = x - hi - mid
    part = jnp.where((fr == 0) | (fr == 3), hi, jnp.where((fr == 1) | (fr == 4), mid, lo))
    feat = jnp.where(fr < 3, part, jnp.where(fr < N_ALIBI_FEATURES, part * BF16_EXACT_INT, 0.0))
    feat_ref[...] = feat.astype(BF16)
    kk = lax.broadcasted_iota(jnp.int32, (tk, w), 0)
    qq = lax.broadcasted_iota(jnp.int32, (tk, w), 1)
    qq = jnp.where(qq >= tq, qq - tq, qq)
    ahead = (kk - qq).astype(F32)
    same_chunk = (kk >> CHUNK_SHIFT) == (qq >> CHUNK_SHIFT)
    corr_ref[...] = jnp.where(kk <= qq, 0.0,
                              jnp.where(same_chunk, -2.0 * slope2 * ahead, MASK_VALUE))

    def init_query_block(qi, carry):
        qT = qT_ref[0, 0, qi]
        row = lax.broadcasted_iota(jnp.int32, qT.shape, 0)
        zero = jnp.zeros_like(qT)
        qs_ref[qi, :, 0:tq] = jnp.where(row < DA_HEAD_DIM, qT, zero)
        qs_ref[qi, :, tq:w] = jnp.where(row >= DA_HEAD_DIM, qT, zero)
        return carry

    lax.fori_loop(0, nq, init_query_block, 0)
    ones_rows = jnp.ones((ATTN_SUM_ROWS, tk), BF16)

    def qk(pair, t, s_slot, diag):
        qi, kj = pair(t)
        start = kj * tk if isinstance(kj, int) else pl.multiple_of(kj * tk, tk)
        ka = jnp.concatenate([k_ref[0, 0, pl.ds(start, tk), :], kf_ref[pl.ds(start, tk), :]],
                             axis=1)
        qa = jnp.concatenate([qs_ref[qi], feat_ref[...]], axis=0)
        s = jnp.dot(ka, qa, preferred_element_type=F32)
        if diag:
            s = s + corr_ref[...]
        s_refs[s_slot][...] = s
        mc_refs[s_slot][0:1, :] = jnp.max(s, axis=0, keepdims=True)

    def sm(pair, t, s_slot, p_slot, first):
        qi, _ = pair(t)
        if first:
            m_next = mc_refs[s_slot][0:1, :]
        else:
            m_prev = m_ref[qi]
            m_next = jnp.maximum(m_prev, mc_refs[s_slot][0:1, :])
            al_refs[p_slot][0:1, :] = jnp.exp2(m_prev - m_next)
        m_ref[qi] = m_next
        m_rows = jnp.broadcast_to(m_next, (rc, w))
        for c in range(n_chunks):
            p = jnp.exp2(s_refs[s_slot][c * rc:(c + 1) * rc, :] - m_rows)
            p_refs[p_slot][c * rc:(c + 1) * rc, :] = p.astype(BF16)

    def pv(pair, t, p_slot, first):
        qi, kj = pair(t)
        v_aug = jnp.concatenate([vT_ref[0, 0, kj], ones_rows], axis=0)
        upd = jnp.dot(v_aug, p_refs[p_slot][...], preferred_element_type=F32)
        if first:
            acc_ref[qi] = upd
        else:
            acc_ref[qi] = al_refs[p_slot][0:1, :] * acc_ref[qi] + upd

    def run_pipeline(pair, n, steps, diag):
        depth = ATTN_PIPE_DEPTH

        def time_step(tau, u, do_qk=True, do_sm=True, do_pv=True):
            if do_qk:
                qk(pair, tau, u, diag)
            if do_sm:
                sm(pair, tau - (depth - 1), (u - (depth - 1)) % ATTN_SCORE_SLOTS,
                   (u - (depth - 1)) % ATTN_PROB_SLOTS, diag)
            if do_pv:
                pv(pair, tau - depth, (u - depth) % ATTN_PROB_SLOTS, diag)

        for tau in range(depth):
            time_step(tau, tau % ATTN_SCORE_SLOTS, do_sm=tau >= depth - 1, do_pv=False)

        def body(i, carry):
            for u in range(steps):
                time_step(depth + steps * i + u, (depth + u) % ATTN_SCORE_SLOTS)
            return carry

        lax.fori_loop(0, n // steps - 1, body, 0)
        for u in range(steps - depth):
            time_step(n - (steps - depth) + u, (depth + u) % ATTN_SCORE_SLOTS)
        for d in range(depth):
            time_step(n + d, d % ATTN_SCORE_SLOTS, do_qk=False, do_sm=d < depth - 1)

    run_pipeline(lambda t: (t, t), nq, ATTN_DIAG_STEPS, diag=True)

    norm_base = (pl.program_id(0) * pl.num_programs(1) + h) * nq
    m_min = jnp.min(m_ref[...].reshape(nq, w), axis=1, keepdims=True)
    for qi in range(nq):
        thr_ref[qi] = m_min[qi, 0] - (ATTN_ZERO_EXP2 + ATTN_BOUND_SLACK)

    def choose(i, carry):
        count, pad_q, pad_k = carry
        qi, kj = qb_ref[nq + i], kb_ref[nq + i]
        last_key = ((kj + 1) * tk - 1).astype(F32)
        bound = (qn_ref[norm_base + qi] * kn_ref[norm_base + kj] * ATTN_NORM_SLACK
                 + slope2 * last_key)
        keep = bound >= thr_ref[qi]
        ql_ref[count] = qi
        kl_ref[count] = kj
        return (count + keep.astype(jnp.int32), jnp.where(keep, pad_q, qi),
                jnp.where(keep, pad_k, kj))

    n_keep, pad_q, pad_k = lax.fori_loop(0, n_past, choose,
                                         (jnp.int32(0), jnp.int32(0), jnp.int32(0)))
    n_pad = (-n_keep) & (ATTN_PAST_STEPS - 1)
    for r in range(ATTN_PAST_STEPS - 1):
        @pl.when(r < n_pad)
        def _():
            ql_ref[n_keep + r] = pad_q
            kl_ref[n_keep + r] = pad_k

    n_visit = n_keep + n_pad

    @pl.when(n_visit > 0)
    def _():
        run_pipeline(lambda t: (ql_ref[t], kl_ref[t]), n_visit, ATTN_PAST_STEPS, diag=False)

    lv = lamv_ref[...]
    lam = (jnp.exp(jnp.sum(lv[0:1] * lv[1:2], axis=-1, keepdims=True))
           - jnp.exp(jnp.sum(lv[2:3] * lv[3:4], axis=-1, keepdims=True)) + lam_init)

    def finish_query_block(qi, carry):
        o = acc_ref[qi, 0:DA_V_DIM, :] * (1.0 / acc_ref[qi, DA_V_DIM:DA_V_DIM + 1, :])
        out = o[:, 0:tq] - lam * o[:, tq:w]
        ms = jnp.mean(out * out, axis=0, keepdims=True)
        out = out * lax.rsqrt(ms + HEAD_NORM_EPS) * subln_ref[...] * (1.0 - lam_init)
        rows = pl.ds(pl.multiple_of(qi * tq, tq), tq)
        g = g_ref[0, rows, :].astype(F32)
        y_ref[0, rows, :] = (g * jax.nn.sigmoid(g) * out.T).astype(BF16)
        return carry

    def finish_two(i, carry):
        finish_query_block(2 * i, carry)
        return finish_query_block(2 * i + 1, carry)

    lax.fori_loop(0, nq // 2, finish_two, 0)


def _diff_attention(qT, k, vT, g, q_norms, k_norms, slopes, lamv, subln, lam_init):
    b, nh, nq, dk, tq = qT.shape
    s = k.shape[2]
    w = 2 * tq
    qb, kb = _attn_schedule(nq)
    kernel = functools.partial(_attn_kernel, tq=tq, nq=nq, lam_init=lam_init)
    smem = pl.BlockSpec(memory_space=pltpu.SMEM)
    once = pl.Buffered(1)
    return pl.pallas_call(
        kernel,
        grid=(b, nh),
        in_specs=[
            smem, smem, smem, smem, smem,
            pl.BlockSpec((1, 1, nq, dk, tq), lambda bi, h: (bi, h, 0, 0, 0), pipeline_mode=once),
            pl.BlockSpec((1, 1, s, dk), lambda bi, h: (bi, h, 0, 0)),
            pl.BlockSpec((s, LANES), lambda bi, h: (0, 0), pipeline_mode=once),
            pl.BlockSpec((1, 1, nq, DA_V_DIM, tq), lambda bi, h: (bi, h, 0, 0, 0)),
            pl.BlockSpec((1, s, DA_V_DIM), lambda bi, h: (bi, 0, h)),
            pl.BlockSpec((ATTN_LAMBDA_ROWS, LANES), lambda bi, h: (0, 0)),
            pl.BlockSpec((DA_V_DIM, 1), lambda bi, h: (0, 0)),
        ],
        out_specs=pl.BlockSpec((1, s, DA_V_DIM), lambda bi, h: (bi, 0, h)),
        out_shape=jax.ShapeDtypeStruct((b, s, nh * DA_V_DIM), BF16),
        scratch_shapes=(
            [pltpu.VMEM((nq, LANES, w), BF16),
             pltpu.VMEM((LANES, w), BF16)]
            + [pltpu.VMEM((tq, w), F32)] * ATTN_SCORE_SLOTS
            + [pltpu.VMEM((ATTN_STAT_ROWS, w), F32)] * ATTN_SCORE_SLOTS
            + [pltpu.VMEM((tq, w), BF16)] * ATTN_PROB_SLOTS
            + [pltpu.VMEM((ATTN_STAT_ROWS, w), F32)] * ATTN_PROB_SLOTS
            + [pltpu.VMEM((nq, 1, w), F32),
               pltpu.VMEM((nq, DA_V_DIM + ATTN_SUM_ROWS, w), F32),
               pltpu.VMEM((tq, w), F32),
               pltpu.SMEM((nq,), F32),
               pltpu.SMEM((len(qb),), jnp.int32),
               pltpu.SMEM((len(qb),), jnp.int32)]
        ),
        compiler_params=_cparams(("arbitrary", "arbitrary")),
        name="diff_attention",
    )(slopes, jnp.asarray(qb), jnp.asarray(kb), q_norms.reshape(-1), k_norms.reshape(-1), qT, k,
      _alibi_key_features(s), vT, g, lamv, subln.reshape(DA_V_DIM, 1))


def _ret_kernel(lg_ref, q_ref, k_ref, v_ref, g_ref, y_ref, state_ref, decay_ref, qdec_ref,
                kdec_ref, *, blk):
    n = pl.program_id(1)
    dk, dv = RET_QK_DIM, RET_V_DIM

    @pl.when(n == 0)
    def _():
        state_ref[...] = jnp.zeros(state_ref.shape, F32)
        ti = lax.broadcasted_iota(jnp.int32, (blk, blk), 0)
        ui = lax.broadcasted_iota(jnp.int32, (blk, blk), 1)
        dist = jnp.abs(ti - ui).astype(F32)
        allowed = (ui >> CHUNK_SHIFT) <= (ti >> CHUNK_SHIFT)
        t = lax.broadcasted_iota(jnp.int32, (blk, dk), 0).astype(F32)
        for h in range(RET_HEADS):
            decay_ref[h] = jnp.where(allowed, jnp.exp(lg_ref[h] * dist), 0.0) * dk ** -0.5
            qdec_ref[h] = jnp.exp(lg_ref[h] * t)
            kdec_ref[h] = jnp.exp(lg_ref[h] * (blk - t)) * dk ** -0.5

    def block(rows):
        for h in range(RET_HEADS):
            lg = lg_ref[h]
            q = q_ref[0, rows, h * dk:(h + 1) * dk]
            k = k_ref[0, rows, h * dk:(h + 1) * dk]
            v = v_ref[0, rows, h * dv:(h + 1) * dv]
            qd = (q.astype(F32) * qdec_ref[h]).astype(BF16)
            kd = (k.astype(F32) * kdec_ref[h]).astype(BF16)

            a = lax.dot_general(q, k, (((1,), (1,)), ((), ())), preferred_element_type=F32)
            a = (a * decay_ref[h]).astype(BF16)
            out = jnp.dot(a, v, preferred_element_type=F32)
            out = out + jnp.dot(qd, state_ref[h].astype(BF16), preferred_element_type=F32)
            kv = lax.dot_general(kd, v, (((0,), (0,)), ((), ())), preferred_element_type=F32)
            block_decay = jnp.exp(lg * jnp.full((1, 1), float(blk), F32))
            state_ref[h] = block_decay * state_ref[h] + kv

            ss = jnp.sum(out * out, axis=-1, keepdims=True)
            out = out * lax.rsqrt(ss * (1.0 / dv) + HEAD_NORM_EPS)
            g = g_ref[0, rows, h * dv:(h + 1) * dv].astype(F32)
            y_ref[0, rows, h * dv:(h + 1) * dv] = (g * jax.nn.sigmoid(g) * out).astype(BF16)

    for sub in range(q_ref.shape[1] // blk):
        block(slice(sub * blk, (sub + 1) * blk))


def _retention(q, k, v, g, log_gammas, blk, tile):
    b, s, qk_w = q.shape
    v_w = v.shape[-1]
    assert tile % blk == 0 and s % tile == 0
    return pl.pallas_call(
        functools.partial(_ret_kernel, blk=blk),
        grid=(b, s // tile),
        in_specs=[
            pl.BlockSpec(memory_space=pltpu.SMEM),
            pl.BlockSpec((1, tile, qk_w), lambda bi, i: (bi, i, 0)),
            pl.BlockSpec((1, tile, qk_w), lambda bi, i: (bi, i, 0)),
            pl.BlockSpec((1, tile, v_w), lambda bi, i: (bi, i, 0)),
            pl.BlockSpec((1, tile, v_w), lambda bi, i: (bi, i, 0)),
        ],
        out_specs=pl.BlockSpec((1, tile, v_w), lambda bi, i: (bi, i, 0)),
        out_shape=jax.ShapeDtypeStruct((b, s, v_w), BF16),
        scratch_shapes=[
            pltpu.VMEM((RET_HEADS, RET_QK_DIM, RET_V_DIM), F32),
            pltpu.VMEM((RET_HEADS, blk, blk), F32),
            pltpu.VMEM((RET_HEADS, blk, RET_QK_DIM), F32),
            pltpu.VMEM((RET_HEADS, blk, RET_QK_DIM), F32),
        ],
        compiler_params=_cparams(("arbitrary", "arbitrary")),
        name="retention",
    )(log_gammas, q, k, v, g)


def kernel(x, c, ada_w, ada_b, pre_gain, post_gain, da_w_in, da_w_out, da_lambda_q1, da_lambda_k1,
           da_lambda_q2, da_lambda_k2, da_subln_gain, ret_w_in, ret_w_out):
    b, s, d = x.shape
    mod = _modulation(c, ada_w, ada_b)

    attn_tile = 512
    qk_w = 2 * DA_HEADS * DA_HEAD_DIM
    v_w = DA_HEADS * DA_V_DIM
    q_scale = LOG2E * DA_HEAD_DIM ** -0.5
    da_plan = (("headT", 0, DA_HEADS, 2 * DA_HEAD_DIM, q_scale, 0),
               ("head", qk_w, DA_HEADS, 2 * DA_HEAD_DIM, 1.0, DA_HEADS),
               ("headT", 2 * qk_w, DA_HEADS, DA_V_DIM, 1.0, None),
               ("tok", 2 * qk_w + v_w, None, v_w, 1.0, None))
    mod0 = mod[0].reshape(b, 1, 3 * d)
    qT, k, vT, g, norms = _in_projection(x, mod0, pre_gain[0:1], da_w_in[0].astype(BF16),
                                         da_plan, tm=attn_tile, slab=256)
    q_norms = jnp.transpose(norms[:, :, 0:DA_HEADS, 0], (0, 2, 1))
    k_norms = jnp.transpose(norms[:, :, DA_HEADS:2 * DA_HEADS, 0], (0, 2, 1))
    lam_init = 0.8 - 0.6 * math.exp(-0.3 * 0)
    slopes = jnp.asarray([2.0 ** (-8.0 * (hh + 1) / DA_HEADS) for hh in range(DA_HEADS)], F32)
    lamv = jnp.stack([da_lambda_q1[0], da_lambda_k1[0], da_lambda_q2[0], da_lambda_k2[0]])
    lamv = jnp.pad(lamv, ((0, ATTN_LAMBDA_ROWS - 4), (0, LANES - DA_HEAD_DIM)))
    y = _diff_attention(qT, k, vT, g, q_norms, k_norms, slopes, lamv, da_subln_gain[0], lam_init)
    x = _out_projection(y, da_w_out[0].astype(BF16), x, mod0, post_gain[0:1], tm=1024)

    rqk = RET_HEADS * RET_QK_DIM
    rv = RET_HEADS * RET_V_DIM
    ret_plan = (("tok", 0, None, rqk, 1.0, None), ("tok", rqk, None, rqk, 1.0, None),
                ("tok", 2 * rqk, None, rv, 1.0, None), ("tok", 2 * rqk + rv, None, rv, 1.0, None))
    mod1 = mod[1].reshape(b, 1, 3 * d)
    q, k, v, g = _in_projection(x, mod1, pre_gain[1:2], ret_w_in[0].astype(BF16), ret_plan,
                                tm=512, slab=256)
    gammas = 1.0 - 2.0 ** (-5.0 - jnp.arange(RET_HEADS, dtype=F32))
    y = _retention(q, k, v, g, jnp.log(gammas), blk=256, tile=512)
    return _out_projection(y, ret_w_out[0].astype(BF16), x, mod1, post_gain[1:2], tm=1024)
```

```python
import functools
import math

import jax
import jax.numpy as jnp
import numpy as np
from jax import lax
from jax.experimental import pallas as pl
from jax.experimental.pallas import tpu as pltpu

CHUNK = 64
CHUNK_SHIFT = 6
DA_HEADS = 8
DA_HEAD_DIM = 64
DA_V_DIM = 128
RET_HEADS = 4
RET_QK_DIM = 256
RET_V_DIM = 512
NORM_EPS = 1e-6
HEAD_NORM_EPS = 1e-5
MASK_VALUE = -1e30
LOG2E = 1.4426950408889634

V7X_VMEM_LIMIT_BYTES = 56 * 1024 * 1024
LANES = 128
SUBLANES = 8
BF16_EXACT_INT = 256

ATTN_TILE = 512
RET_INPROJ_ROWS = 512
OUTPROJ_ROWS = 1024
RET_BLOCK = 256
RET_TILE = 1024
INPROJ_SLAB = 256

BF16 = jnp.bfloat16
F32 = jnp.float32


def _cparams(sem, vmem=V7X_VMEM_LIMIT_BYTES, flags=None):
    return pltpu.CompilerParams(dimension_semantics=sem, vmem_limit_bytes=vmem, flags=flags)


def _mod_kernel(c_ref, w_ref, b_ref, o_ref):
    c = c_ref[...]
    cond = c * jax.nn.sigmoid(c)
    o_ref[0] = jnp.dot(cond, w_ref[0], preferred_element_type=F32) + b_ref[0]


def _modulation(c, ada_w, ada_b):
    depth, d, d3 = ada_w.shape
    b = c.shape[0]
    n_tiles = d3 // d
    return pl.pallas_call(
        _mod_kernel,
        grid=(depth, n_tiles),
        in_specs=[
            pl.BlockSpec((b, d), lambda l, j: (0, 0)),
            pl.BlockSpec((1, d, d), lambda l, j: (l, 0, j)),
            pl.BlockSpec((1, 1, d), lambda l, j: (l, 0, j)),
        ],
        out_specs=pl.BlockSpec((1, b, d), lambda l, j: (l, 0, j)),
        out_shape=jax.ShapeDtypeStruct((depth, b, d3), F32),
        compiler_params=_cparams(("arbitrary", "arbitrary")),
        name="modulation",
    )(c, ada_w, ada_b.reshape(depth, 1, d3))


def _inproj_kernel(x_ref, mod_ref, gain_ref, w_ref, *refs, d_model, plan, slab, n_norm_rows):
    out_refs = refs[:len(plan)]
    norm_ref = refs[len(plan)] if n_norm_rows else None
    hb_ref = refs[-1]
    x = x_ref[0]
    ms = jnp.mean(x * x, axis=-1, keepdims=True)
    hn = x * lax.rsqrt(ms + NORM_EPS) * gain_ref[...]
    shift = mod_ref[0, :, 0:d_model]
    scale = mod_ref[0, :, d_model:2 * d_model]
    hb_ref[...] = (hn * (1.0 + scale) + shift).astype(BF16)
    for (kind, c0, n_heads, width, out_scale, norm_row), o_ref in zip(plan, out_refs):
        total = width if kind == "tok" else n_heads * width
        for s0 in range(0, total, slab):
            r = jnp.dot(hb_ref[...], w_ref[:, c0 + s0:c0 + s0 + slab],
                        preferred_element_type=F32)
            if out_scale != 1.0:
                r = r * out_scale
            if kind == "tok":
                o_ref[0, :, s0:s0 + slab] = r.astype(BF16)
            elif kind == "head" and width >= slab:
                hh, off = divmod(s0, width)
                o_ref[0, hh, :, off:off + slab] = r.astype(BF16)
            else:
                for p in range(slab // width):
                    head = s0 // width + p
                    piece = r[:, p * width:(p + 1) * width]
                    if kind == "head":
                        o_ref[0, head] = piece.astype(BF16)
                    else:
                        o_ref[0, head, 0] = piece.T.astype(BF16)
                    if norm_row is not None:
                        norm2 = jnp.max(jnp.sum(piece * piece, axis=1, keepdims=True),
                                        axis=0, keepdims=True)
                        norm_ref[0, 0, norm_row + head:norm_row + head + 1, :] = (
                            jnp.broadcast_to(jnp.sqrt(norm2), (1, LANES)))


def _in_projection(x, mod, gain, w_bf16, plan, tm, slab):
    b, s, d = x.shape
    n = w_bf16.shape[1]
    out_shapes, out_specs = [], []
    n_norm_rows = sum(e[2] for e in plan if e[5] is not None)
    for (kind, c0, n_heads, width, _, _) in plan:
        if kind == "tok":
            out_shapes.append(jax.ShapeDtypeStruct((b, s, width), BF16))
            out_specs.append(pl.BlockSpec((1, tm, width), lambda bi, i: (bi, i, 0)))
        elif kind == "head":
            out_shapes.append(jax.ShapeDtypeStruct((b, n_heads, s, width), BF16))
            out_specs.append(pl.BlockSpec((1, n_heads, tm, width), lambda bi, i: (bi, 0, i, 0)))
        else:
            assert slab % width == 0
            out_shapes.append(jax.ShapeDtypeStruct((b, n_heads, s // tm, width, tm), BF16))
            out_specs.append(pl.BlockSpec((1, n_heads, 1, width, tm),
                                          lambda bi, i: (bi, 0, i, 0, 0)))
    if n_norm_rows:
        out_shapes.append(jax.ShapeDtypeStruct((b, s // tm, n_norm_rows, LANES), F32))
        out_specs.append(pl.BlockSpec((1, 1, n_norm_rows, LANES), lambda bi, i: (bi, i, 0, 0)))
    return pl.pallas_call(
        functools.partial(_inproj_kernel, d_model=d, plan=plan, slab=slab,
                          n_norm_rows=n_norm_rows),
        grid=(b, s // tm),
        in_specs=[
            pl.BlockSpec((1, tm, d), lambda bi, i: (bi, i, 0)),
            pl.BlockSpec((1, 1, 3 * d), lambda bi, i: (bi, 0, 0)),
            pl.BlockSpec((1, d), lambda bi, i: (0, 0)),
            pl.BlockSpec((d, n), lambda bi, i: (0, 0)),
        ],
        out_specs=out_specs,
        out_shape=out_shapes,
        scratch_shapes=[pltpu.VMEM((tm, d), BF16)],
        compiler_params=_cparams(("arbitrary", "arbitrary")),
        name="in_projection",
    )(x, mod, gain, w_bf16)


def _outproj_kernel(y_ref, w_ref, x_ref, mod_ref, gain_ref, o_ref, *, d_model):
    t = jnp.dot(y_ref[0], w_ref[...], preferred_element_type=F32)
    ms = jnp.mean(t * t, axis=-1, keepdims=True)
    n = t * lax.rsqrt(ms + NORM_EPS) * gain_ref[...]
    gate = mod_ref[0, :, 2 * d_model:3 * d_model]
    o_ref[0] = x_ref[0] + gate * n


def _out_projection(y, w_bf16, x, mod, gain, tm):
    b, s, d = x.shape
    k = y.shape[-1]
    return pl.pallas_call(
        functools.partial(_outproj_kernel, d_model=d),
        grid=(b, s // tm),
        in_specs=[
            pl.BlockSpec((1, tm, k), lambda bi, i: (bi, i, 0)),
            pl.BlockSpec((k, d), lambda bi, i: (0, 0)),
            pl.BlockSpec((1, tm, d), lambda bi, i: (bi, i, 0)),
            pl.BlockSpec((1, 1, 3 * d), lambda bi, i: (bi, 0, 0)),
            pl.BlockSpec((1, d), lambda bi, i: (0, 0)),
        ],
        out_specs=pl.BlockSpec((1, tm, d), lambda bi, i: (bi, i, 0)),
        out_shape=jax.ShapeDtypeStruct((b, s, d), F32),
        compiler_params=_cparams(("arbitrary", "arbitrary")),
        name="out_projection",
    )(y, w_bf16, x, mod, gain)


N_ALIBI_FEATURES = 6
ATTN_ROW_CHUNK = 16
ATTN_SUM_ROWS = 16
ATTN_SCORE_SLOTS = 4
ATTN_PROB_SLOTS = 2
ATTN_DIAG_STEPS = 8
ATTN_PAST_STEPS = 8
ATTN_PIPE_DEPTH = 3
ATTN_LAMBDA_ROWS = 32
ATTN_STAT_ROWS = 4
ATTN_ZERO_EXP2 = 150.0
ATTN_BOUND_SLACK = 1.0
ATTN_NORM_SLACK = 1.01


def _alibi_key_features(s):
    pos = jnp.arange(s, dtype=jnp.int32)[:, None]
    col = jnp.arange(LANES, dtype=jnp.int32)[None, :]
    lo = (pos % BF16_EXACT_INT).astype(F32)
    hi = (pos // BF16_EXACT_INT).astype(F32)
    feat = jnp.where(col < 3, lo, jnp.where(col < N_ALIBI_FEATURES, hi, 0.0))
    return feat.astype(BF16)


def _attn_schedule(nq):
    pairs = [(qi, qi) for qi in range(nq)] + [(qi, j) for qi in range(nq) for j in range(qi)]
    qb, kb = zip(*pairs)
    return np.asarray(qb, np.int32), np.asarray(kb, np.int32)


def _attn_kernel(slope_ref, qb_ref, kb_ref, qn_ref, kn_ref, qT_ref, k_ref, kf_ref, vT_ref, g_ref,
                 lamv_ref, subln_ref, y_ref, qs_ref, feat_ref, *scratch, tq, nq, lam_init):
    h = pl.program_id(1)
    tk = tq
    w = 2 * tq
    rc = ATTN_ROW_CHUNK
    n_chunks = tk // rc
    n_pos = nq * (nq + 1) // 2
    scratch = list(scratch)
    s_refs = [scratch.pop(0) for _ in range(ATTN_SCORE_SLOTS)]
    mc_refs = [scratch.pop(0) for _ in range(ATTN_SCORE_SLOTS)]
    p_refs = [scratch.pop(0) for _ in range(ATTN_PROB_SLOTS)]
    al_refs = [scratch.pop(0) for _ in range(ATTN_PROB_SLOTS)]
    m_ref, acc_ref, corr_ref, thr_ref, ql_ref, kl_ref = scratch
    n_past = n_pos - nq
    assert nq % ATTN_DIAG_STEPS == 0 and nq >= ATTN_DIAG_STEPS
    assert ATTN_DIAG_STEPS % ATTN_SCORE_SLOTS == 0 and ATTN_PAST_STEPS % ATTN_SCORE_SLOTS == 0
    assert n_past % ATTN_PAST_STEPS == 0

    slope2 = slope_ref[h] * LOG2E
    fr = lax.broadcasted_iota(jnp.int32, (LANES, w), 0)
    x = jnp.full((LANES, w), slope2, F32)
    hi = x.astype(BF16).astype(F32)
    mid = (x - hi).astype(BF16).astype(F32)
    lo = x - hi - mid
    part = jnp.where((fr == 0) | (fr == 3), hi, jnp.where((fr == 1) | (fr == 4), mid, lo))
    feat = jnp.where(fr < 3, part, jnp.where(fr < N_ALIBI_FEATURES, part * BF16_EXACT_INT, 0.0))
    feat_ref[...] = feat.astype(BF16)
    kk = lax.broadcasted_iota(jnp.int32, (tk, w), 0)
    qq = lax.broadcasted_iota(jnp.int32, (tk, w), 1)
    qq = jnp.where(qq >= tq, qq - tq, qq)
    ahead = (kk - qq).astype(F32)
    same_chunk = (kk >> CHUNK_SHIFT) == (qq >> CHUNK_SHIFT)
    corr_ref[...] = jnp.where(kk <= qq, 0.0,
                              jnp.where(same_chunk, -2.0 * slope2 * ahead, MASK_VALUE))

    def init_query_block(qi, carry):
        qT = qT_ref[0, 0, qi]
        row = lax.broadcasted_iota(jnp.int32, qT.shape, 0)
        zero = jnp.zeros_like(qT)
        qs_ref[qi, :, 0:tq] = jnp.where(row < DA_HEAD_DIM, qT, zero)
        qs_ref[qi, :, tq:w] = jnp.where(row >= DA_HEAD_DIM, qT, zero)
        return carry

    lax.fori_loop(0, nq, init_query_block, 0)
    ones_rows = jnp.ones((ATTN_SUM_ROWS, tk), BF16)

    def qk(pair, t, s_slot, diag):
        qi, kj = pair(t)
        start = kj * tk if isinstance(kj, int) else pl.multiple_of(kj * tk, tk)
        ka = jnp.concatenate([k_ref[0, 0, pl.ds(start, tk), :], kf_ref[pl.ds(start, tk), :]],
                             axis=1)
        qa = jnp.concatenate([qs_ref[qi], feat_ref[...]], axis=0)
        s = jnp.dot(ka, qa, preferred_element_type=F32)
        if diag:
            s = s + corr_ref[...]
        s_refs[s_slot][...] = s
        mc_refs[s_slot][0:1, :] = jnp.max(s, axis=0, keepdims=True)

    def sm(pair, t, s_slot, p_slot, first):
        qi, _ = pair(t)
        if first:
            m_next = mc_refs[s_slot][0:1, :]
        else:
            m_prev = m_ref[qi]
            m_next = jnp.maximum(m_prev, mc_refs[s_slot][0:1, :])
            al_refs[p_slot][0:1, :] = jnp.exp2(m_prev - m_next)
        m_ref[qi] = m_next
        m_rows = jnp.broadcast_to(m_next, (rc, w))
        for c in range(n_chunks):
            p = jnp.exp2(s_refs[s_slot][c * rc:(c + 1) * rc, :] - m_rows)
            p_refs[p_slot][c * rc:(c + 1) * rc, :] = p.astype(BF16)

    def pv(pair, t, p_slot, first):
        qi, kj = pair(t)
        v_aug = jnp.concatenate([vT_ref[0, 0, kj], ones_rows], axis=0)
        upd = jnp.dot(v_aug, p_refs[p_slot][...], preferred_element_type=F32)
        if first:
            acc_ref[qi] = upd
        else:
            acc_ref[qi] = al_refs[p_slot][0:1, :] * acc_ref[qi] + upd

    def run_pipeline(pair, n, steps, diag):
        depth = ATTN_PIPE_DEPTH

        def time_step(tau, u, do_qk=True, do_sm=True, do_pv=True):
            if do_qk:
                qk(pair, tau, u, diag)
            if do_sm:
                sm(pair, tau - (depth - 1), (u - (depth - 1)) % ATTN_SCORE_SLOTS,
                   (u - (depth - 1)) % ATTN_PROB_SLOTS, diag)
            if do_pv:
                pv(pair, tau - depth, (u - depth) % ATTN_PROB_SLOTS, diag)

        for tau in range(depth):
            time_step(tau, tau % ATTN_SCORE_SLOTS, do_sm=tau >= depth - 1, do_pv=False)

        def body(i, carry):
            for u in range(steps):
                time_step(depth + steps * i + u, (depth + u) % ATTN_SCORE_SLOTS)
            return carry

        lax.fori_loop(0, n // steps - 1, body, 0)
        for u in range(steps - depth):
            time_step(n - (steps - depth) + u, (depth + u) % ATTN_SCORE_SLOTS)
        for d in range(depth):
            time_step(n + d, d % ATTN_SCORE_SLOTS, do_qk=False, do_sm=d < depth - 1)

    run_pipeline(lambda t: (t, t), nq, ATTN_DIAG_STEPS, diag=True)

    norm_base = (pl.program_id(0) * pl.num_programs(1) + h) * nq
    m_min = jnp.min(m_ref[...].reshape(nq, w), axis=1, keepdims=True)
    for qi in range(nq):
        thr_ref[qi] = m_min[qi, 0] - (ATTN_ZERO_EXP2 + ATTN_BOUND_SLACK)

    def choose(i, carry):
        count, pad_q, pad_k = carry
        qi, kj = qb_ref[nq + i], kb_ref[nq + i]
        last_key = ((kj + 1) * tk - 1).astype(F32)
        bound = (qn_ref[norm_base + qi] * kn_ref[norm_base + kj] * ATTN_NORM_SLACK
                 + slope2 * last_key)
        keep = bound >= thr_ref[qi]
        ql_ref[count] = qi
        kl_ref[count] = kj
        return (count + keep.astype(jnp.int32), jnp.where(keep, pad_q, qi),
                jnp.where(keep, pad_k, kj))

    n_keep, pad_q, pad_k = lax.fori_loop(0, n_past, choose,
                                         (jnp.int32(0), jnp.int32(0), jnp.int32(0)))
    n_pad = (-n_keep) & (ATTN_PAST_STEPS - 1)
    for r in range(ATTN_PAST_STEPS - 1):
        @pl.when(r < n_pad)
        def _():
            ql_ref[n_keep + r] = pad_q
            kl_ref[n_keep + r] = pad_k

    n_visit = n_keep + n_pad

    @pl.when(n_visit > 0)
    def _():
        run_pipeline(lambda t: (ql_ref[t], kl_ref[t]), n_visit, ATTN_PAST_STEPS, diag=False)

    lv = lamv_ref[...]
    lam = (jnp.exp(jnp.sum(lv[0:1] * lv[1:2], axis=-1, keepdims=True))
           - jnp.exp(jnp.sum(lv[2:3] * lv[3:4], axis=-1, keepdims=True)) + lam_init)

    def finish_query_block(qi, carry):
        o = acc_ref[qi, 0:DA_V_DIM, :] * (1.0 / acc_ref[qi, DA_V_DIM:DA_V_DIM + 1, :])
        out = o[:, 0:tq] - lam * o[:, tq:w]
        ms = jnp.mean(out * out, axis=0, keepdims=True)
        out = out * lax.rsqrt(ms + HEAD_NORM_EPS) * subln_ref[...] * (1.0 - lam_init)
        rows = pl.ds(pl.multiple_of(qi * tq, tq), tq)
        g = g_ref[0, rows, :].astype(F32)
        y_ref[0, rows, :] = (g * jax.nn.sigmoid(g) * out.T).astype(BF16)
        return carry

    def finish_two(i, carry):
        finish_query_block(2 * i, carry)
        return finish_query_block(2 * i + 1, carry)

    lax.fori_loop(0, nq // 2, finish_two, 0)


def _diff_attention(qT, k, vT, g, q_norms, k_norms, slopes, lamv, subln, lam_init):
    b, nh, nq, dk, tq = qT.shape
    s = k.shape[2]
    w = 2 * tq
    qb, kb = _attn_schedule(nq)
    kernel = functools.partial(_attn_kernel, tq=tq, nq=nq, lam_init=lam_init)
    smem = pl.BlockSpec(memory_space=pltpu.SMEM)
    once = pl.Buffered(1)
    return pl.pallas_call(
        kernel,
        grid=(b, nh),
        in_specs=[
            smem, smem, smem, smem, smem,
            pl.BlockSpec((1, 1, nq, dk, tq), lambda bi, h: (bi, h, 0, 0, 0), pipeline_mode=once),
            pl.BlockSpec((1, 1, s, dk), lambda bi, h: (bi, h, 0, 0)),
            pl.BlockSpec((s, LANES), lambda bi, h: (0, 0), pipeline_mode=once),
            pl.BlockSpec((1, 1, nq, DA_V_DIM, tq), lambda bi, h: (bi, h, 0, 0, 0)),
            pl.BlockSpec((1, s, DA_V_DIM), lambda bi, h: (bi, 0, h)),
            pl.BlockSpec((ATTN_LAMBDA_ROWS, LANES), lambda bi, h: (0, 0)),
            pl.BlockSpec((DA_V_DIM, 1), lambda bi, h: (0, 0)),
        ],
        out_specs=pl.BlockSpec((1, s, DA_V_DIM), lambda bi, h: (bi, 0, h)),
        out_shape=jax.ShapeDtypeStruct((b, s, nh * DA_V_DIM), BF16),
        scratch_shapes=(
            [pltpu.VMEM((nq, LANES, w), BF16),
             pltpu.VMEM((LANES, w), BF16)]
            + [pltpu.VMEM((tq, w), F32)] * ATTN_SCORE_SLOTS
            + [pltpu.VMEM((ATTN_STAT_ROWS, w), F32)] * ATTN_SCORE_SLOTS
            + [pltpu.VMEM((tq, w), BF16)] * ATTN_PROB_SLOTS
            + [pltpu.VMEM((ATTN_STAT_ROWS, w), F32)] * ATTN_PROB_SLOTS
            + [pltpu.VMEM((nq, 1, w), F32),
               pltpu.VMEM((nq, DA_V_DIM + ATTN_SUM_ROWS, w), F32),
               pltpu.VMEM((tq, w), F32),
               pltpu.SMEM((nq,), F32),
               pltpu.SMEM((len(qb),), jnp.int32),
               pltpu.SMEM((len(qb),), jnp.int32)]
        ),
        compiler_params=_cparams(("arbitrary", "arbitrary")),
        name="diff_attention",
    )(slopes, jnp.asarray(qb), jnp.asarray(kb), q_norms.reshape(-1), k_norms.reshape(-1), qT, k,
      _alibi_key_features(s), vT, g, lamv, subln.reshape(DA_V_DIM, 1))


def _ret_kernel(lg_ref, q_ref, k_ref, v_ref, g_ref, y_ref, state_ref, decay_ref, qdec_ref,
                kdec_ref, *, blk):
    n = pl.program_id(1)
    dk, dv = RET_QK_DIM, RET_V_DIM

    @pl.when(n == 0)
    def _():
        state_ref[...] = jnp.zeros(state_ref.shape, F32)
        ti = lax.broadcasted_iota(jnp.int32, (blk, blk), 0)
        ui = lax.broadcasted_iota(jnp.int32, (blk, blk), 1)
        dist = jnp.abs(ti - ui).astype(F32)
        allowed = (ui >> CHUNK_SHIFT) <= (ti >> CHUNK_SHIFT)
        t = lax.broadcasted_iota(jnp.int32, (blk, dk), 0).astype(F32)
        for h in range(RET_HEADS):
            decay_ref[h] = jnp.where(allowed, jnp.exp(lg_ref[h] * dist), 0.0) * dk ** -0.5
            qdec_ref[h] = jnp.exp(lg_ref[h] * t)
            kdec_ref[h] = jnp.exp(lg_ref[h] * (blk - t)) * dk ** -0.5

    def block(rows):
        for h in range(RET_HEADS):
            lg = lg_ref[h]
            q = q_ref[0, rows, h * dk:(h + 1) * dk]
            k = k_ref[0, rows, h * dk:(h + 1) * dk]
            v = v_ref[0, rows, h * dv:(h + 1) * dv]
            qd = (q.astype(F32) * qdec_ref[h]).astype(BF16)
            kd = (k.astype(F32) * kdec_ref[h]).astype(BF16)

            a = lax.dot_general(q, k, (((1,), (1,)), ((), ())), preferred_element_type=F32)
            a = (a * decay_ref[h]).astype(BF16)
            out = jnp.dot(a, v, preferred_element_type=F32)
            out = out + jnp.dot(qd, state_ref[h].astype(BF16), preferred_element_type=F32)
            kv = lax.dot_general(kd, v, (((0,), (0,)), ((), ())), preferred_element_type=F32)
            block_decay = jnp.exp(lg * jnp.full((1, 1), float(blk), F32))
            state_ref[h] = block_decay * state_ref[h] + kv

            ss = jnp.sum(out * out, axis=-1, keepdims=True)
            out = out * lax.rsqrt(ss * (1.0 / dv) + HEAD_NORM_EPS)
            g = g_ref[0, rows, h * dv:(h + 1) * dv].astype(F32)
            y_ref[0, rows, h * dv:(h + 1) * dv] = (g * jax.nn.sigmoid(g) * out).astype(BF16)

    for sub in range(q_ref.shape[1] // blk):
        block(slice(sub * blk, (sub + 1) * blk))


def _retention(q, k, v, g, log_gammas, blk, tile):
    b, s, qk_w = q.shape
    v_w = v.shape[-1]
    assert tile % blk == 0 and s % tile == 0
    return pl.pallas_call(
        functools.partial(_ret_kernel, blk=blk),
        grid=(b, s // tile),
        in_specs=[
            pl.BlockSpec(memory_space=pltpu.SMEM),
            pl.BlockSpec((1, tile, qk_w), lambda bi, i: (bi, i, 0)),
            pl.BlockSpec((1, tile, qk_w), lambda bi, i: (bi, i, 0)),
            pl.BlockSpec((1, tile, v_w), lambda bi, i: (bi, i, 0)),
            pl.BlockSpec((1, tile, v_w), lambda bi, i: (bi, i, 0)),
        ],
        out_specs=pl.BlockSpec((1, tile, v_w), lambda bi, i: (bi, i, 0)),
        out_shape=jax.ShapeDtypeStruct((b, s, v_w), BF16),
        scratch_shapes=[
            pltpu.VMEM((RET_HEADS, RET_QK_DIM, RET_V_DIM), F32),
            pltpu.VMEM((RET_HEADS, blk, blk), F32),
            pltpu.VMEM((RET_HEADS, blk, RET_QK_DIM), F32),
            pltpu.VMEM((RET_HEADS, blk, RET_QK_DIM), F32),
        ],
        compiler_params=_cparams(("arbitrary", "arbitrary")),
        name="retention",
    )(log_gammas, q, k, v, g)


def kernel(x, c, ada_w, ada_b, pre_gain, post_gain, da_w_in, da_w_out, da_lambda_q1, da_lambda_k1,
           da_lambda_q2, da_lambda_k2, da_subln_gain, ret_w_in, ret_w_out):
    b, s, d = x.shape
    mod = _modulation(c, ada_w, ada_b)

    qk_w = 2 * DA_HEADS * DA_HEAD_DIM
    v_w = DA_HEADS * DA_V_DIM
    q_scale = LOG2E * DA_HEAD_DIM ** -0.5
    da_plan = (("headT", 0, DA_HEADS, 2 * DA_HEAD_DIM, q_scale, 0),
               ("head", qk_w, DA_HEADS, 2 * DA_HEAD_DIM, 1.0, DA_HEADS),
               ("headT", 2 * qk_w, DA_HEADS, DA_V_DIM, 1.0, None),
               ("tok", 2 * qk_w + v_w, None, v_w, 1.0, None))
    mod0 = mod[0].reshape(b, 1, 3 * d)
    qT, k, vT, g, norms = _in_projection(x, mod0, pre_gain[0:1], da_w_in[0].astype(BF16),
                                         da_plan, tm=ATTN_TILE, slab=INPROJ_SLAB)
    q_norms = jnp.transpose(norms[:, :, 0:DA_HEADS, 0], (0, 2, 1))
    k_norms = jnp.transpose(norms[:, :, DA_HEADS:2 * DA_HEADS, 0], (0, 2, 1))
    lam_init = 0.8 - 0.6 * math.exp(-0.3 * 0)
    slopes = jnp.asarray([2.0 ** (-8.0 * (hh + 1) / DA_HEADS) for hh in range(DA_HEADS)], F32)
    lamv = jnp.stack([da_lambda_q1[0], da_lambda_k1[0], da_lambda_q2[0], da_lambda_k2[0]])
    lamv = jnp.pad(lamv, ((0, ATTN_LAMBDA_ROWS - 4), (0, LANES - DA_HEAD_DIM)))
    y = _diff_attention(qT, k, vT, g, q_norms, k_norms, slopes, lamv, da_subln_gain[0], lam_init)
    x = _out_projection(y, da_w_out[0].astype(BF16), x, mod0, post_gain[0:1], tm=OUTPROJ_ROWS)

    rqk = RET_HEADS * RET_QK_DIM
    rv = RET_HEADS * RET_V_DIM
    ret_plan = (("tok", 0, None, rqk, 1.0, None), ("tok", rqk, None, rqk, 1.0, None),
                ("tok", 2 * rqk, None, rv, 1.0, None), ("tok", 2 * rqk + rv, None, rv, 1.0, None))
    mod1 = mod[1].reshape(b, 1, 3 * d)
    q, k, v, g = _in_projection(x, mod1, pre_gain[1:2], ret_w_in[0].astype(BF16), ret_plan,
                                tm=RET_INPROJ_ROWS, slab=INPROJ_SLAB)
    gammas = 1.0 - 2.0 ** (-5.0 - jnp.arange(RET_HEADS, dtype=F32))
    y = _retention(q, k, v, g, jnp.log(gammas), blk=RET_BLOCK, tile=RET_TILE)
    return _out_projection(y, ret_w_out[0].astype(BF16), x, mod1, post_gain[1:2], tm=OUTPROJ_ROWS)
```

```python
import functools
import math

import jax
import jax.numpy as jnp
import numpy as np
from jax import lax
from jax.experimental import pallas as pl
from jax.experimental.pallas import tpu as pltpu

CHUNK = 64
CHUNK_SHIFT = 6
DA_HEADS = 8
DA_HEAD_DIM = 64
DA_V_DIM = 128
RET_HEADS = 4
RET_QK_DIM = 256
RET_V_DIM = 512
NORM_EPS = 1e-6
HEAD_NORM_EPS = 1e-5
MASK_VALUE = -1e30
LOG2E = 1.4426950408889634

V7X_VMEM_LIMIT_BYTES = 56 * 1024 * 1024
LANES = 128
SUBLANES = 8
BF16_EXACT_INT = 256

ATTN_TILE = 512
RET_INPROJ_ROWS = 1024
OUTPROJ_ROWS = 1024
RET_BLOCK = 256
RET_TILE = 1024
INPROJ_SLAB = 256

BF16 = jnp.bfloat16
F32 = jnp.float32


def _cparams(sem, vmem=V7X_VMEM_LIMIT_BYTES, flags=None):
    return pltpu.CompilerParams(dimension_semantics=sem, vmem_limit_bytes=vmem, flags=flags)


def _mod_kernel(c_ref, w_ref, b_ref, o_ref):
    c = c_ref[...]
    cond = c * jax.nn.sigmoid(c)
    o_ref[0] = jnp.dot(cond, w_ref[0], preferred_element_type=F32) + b_ref[0]


def _modulation(c, ada_w, ada_b):
    depth, d, d3 = ada_w.shape
    b = c.shape[0]
    n_tiles = d3 // d
    return pl.pallas_call(
        _mod_kernel,
        grid=(depth, n_tiles),
        in_specs=[
            pl.BlockSpec((b, d), lambda l, j: (0, 0)),
            pl.BlockSpec((1, d, d), lambda l, j: (l, 0, j)),
            pl.BlockSpec((1, 1, d), lambda l, j: (l, 0, j)),
        ],
        out_specs=pl.BlockSpec((1, b, d), lambda l, j: (l, 0, j)),
        out_shape=jax.ShapeDtypeStruct((depth, b, d3), F32),
        compiler_params=_cparams(("arbitrary", "arbitrary")),
        name="modulation",
    )(c, ada_w, ada_b.reshape(depth, 1, d3))


def _inproj_kernel(x_ref, mod_ref, gain_ref, w_ref, *refs, d_model, plan, slab, n_norm_rows):
    out_refs = refs[:len(plan)]
    norm_ref = refs[len(plan)] if n_norm_rows else None
    hb_ref = refs[-1]
    x = x_ref[0]
    ms = jnp.mean(x * x, axis=-1, keepdims=True)
    hn = x * lax.rsqrt(ms + NORM_EPS) * gain_ref[...]
    shift = mod_ref[0, :, 0:d_model]
    scale = mod_ref[0, :, d_model:2 * d_model]
    hb_ref[...] = (hn * (1.0 + scale) + shift).astype(BF16)
    for (kind, c0, n_heads, width, out_scale, norm_row), o_ref in zip(plan, out_refs):
        total = width if kind == "tok" else n_heads * width
        for s0 in range(0, total, slab):
            r = jnp.dot(hb_ref[...], w_ref[:, c0 + s0:c0 + s0 + slab],
                        preferred_element_type=F32)
            if out_scale != 1.0:
                r = r * out_scale
            if kind == "tok":
                o_ref[0, :, s0:s0 + slab] = r.astype(BF16)
            elif kind == "head" and width >= slab:
                hh, off = divmod(s0, width)
                o_ref[0, hh, :, off:off + slab] = r.astype(BF16)
            else:
                for p in range(slab // width):
                    head = s0 // width + p
                    piece = r[:, p * width:(p + 1) * width]
                    if kind == "head":
                        o_ref[0, head] = piece.astype(BF16)
                    else:
                        o_ref[0, head, 0] = piece.T.astype(BF16)
                    if norm_row is not None:
                        norm2 = jnp.max(jnp.sum(piece * piece, axis=1, keepdims=True),
                                        axis=0, keepdims=True)
                        norm_ref[0, 0, norm_row + head:norm_row + head + 1, :] = (
                            jnp.broadcast_to(jnp.sqrt(norm2), (1, LANES)))


def _in_projection(x, mod, gain, w_bf16, plan, tm, slab):
    b, s, d = x.shape
    n = w_bf16.shape[1]
    out_shapes, out_specs = [], []
    n_norm_rows = sum(e[2] for e in plan if e[5] is not None)
    for (kind, c0, n_heads, width, _, _) in plan:
        if kind == "tok":
            out_shapes.append(jax.ShapeDtypeStruct((b, s, width), BF16))
            out_specs.append(pl.BlockSpec((1, tm, width), lambda bi, i: (bi, i, 0)))
        elif kind == "head":
            out_shapes.append(jax.ShapeDtypeStruct((b, n_heads, s, width), BF16))
            out_specs.append(pl.BlockSpec((1, n_heads, tm, width), lambda bi, i: (bi, 0, i, 0)))
        else:
            assert slab % width == 0
            out_shapes.append(jax.ShapeDtypeStruct((b, n_heads, s // tm, width, tm), BF16))
            out_specs.append(pl.BlockSpec((1, n_heads, 1, width, tm),
                                          lambda bi, i: (bi, 0, i, 0, 0)))
    if n_norm_rows:
        out_shapes.append(jax.ShapeDtypeStruct((b, s // tm, n_norm_rows, LANES), F32))
        out_specs.append(pl.BlockSpec((1, 1, n_norm_rows, LANES), lambda bi, i: (bi, i, 0, 0)))
    return pl.pallas_call(
        functools.partial(_inproj_kernel, d_model=d, plan=plan, slab=slab,
                          n_norm_rows=n_norm_rows),
        grid=(b, s // tm),
        in_specs=[
            pl.BlockSpec((1, tm, d), lambda bi, i: (bi, i, 0)),
            pl.BlockSpec((1, 1, 3 * d), lambda bi, i: (bi, 0, 0)),
            pl.BlockSpec((1, d), lambda bi, i: (0, 0)),
            pl.BlockSpec((d, n), lambda bi, i: (0, 0), pipeline_mode=pl.Buffered(1)),
        ],
        out_specs=out_specs,
        out_shape=out_shapes,
        scratch_shapes=[pltpu.VMEM((tm, d), BF16)],
        compiler_params=_cparams(("arbitrary", "arbitrary")),
        name="in_projection",
    )(x, mod, gain, w_bf16)


def _outproj_kernel(y_ref, w_ref, x_ref, mod_ref, gain_ref, o_ref, *, d_model):
    if len(y_ref.shape) == 4:
        y = jnp.concatenate([y_ref[0, h] for h in range(y_ref.shape[1])], axis=1)
    else:
        y = y_ref[0]
    t = jnp.dot(y, w_ref[...], preferred_element_type=F32)
    ms = jnp.mean(t * t, axis=-1, keepdims=True)
    n = t * lax.rsqrt(ms + NORM_EPS) * gain_ref[...]
    gate = mod_ref[0, :, 2 * d_model:3 * d_model]
    o_ref[0] = x_ref[0] + gate * n


def _out_projection(y, w_bf16, x, mod, gain, tm):
    b, s, d = x.shape
    k = w_bf16.shape[0]
    if y.ndim == 4:
        y_spec = pl.BlockSpec((1, y.shape[1], tm, y.shape[3]), lambda bi, i: (bi, 0, i, 0))
    else:
        y_spec = pl.BlockSpec((1, tm, k), lambda bi, i: (bi, i, 0))
    return pl.pallas_call(
        functools.partial(_outproj_kernel, d_model=d),
        grid=(b, s // tm),
        in_specs=[
            y_spec,
            pl.BlockSpec((k, d), lambda bi, i: (0, 0)),
            pl.BlockSpec((1, tm, d), lambda bi, i: (bi, i, 0)),
            pl.BlockSpec((1, 1, 3 * d), lambda bi, i: (bi, 0, 0)),
            pl.BlockSpec((1, d), lambda bi, i: (0, 0)),
        ],
        out_specs=pl.BlockSpec((1, tm, d), lambda bi, i: (bi, i, 0)),
        out_shape=jax.ShapeDtypeStruct((b, s, d), F32),
        compiler_params=_cparams(("arbitrary", "arbitrary")),
        name="out_projection",
    )(y, w_bf16, x, mod, gain)


N_ALIBI_FEATURES = 6
ATTN_ROW_CHUNK = 16
ATTN_SUM_ROWS = 16
ATTN_SCORE_SLOTS = 4
ATTN_PROB_SLOTS = 2
ATTN_DIAG_STEPS = 8
ATTN_PAST_STEPS = 8
ATTN_PIPE_DEPTH = 3
ATTN_LAMBDA_ROWS = 32
ATTN_STAT_ROWS = 4
ATTN_ZERO_EXP2 = 150.0
ATTN_BOUND_SLACK = 1.0
ATTN_NORM_SLACK = 1.01


def _alibi_key_features(s):
    pos = jnp.arange(s, dtype=jnp.int32)[:, None]
    col = jnp.arange(LANES, dtype=jnp.int32)[None, :]
    lo = (pos % BF16_EXACT_INT).astype(F32)
    hi = (pos // BF16_EXACT_INT).astype(F32)
    feat = jnp.where(col < 3, lo, jnp.where(col < N_ALIBI_FEATURES, hi, 0.0))
    return feat.astype(BF16)


def _attn_schedule(nq):
    pairs = [(qi, qi) for qi in range(nq)] + [(qi, j) for qi in range(nq) for j in range(qi)]
    qb, kb = zip(*pairs)
    return np.asarray(qb, np.int32), np.asarray(kb, np.int32)


def _attn_kernel(slope_ref, qb_ref, kb_ref, qn_ref, kn_ref, qT_ref, k_ref, kf_ref, vT_ref, g_ref,
                 lamv_ref, subln_ref, y_ref, qs_ref, feat_ref, *scratch, tq, nq, lam_init):
    h = pl.program_id(1)
    tk = tq
    w = 2 * tq
    rc = ATTN_ROW_CHUNK
    n_chunks = tk // rc
    n_pos = nq * (nq + 1) // 2
    scratch = list(scratch)
    s_refs = [scratch.pop(0) for _ in range(ATTN_SCORE_SLOTS)]
    mc_refs = [scratch.pop(0) for _ in range(ATTN_SCORE_SLOTS)]
    p_refs = [scratch.pop(0) for _ in range(ATTN_PROB_SLOTS)]
    al_refs = [scratch.pop(0) for _ in range(ATTN_PROB_SLOTS)]
    m_ref, acc_ref, corr_ref, thr_ref, ql_ref, kl_ref = scratch
    n_past = n_pos - nq
    assert nq % ATTN_DIAG_STEPS == 0 and nq >= ATTN_DIAG_STEPS
    assert ATTN_DIAG_STEPS % ATTN_SCORE_SLOTS == 0 and ATTN_PAST_STEPS % ATTN_SCORE_SLOTS == 0
    assert n_past % ATTN_PAST_STEPS == 0

    slope2 = slope_ref[h] * LOG2E
    fr = lax.broadcasted_iota(jnp.int32, (LANES, w), 0)
    x = jnp.full((LANES, w), slope2, F32)
    hi = x.astype(BF16).astype(F32)
    mid = (x - hi).astype(BF16).astype(F32)
    lo = x - hi - mid
    part = jnp.where((fr == 0) | (fr == 3), hi, jnp.where((fr == 1) | (fr == 4), mid, lo))
    feat = jnp.where(fr < 3, part, jnp.where(fr < N_ALIBI_FEATURES, part * BF16_EXACT_INT, 0.0))
    feat_ref[...] = feat.astype(BF16)
    kk = lax.broadcasted_iota(jnp.int32, (tk, w), 0)
    qq = lax.broadcasted_iota(jnp.int32, (tk, w), 1)
    qq = jnp.where(qq >= tq, qq - tq, qq)
    ahead = (kk - qq).astype(F32)
    same_chunk = (kk >> CHUNK_SHIFT) == (qq >> CHUNK_SHIFT)
    corr_ref[...] = jnp.where(kk <= qq, 0.0,
                              jnp.where(same_chunk, -2.0 * slope2 * ahead, MASK_VALUE))

    def init_query_block(qi, carry):
        qT = qT_ref[0, 0, qi]
        row = lax.broadcasted_iota(jnp.int32, qT.shape, 0)
        zero = jnp.zeros_like(qT)
        qs_ref[qi, :, 0:tq] = jnp.where(row < DA_HEAD_DIM, qT, zero)
        qs_ref[qi, :, tq:w] = jnp.where(row >= DA_HEAD_DIM, qT, zero)
        return carry

    lax.fori_loop(0, nq, init_query_block, 0)
    ones_rows = jnp.ones((ATTN_SUM_ROWS, tk), BF16)

    def qk(pair, t, s_slot, diag):
        qi, kj = pair(t)
        start = kj * tk if isinstance(kj, int) else pl.multiple_of(kj * tk, tk)
        ka = jnp.concatenate([k_ref[0, 0, pl.ds(start, tk), :], kf_ref[pl.ds(start, tk), :]],
                             axis=1)
        qa = jnp.concatenate([qs_ref[qi], feat_ref[...]], axis=0)
        s = jnp.dot(ka, qa, preferred_element_type=F32)
        if diag:
            s = s + corr_ref[...]
        s_refs[s_slot][...] = s
        mc_refs[s_slot][0:1, :] = jnp.max(s, axis=0, keepdims=True)

    def sm(pair, t, s_slot, p_slot, first):
        qi, _ = pair(t)
        if first:
            m_next = mc_refs[s_slot][0:1, :]
        else:
            m_prev = m_ref[qi]
            m_next = jnp.maximum(m_prev, mc_refs[s_slot][0:1, :])
            al_refs[p_slot][0:1, :] = jnp.exp2(m_prev - m_next)
        m_ref[qi] = m_next
        m_rows = jnp.broadcast_to(m_next, (rc, w))
        for c in range(n_chunks):
            p = jnp.exp2(s_refs[s_slot][c * rc:(c + 1) * rc, :] - m_rows)
            p_refs[p_slot][c * rc:(c + 1) * rc, :] = p.astype(BF16)

    def pv(pair, t, p_slot, first):
        qi, kj = pair(t)
        v_aug = jnp.concatenate([vT_ref[0, 0, kj], ones_rows], axis=0)
        upd = jnp.dot(v_aug, p_refs[p_slot][...], preferred_element_type=F32)
        if first:
            acc_ref[qi] = upd
        else:
            acc_ref[qi] = al_refs[p_slot][0:1, :] * acc_ref[qi] + upd

    def run_pipeline(pair, n, steps, diag):
        depth = ATTN_PIPE_DEPTH

        def time_step(tau, u, do_qk=True, do_sm=True, do_pv=True):
            if do_qk:
                qk(pair, tau, u, diag)
            if do_sm:
                sm(pair, tau - (depth - 1), (u - (depth - 1)) % ATTN_SCORE_SLOTS,
                   (u - (depth - 1)) % ATTN_PROB_SLOTS, diag)
            if do_pv:
                pv(pair, tau - depth, (u - depth) % ATTN_PROB_SLOTS, diag)

        for tau in range(depth):
            time_step(tau, tau % ATTN_SCORE_SLOTS, do_sm=tau >= depth - 1, do_pv=False)

        def body(i, carry):
            for u in range(steps):
                time_step(depth + steps * i + u, (depth + u) % ATTN_SCORE_SLOTS)
            return carry

        lax.fori_loop(0, n // steps - 1, body, 0)
        for u in range(steps - depth):
            time_step(n - (steps - depth) + u, (depth + u) % ATTN_SCORE_SLOTS)
        for d in range(depth):
            time_step(n + d, d % ATTN_SCORE_SLOTS, do_qk=False, do_sm=d < depth - 1)

    run_pipeline(lambda t: (t, t), nq, ATTN_DIAG_STEPS, diag=True)

    norm_base = (pl.program_id(0) * pl.num_programs(1) + h) * nq
    m_min = jnp.min(m_ref[...].reshape(nq, w), axis=1, keepdims=True)
    for qi in range(nq):
        thr_ref[qi] = m_min[qi, 0] - (ATTN_ZERO_EXP2 + ATTN_BOUND_SLACK)

    def choose(i, carry):
        count, pad_q, pad_k = carry
        qi, kj = qb_ref[nq + i], kb_ref[nq + i]
        last_key = ((kj + 1) * tk - 1).astype(F32)
        bound = (qn_ref[norm_base + qi] * kn_ref[norm_base + kj] * ATTN_NORM_SLACK
                 + slope2 * last_key)
        keep = bound >= thr_ref[qi]
        ql_ref[count] = qi
        kl_ref[count] = kj
        return (count + keep.astype(jnp.int32), jnp.where(keep, pad_q, qi),
                jnp.where(keep, pad_k, kj))

    n_keep, pad_q, pad_k = lax.fori_loop(0, n_past, choose,
                                         (jnp.int32(0), jnp.int32(0), jnp.int32(0)))
    n_pad = (-n_keep) & (ATTN_PAST_STEPS - 1)
    for r in range(ATTN_PAST_STEPS - 1):
        @pl.when(r < n_pad)
        def _():
            ql_ref[n_keep + r] = pad_q
            kl_ref[n_keep + r] = pad_k

    n_visit = n_keep + n_pad

    @pl.when(n_visit > 0)
    def _():
        run_pipeline(lambda t: (ql_ref[t], kl_ref[t]), n_visit, ATTN_PAST_STEPS, diag=False)

    lv = lamv_ref[...]
    lam = (jnp.exp(jnp.sum(lv[0:1] * lv[1:2], axis=-1, keepdims=True))
           - jnp.exp(jnp.sum(lv[2:3] * lv[3:4], axis=-1, keepdims=True)) + lam_init)

    def finish_query_block(qi, carry):
        o = acc_ref[qi, 0:DA_V_DIM, :] * (1.0 / acc_ref[qi, DA_V_DIM:DA_V_DIM + 1, :])
        out = o[:, 0:tq] - lam * o[:, tq:w]
        ms = jnp.mean(out * out, axis=0, keepdims=True)
        out = out * lax.rsqrt(ms + HEAD_NORM_EPS) * subln_ref[...] * (1.0 - lam_init)
        rows = pl.ds(pl.multiple_of(qi * tq, tq), tq)
        g = g_ref[0, 0, rows, :].astype(F32)
        y_ref[0, 0, rows, :] =(g * jax.nn.sigmoid(g) * out.T).astype(BF16)
        return carry

    def finish_two(i, carry):
        finish_query_block(2 * i, carry)
        return finish_query_block(2 * i + 1, carry)

    lax.fori_loop(0, nq // 2, finish_two, 0)


def _diff_attention(qT, k, vT, g, q_norms, k_norms, slopes, lamv, subln, lam_init):
    b, nh, nq, dk, tq = qT.shape
    s = k.shape[2]
    w = 2 * tq
    qb, kb = _attn_schedule(nq)
    kernel = functools.partial(_attn_kernel, tq=tq, nq=nq, lam_init=lam_init)
    smem = pl.BlockSpec(memory_space=pltpu.SMEM)
    once = pl.Buffered(1)
    return pl.pallas_call(
        kernel,
        grid=(b, nh),
        in_specs=[
            smem, smem, smem, smem, smem,
            pl.BlockSpec((1, 1, nq, dk, tq), lambda bi, h: (bi, h, 0, 0, 0)),
            pl.BlockSpec((1, 1, s, dk), lambda bi, h: (bi, h, 0, 0)),
            pl.BlockSpec((s, LANES), lambda bi, h: (0, 0), pipeline_mode=once),
            pl.BlockSpec((1, 1, nq, DA_V_DIM, tq), lambda bi, h: (bi, h, 0, 0, 0)),
            pl.BlockSpec((1, 1, s, DA_V_DIM), lambda bi, h: (bi, h, 0, 0)),
            pl.BlockSpec((ATTN_LAMBDA_ROWS, LANES), lambda bi, h: (0, 0)),
            pl.BlockSpec((DA_V_DIM, 1), lambda bi, h: (0, 0)),
        ],
        out_specs=pl.BlockSpec((1, 1, s, DA_V_DIM), lambda bi, h: (bi, h, 0, 0)),
        out_shape=jax.ShapeDtypeStruct((b, nh, s, DA_V_DIM), BF16),
        scratch_shapes=(
            [pltpu.VMEM((nq, LANES, w), BF16),
             pltpu.VMEM((LANES, w), BF16)]
            + [pltpu.VMEM((tq, w), F32)] * ATTN_SCORE_SLOTS
            + [pltpu.VMEM((ATTN_STAT_ROWS, w), F32)] * ATTN_SCORE_SLOTS
            + [pltpu.VMEM((tq, w), BF16)] * ATTN_PROB_SLOTS
            + [pltpu.VMEM((ATTN_STAT_ROWS, w), F32)] * ATTN_PROB_SLOTS
            + [pltpu.VMEM((nq, 1, w), F32),
               pltpu.VMEM((nq, DA_V_DIM + ATTN_SUM_ROWS, w), F32),
               pltpu.VMEM((tq, w), F32),
               pltpu.SMEM((nq,), F32),
               pltpu.SMEM((len(qb),), jnp.int32),
               pltpu.SMEM((len(qb),), jnp.int32)]
        ),
        compiler_params=_cparams(("arbitrary", "arbitrary")),
        name="diff_attention",
    )(slopes, jnp.asarray(qb), jnp.asarray(kb), q_norms.reshape(-1), k_norms.reshape(-1), qT, k,
      _alibi_key_features(s), vT, g, lamv, subln.reshape(DA_V_DIM, 1))


def _ret_kernel(lg_ref, q_ref, k_ref, v_ref, g_ref, y_ref, state_ref, decay_ref, qdec_ref,
                kdec_ref, *, blk):
    n = pl.program_id(1)
    dk, dv = RET_QK_DIM, RET_V_DIM

    @pl.when(n == 0)
    def _():
        state_ref[...] = jnp.zeros(state_ref.shape, F32)
        ti = lax.broadcasted_iota(jnp.int32, (blk, blk), 0)
        ui = lax.broadcasted_iota(jnp.int32, (blk, blk), 1)
        dist = jnp.abs(ti - ui).astype(F32)
        allowed = (ui >> CHUNK_SHIFT) <= (ti >> CHUNK_SHIFT)
        t = lax.broadcasted_iota(jnp.int32, (blk, dk), 0).astype(F32)
        for h in range(RET_HEADS):
            decay_ref[h] = jnp.where(allowed, jnp.exp(lg_ref[h] * dist), 0.0) * dk ** -0.5
            qdec_ref[h] = jnp.exp(lg_ref[h] * t)
            kdec_ref[h] = jnp.exp(lg_ref[h] * (blk - t)) * dk ** -0.5

    def block(rows):
        for h in range(RET_HEADS):
            lg = lg_ref[h]
            q = q_ref[0, rows, h * dk:(h + 1) * dk]
            k = k_ref[0, rows, h * dk:(h + 1) * dk]
            v = v_ref[0, rows, h * dv:(h + 1) * dv]
            qd = (q.astype(F32) * qdec_ref[h]).astype(BF16)
            kd = (k.astype(F32) * kdec_ref[h]).astype(BF16)

            a = lax.dot_general(q, k, (((1,), (1,)), ((), ())), preferred_element_type=F32)
            a = (a * decay_ref[h]).astype(BF16)
            out = jnp.dot(a, v, preferred_element_type=F32)
            out = out + jnp.dot(qd, state_ref[h].astype(BF16), preferred_element_type=F32)
            kv = lax.dot_general(kd, v, (((0,), (0,)), ((), ())), preferred_element_type=F32)
            block_decay = jnp.exp(lg * jnp.full((1, 1), float(blk), F32))
            state_ref[h] = block_decay * state_ref[h] + kv

            ss = jnp.sum(out * out, axis=-1, keepdims=True)
            out = out * lax.rsqrt(ss * (1.0 / dv) + HEAD_NORM_EPS)
            g = g_ref[0, rows, h * dv:(h + 1) * dv].astype(F32)
            y_ref[0, rows, h * dv:(h + 1) * dv] = (g * jax.nn.sigmoid(g) * out).astype(BF16)

    for sub in range(q_ref.shape[1] // blk):
        block(slice(sub * blk, (sub + 1) * blk))


def _retention(q, k, v, g, log_gammas, blk, tile):
    b, s, qk_w = q.shape
    v_w = v.shape[-1]
    assert tile % blk == 0 and s % tile == 0
    return pl.pallas_call(
        functools.partial(_ret_kernel, blk=blk),
        grid=(b, s // tile),
        in_specs=[
            pl.BlockSpec(memory_space=pltpu.SMEM),
            pl.BlockSpec((1, tile, qk_w), lambda bi, i: (bi, i, 0)),
            pl.BlockSpec((1, tile, qk_w), lambda bi, i: (bi, i, 0)),
            pl.BlockSpec((1, tile, v_w), lambda bi, i: (bi, i, 0)),
            pl.BlockSpec((1, tile, v_w), lambda bi, i: (bi, i, 0)),
        ],
        out_specs=pl.BlockSpec((1, tile, v_w), lambda bi, i: (bi, i, 0)),
        out_shape=jax.ShapeDtypeStruct((b, s, v_w), BF16),
        scratch_shapes=[
            pltpu.VMEM((RET_HEADS, RET_QK_DIM, RET_V_DIM), F32),
            pltpu.VMEM((RET_HEADS, blk, blk), F32),
            pltpu.VMEM((RET_HEADS, blk, RET_QK_DIM), F32),
            pltpu.VMEM((RET_HEADS, blk, RET_QK_DIM), F32),
        ],
        compiler_params=_cparams(("arbitrary", "arbitrary")),
        name="retention",
    )(log_gammas, q, k, v, g)


def kernel(x, c, ada_w, ada_b, pre_gain, post_gain, da_w_in, da_w_out, da_lambda_q1, da_lambda_k1,
           da_lambda_q2, da_lambda_k2, da_subln_gain, ret_w_in, ret_w_out):
    b, s, d = x.shape
    mod = _modulation(c, ada_w, ada_b)

    qk_w = 2 * DA_HEADS * DA_HEAD_DIM
    v_w = DA_HEADS * DA_V_DIM
    q_scale = LOG2E * DA_HEAD_DIM ** -0.5
    da_plan = (("headT", 0, DA_HEADS, 2 * DA_HEAD_DIM, q_scale, 0),
               ("head", qk_w, DA_HEADS, 2 * DA_HEAD_DIM, 1.0, DA_HEADS),
               ("headT", 2 * qk_w, DA_HEADS, DA_V_DIM, 1.0, None),
               ("head", 2 * qk_w + v_w, DA_HEADS, DA_V_DIM, 1.0, None))
    mod0 = mod[0].reshape(b, 1, 3 * d)
    qT, k, vT, g, norms = _in_projection(x, mod0, pre_gain[0:1], da_w_in[0].astype(BF16),
                                         da_plan, tm=ATTN_TILE, slab=INPROJ_SLAB)
    q_norms = jnp.transpose(norms[:, :, 0:DA_HEADS, 0], (0, 2, 1))
    k_norms = jnp.transpose(norms[:, :, DA_HEADS:2 * DA_HEADS, 0], (0, 2, 1))
    lam_init = 0.8 - 0.6 * math.exp(-0.3 * 0)
    slopes = jnp.asarray([2.0 ** (-8.0 * (hh + 1) / DA_HEADS) for hh in range(DA_HEADS)], F32)
    lamv = jnp.stack([da_lambda_q1[0], da_lambda_k1[0], da_lambda_q2[0], da_lambda_k2[0]])
    lamv = jnp.pad(lamv, ((0, ATTN_LAMBDA_ROWS - 4), (0, LANES - DA_HEAD_DIM)))
    y = _diff_attention(qT, k, vT, g, q_norms, k_norms, slopes, lamv, da_subln_gain[0], lam_init)
    x = _out_projection(y, da_w_out[0].astype(BF16), x, mod0, post_gain[0:1], tm=OUTPROJ_ROWS)

    rqk = RET_HEADS * RET_QK_DIM
    rv = RET_HEADS * RET_V_DIM
    ret_plan = (("tok", 0, None, rqk, 1.0, None), ("tok", rqk, None, rqk, 1.0, None),
                ("tok", 2 * rqk, None, rv, 1.0, None), ("tok", 2 * rqk + rv, None, rv, 1.0, None))
    mod1 = mod[1].reshape(b, 1, 3 * d)
    q, k, v, g = _in_projection(x, mod1, pre_gain[1:2], ret_w_in[0].astype(BF16), ret_plan,
                                tm=RET_INPROJ_ROWS, slab=INPROJ_SLAB)
    gammas = 1.0 - 2.0 ** (-5.0 - jnp.arange(RET_HEADS, dtype=F32))
    y = _retention(q, k, v, g, jnp.log(gammas), blk=RET_BLOCK, tile=RET_TILE)
    return _out_projection(y, ret_w_out[0].astype(BF16), x, mod1, post_gain[1:2], tm=OUTPROJ_ROWS)
```

```python
import functools
import math

import jax
import jax.numpy as jnp
import numpy as np
from jax import lax
from jax.experimental import pallas as pl
from jax.experimental.pallas import tpu as pltpu

CHUNK = 64
CHUNK_SHIFT = 6
DA_HEADS = 8
DA_HEAD_DIM = 64
DA_V_DIM = 128
RET_HEADS = 4
RET_QK_DIM = 256
RET_V_DIM = 512
NORM_EPS = 1e-6
HEAD_NORM_EPS = 1e-5
MASK_VALUE = -1e30
LOG2E = 1.4426950408889634

V7X_VMEM_LIMIT_BYTES = 56 * 1024 * 1024
LANES = 128
SUBLANES = 8
BF16_EXACT_INT = 256

ATTN_TILE = 512
RET_INPROJ_ROWS = 1024
OUTPROJ_ROWS = 1024
RET_BLOCK = 256
RET_TILE = 1024
INPROJ_SLAB = 256
INPROJ_WEIGHT_PARTS = 4

BF16 = jnp.bfloat16
F32 = jnp.float32


def _cparams(sem, vmem=V7X_VMEM_LIMIT_BYTES, flags=None):
    return pltpu.CompilerParams(dimension_semantics=sem, vmem_limit_bytes=vmem, flags=flags)


def _mod_kernel(c_ref, w_ref, b_ref, o_ref):
    c = c_ref[...]
    cond = c * jax.nn.sigmoid(c)
    o_ref[0] = jnp.dot(cond, w_ref[0], preferred_element_type=F32) + b_ref[0]


def _modulation(c, ada_w, ada_b):
    depth, d, d3 = ada_w.shape
    b = c.shape[0]
    n_tiles = d3 // d
    return pl.pallas_call(
        _mod_kernel,
        grid=(depth, n_tiles),
        in_specs=[
            pl.BlockSpec((b, d), lambda l, j: (0, 0)),
            pl.BlockSpec((1, d, d), lambda l, j: (l, 0, j)),
            pl.BlockSpec((1, 1, d), lambda l, j: (l, 0, j)),
        ],
        out_specs=pl.BlockSpec((1, b, d), lambda l, j: (l, 0, j)),
        out_shape=jax.ShapeDtypeStruct((depth, b, d3), F32),
        compiler_params=_cparams(("arbitrary", "arbitrary")),
        name="modulation",
    )(c, ada_w, ada_b.reshape(depth, 1, d3))


def _inproj_kernel(x_ref, mod_ref, gain_ref, *refs, d_model, plan, slab, n_norm_rows):
    w_refs, refs = refs[:INPROJ_WEIGHT_PARTS], refs[INPROJ_WEIGHT_PARTS:]
    part_w = w_refs[0].shape[1]
    out_refs = refs[:len(plan)]
    norm_ref = refs[len(plan)] if n_norm_rows else None
    hb_ref = refs[-1]
    x = x_ref[0]
    ms = jnp.mean(x * x, axis=-1, keepdims=True)
    hn = x * lax.rsqrt(ms + NORM_EPS) * gain_ref[...]
    shift = mod_ref[0, :, 0:d_model]
    scale = mod_ref[0, :, d_model:2 * d_model]
    hb_ref[...] = (hn * (1.0 + scale) + shift).astype(BF16)
    for (kind, c0, n_heads, width, out_scale, norm_row), o_ref in zip(plan, out_refs):
        total = width if kind == "tok" else n_heads * width
        for s0 in range(0, total, slab):
            part, col = divmod(c0 + s0, part_w)
            r = jnp.dot(hb_ref[...], w_refs[part][:, col:col + slab],
                        preferred_element_type=F32)
            if out_scale != 1.0:
                r = r * out_scale
            if kind == "tok":
                o_ref[0, :, s0:s0 + slab] = r.astype(BF16)
            elif kind == "head" and width >= slab:
                hh, off = divmod(s0, width)
                o_ref[0, hh, :, off:off + slab] = r.astype(BF16)
            else:
                for p in range(slab // width):
                    head = s0 // width + p
                    piece = r[:, p * width:(p + 1) * width]
                    if kind == "head":
                        o_ref[0, head] = piece.astype(BF16)
                    else:
                        o_ref[0, head, 0] = piece.T.astype(BF16)
                    if norm_row is not None:
                        norm2 = jnp.max(jnp.sum(piece * piece, axis=1, keepdims=True),
                                        axis=0, keepdims=True)
                        norm_ref[0, 0, norm_row + head:norm_row + head + 1, :] = (
                            jnp.broadcast_to(jnp.sqrt(norm2), (1, LANES)))


def _in_projection(x, mod, gain, w_bf16, plan, tm, slab):
    b, s, d = x.shape
    n = w_bf16.shape[1]
    out_shapes, out_specs = [], []
    n_norm_rows = sum(e[2] for e in plan if e[5] is not None)
    for (kind, c0, n_heads, width, _, _) in plan:
        if kind == "tok":
            out_shapes.append(jax.ShapeDtypeStruct((b, s, width), BF16))
            out_specs.append(pl.BlockSpec((1, tm, width), lambda bi, i: (bi, i, 0)))
        elif kind == "head":
            out_shapes.append(jax.ShapeDtypeStruct((b, n_heads, s, width), BF16))
            out_specs.append(pl.BlockSpec((1, n_heads, tm, width), lambda bi, i: (bi, 0, i, 0)))
        else:
            assert slab % width == 0
            out_shapes.append(jax.ShapeDtypeStruct((b, n_heads, s // tm, width, tm), BF16))
            out_specs.append(pl.BlockSpec((1, n_heads, 1, width, tm),
                                          lambda bi, i: (bi, 0, i, 0, 0)))
    if n_norm_rows:
        out_shapes.append(jax.ShapeDtypeStruct((b, s // tm, n_norm_rows, LANES), F32))
        out_specs.append(pl.BlockSpec((1, 1, n_norm_rows, LANES), lambda bi, i: (bi, i, 0, 0)))
    return pl.pallas_call(
        functools.partial(_inproj_kernel, d_model=d, plan=plan, slab=slab,
                          n_norm_rows=n_norm_rows),
        grid=(b, s // tm),
        in_specs=[
            pl.BlockSpec((1, tm, d), lambda bi, i: (bi, i, 0)),
            pl.BlockSpec((1, 1, 3 * d), lambda bi, i: (bi, 0, 0)),
            pl.BlockSpec((1, d), lambda bi, i: (0, 0)),
        ] + [
            pl.BlockSpec((d, n // INPROJ_WEIGHT_PARTS), lambda bi, i, j=j: (0, j),
                         pipeline_mode=pl.Buffered(1))
            for j in range(INPROJ_WEIGHT_PARTS)
        ],
        out_specs=out_specs,
        out_shape=out_shapes,
        scratch_shapes=[pltpu.VMEM((tm, d), BF16)],
        compiler_params=_cparams(("arbitrary", "arbitrary")),
        name="in_projection",
    )(x, mod, gain, *([w_bf16] * INPROJ_WEIGHT_PARTS))


def _outproj_kernel(y_ref, w_ref, x_ref, mod_ref, gain_ref, o_ref, *, d_model):
    t = jnp.dot(y_ref[0], w_ref[...], preferred_element_type=F32)
    ms = jnp.mean(t * t, axis=-1, keepdims=True)
    n = t * lax.rsqrt(ms + NORM_EPS) * gain_ref[...]
    gate = mod_ref[0, :, 2 * d_model:3 * d_model]
    o_ref[0] = x_ref[0] + gate * n


def _out_projection(y, w_bf16, x, mod, gain, tm):
    b, s, d = x.shape
    k = y.shape[-1]
    return pl.pallas_call(
        functools.partial(_outproj_kernel, d_model=d),
        grid=(b, s // tm),
        in_specs=[
            pl.BlockSpec((1, tm, k), lambda bi, i: (bi, i, 0)),
            pl.BlockSpec((k, d), lambda bi, i: (0, 0)),
            pl.BlockSpec((1, tm, d), lambda bi, i: (bi, i, 0)),
            pl.BlockSpec((1, 1, 3 * d), lambda bi, i: (bi, 0, 0)),
            pl.BlockSpec((1, d), lambda bi, i: (0, 0)),
        ],
        out_specs=pl.BlockSpec((1, tm, d), lambda bi, i: (bi, i, 0)),
        out_shape=jax.ShapeDtypeStruct((b, s, d), F32),
        compiler_params=_cparams(("arbitrary", "arbitrary")),
        name="out_projection",
    )(y, w_bf16, x, mod, gain)


N_ALIBI_FEATURES = 6
ATTN_ROW_CHUNK = 16
ATTN_SUM_ROWS = 16
ATTN_SCORE_SLOTS = 4
ATTN_PROB_SLOTS = 2
ATTN_DIAG_STEPS = 8
ATTN_PAST_STEPS = 8
ATTN_PIPE_DEPTH = 3
ATTN_LAMBDA_ROWS = 32
ATTN_STAT_ROWS = 4
ATTN_ZERO_EXP2 = 150.0
ATTN_BOUND_SLACK = 1.0
ATTN_NORM_SLACK = 1.01


def _alibi_key_features(s):
    pos = jnp.arange(s, dtype=jnp.int32)[:, None]
    col = jnp.arange(LANES, dtype=jnp.int32)[None, :]
    lo = (pos % BF16_EXACT_INT).astype(F32)
    hi = (pos // BF16_EXACT_INT).astype(F32)
    feat = jnp.where(col < 3, lo, jnp.where(col < N_ALIBI_FEATURES, hi, 0.0))
    return feat.astype(BF16)


def _attn_schedule(nq):
    pairs = [(qi, qi) for qi in range(nq)] + [(qi, j) for qi in range(nq) for j in range(qi)]
    qb, kb = zip(*pairs)
    return np.asarray(qb, np.int32), np.asarray(kb, np.int32)


def _attn_kernel(slope_ref, qb_ref, kb_ref, qn_ref, kn_ref, qT_ref, k_ref, kf_ref, vT_ref, g_ref,
                 lamv_ref, subln_ref, y_ref, qs_ref, feat_ref, *scratch, tq, nq, lam_init):
    h = pl.program_id(1)
    tk = tq
    w = 2 * tq
    rc = ATTN_ROW_CHUNK
    n_chunks = tk // rc
    n_pos = nq * (nq + 1) // 2
    scratch = list(scratch)
    s_refs = [scratch.pop(0) for _ in range(ATTN_SCORE_SLOTS)]
    mc_refs = [scratch.pop(0) for _ in range(ATTN_SCORE_SLOTS)]
    p_refs = [scratch.pop(0) for _ in range(ATTN_PROB_SLOTS)]
    al_refs = [scratch.pop(0) for _ in range(ATTN_PROB_SLOTS)]
    m_ref, acc_ref, corr_ref, thr_ref, ql_ref, kl_ref = scratch
    n_past = n_pos - nq
    assert nq % ATTN_DIAG_STEPS == 0 and nq >= ATTN_DIAG_STEPS
    assert ATTN_DIAG_STEPS % ATTN_SCORE_SLOTS == 0 and ATTN_PAST_STEPS % ATTN_SCORE_SLOTS == 0
    assert n_past % ATTN_PAST_STEPS == 0

    slope2 = slope_ref[h] * LOG2E
    fr = lax.broadcasted_iota(jnp.int32, (LANES, w), 0)
    x = jnp.full((LANES, w), slope2, F32)
    hi = x.astype(BF16).astype(F32)
    mid = (x - hi).astype(BF16).astype(F32)
    lo = x - hi - mid
    part = jnp.where((fr == 0) | (fr == 3), hi, jnp.where((fr == 1) | (fr == 4), mid, lo))
    feat = jnp.where(fr < 3, part, jnp.where(fr < N_ALIBI_FEATURES, part * BF16_EXACT_INT, 0.0))
    feat_ref[...] = feat.astype(BF16)
    kk = lax.broadcasted_iota(jnp.int32, (tk, w), 0)
    qq = lax.broadcasted_iota(jnp.int32, (tk, w), 1)
    qq = jnp.where(qq >= tq, qq - tq, qq)
    ahead = (kk - qq).astype(F32)
    same_chunk = (kk >> CHUNK_SHIFT) == (qq >> CHUNK_SHIFT)
    corr_ref[...] = jnp.where(kk <= qq, 0.0,
                              jnp.where(same_chunk, -2.0 * slope2 * ahead, MASK_VALUE))

    def init_query_block(qi, carry):
        qT = qT_ref[0, 0, qi]
        row = lax.broadcasted_iota(jnp.int32, qT.shape, 0)
        zero = jnp.zeros_like(qT)
        qs_ref[qi, :, 0:tq] = jnp.where(row < DA_HEAD_DIM, qT, zero)
        qs_ref[qi, :, tq:w] = jnp.where(row >= DA_HEAD_DIM, qT, zero)
        return carry

    lax.fori_loop(0, nq, init_query_block, 0)
    ones_rows = jnp.ones((ATTN_SUM_ROWS, tk), BF16)

    def qk(pair, t, s_slot, diag):
        qi, kj = pair(t)
        start = kj * tk if isinstance(kj, int) else pl.multiple_of(kj * tk, tk)
        ka = jnp.concatenate([k_ref[0, 0, pl.ds(start, tk), :], kf_ref[pl.ds(start, tk), :]],
                             axis=1)
        qa = jnp.concatenate([qs_ref[qi], feat_ref[...]], axis=0)
        s = jnp.dot(ka, qa, preferred_element_type=F32)
        if diag:
            s = s + corr_ref[...]
        s_refs[s_slot][...] = s
        mc_refs[s_slot][0:1, :] = jnp.max(s, axis=0, keepdims=True)

    def sm(pair, t, s_slot, p_slot, first):
        qi, _ = pair(t)
        if first:
            m_next = mc_refs[s_slot][0:1, :]
        else:
            m_prev = m_ref[qi]
            m_next = jnp.maximum(m_prev, mc_refs[s_slot][0:1, :])
            al_refs[p_slot][0:1, :] = jnp.exp2(m_prev - m_next)
        m_ref[qi] = m_next
        m_rows = jnp.broadcast_to(m_next, (rc, w))
        for c in range(n_chunks):
            p = jnp.exp2(s_refs[s_slot][c * rc:(c + 1) * rc, :] - m_rows)
            p_refs[p_slot][c * rc:(c + 1) * rc, :] = p.astype(BF16)

    def pv(pair, t, p_slot, first):
        qi, kj = pair(t)
        v_aug = jnp.concatenate([vT_ref[0, 0, kj], ones_rows], axis=0)
        upd = jnp.dot(v_aug, p_refs[p_slot][...], preferred_element_type=F32)
        if first:
            acc_ref[qi] = upd
        else:
            acc_ref[qi] = al_refs[p_slot][0:1, :] * acc_ref[qi] + upd

    def run_pipeline(pair, n, steps, diag):
        depth = ATTN_PIPE_DEPTH

        def time_step(tau, u, do_qk=True, do_sm=True, do_pv=True):
            if do_qk:
                qk(pair, tau, u, diag)
            if do_sm:
                sm(pair, tau - (depth - 1), (u - (depth - 1)) % ATTN_SCORE_SLOTS,
                   (u - (depth - 1)) % ATTN_PROB_SLOTS, diag)
            if do_pv:
                pv(pair, tau - depth, (u - depth) % ATTN_PROB_SLOTS, diag)

        for tau in range(depth):
            time_step(tau, tau % ATTN_SCORE_SLOTS, do_sm=tau >= depth - 1, do_pv=False)

        def body(i, carry):
            for u in range(steps):
                time_step(depth + steps * i + u, (depth + u) % ATTN_SCORE_SLOTS)
            return carry

        lax.fori_loop(0, n // steps - 1, body, 0)
        for u in range(steps - depth):
            time_step(n - (steps - depth) + u, (depth + u) % ATTN_SCORE_SLOTS)
        for d in range(depth):
            time_step(n + d, d % ATTN_SCORE_SLOTS, do_qk=False, do_sm=d < depth - 1)

    run_pipeline(lambda t: (t, t), nq, ATTN_DIAG_STEPS, diag=True)

    norm_base = (pl.program_id(0) * pl.num_programs(1) + h) * nq
    m_min = jnp.min(m_ref[...].reshape(nq, w), axis=1, keepdims=True)
    for qi in range(nq):
        thr_ref[qi] = m_min[qi, 0] - (ATTN_ZERO_EXP2 + ATTN_BOUND_SLACK)

    def choose(i, carry):
        count, pad_q, pad_k = carry
        qi, kj = qb_ref[nq + i], kb_ref[nq + i]
        last_key = ((kj + 1) * tk - 1).astype(F32)
        bound = (qn_ref[norm_base + qi] * kn_ref[norm_base + kj] * ATTN_NORM_SLACK
                 + slope2 * last_key)
        keep = bound >= thr_ref[qi]
        ql_ref[count] = qi
        kl_ref[count] = kj
        return (count + keep.astype(jnp.int32), jnp.where(keep, pad_q, qi),
                jnp.where(keep, pad_k, kj))

    n_keep, pad_q, pad_k = lax.fori_loop(0, n_past, choose,
                                         (jnp.int32(0), jnp.int32(0), jnp.int32(0)))
    n_pad = (-n_keep) & (ATTN_PAST_STEPS - 1)
    for r in range(ATTN_PAST_STEPS - 1):
        @pl.when(r < n_pad)
        def _():
            ql_ref[n_keep + r] = pad_q
            kl_ref[n_keep + r] = pad_k

    n_visit = n_keep + n_pad

    @pl.when(n_visit > 0)
    def _():
        run_pipeline(lambda t: (ql_ref[t], kl_ref[t]), n_visit, ATTN_PAST_STEPS, diag=False)

    lv = lamv_ref[...]
    lam = (jnp.exp(jnp.sum(lv[0:1] * lv[1:2], axis=-1, keepdims=True))
           - jnp.exp(jnp.sum(lv[2:3] * lv[3:4], axis=-1, keepdims=True)) + lam_init)

    def finish_query_block(qi, carry):
        o = acc_ref[qi, 0:DA_V_DIM, :] * (1.0 / acc_ref[qi, DA_V_DIM:DA_V_DIM + 1, :])
        out = o[:, 0:tq] - lam * o[:, tq:w]
        ms = jnp.mean(out * out, axis=0, keepdims=True)
        out = out * lax.rsqrt(ms + HEAD_NORM_EPS) * subln_ref[...] * (1.0 - lam_init)
        rows = pl.ds(pl.multiple_of(qi * tq, tq), tq)
        g = g_ref[0, rows, :].astype(F32)
        y_ref[0, rows, :] = (g * jax.nn.sigmoid(g) * out.T).astype(BF16)
        return carry

    def finish_two(i, carry):
        finish_query_block(2 * i, carry)
        return finish_query_block(2 * i + 1, carry)

    lax.fori_loop(0, nq // 2, finish_two, 0)


def _diff_attention(qT, k, vT, g, q_norms, k_norms, slopes, lamv, subln, lam_init):
    b, nh, nq, dk, tq = qT.shape
    s = k.shape[2]
    w = 2 * tq
    qb, kb = _attn_schedule(nq)
    kernel = functools.partial(_attn_kernel, tq=tq, nq=nq, lam_init=lam_init)
    smem = pl.BlockSpec(memory_space=pltpu.SMEM)
    once = pl.Buffered(1)
    return pl.pallas_call(
        kernel,
        grid=(b, nh),
        in_specs=[
            smem, smem, smem, smem, smem,
            pl.BlockSpec((1, 1, nq, dk, tq), lambda bi, h: (bi, h, 0, 0, 0)),
            pl.BlockSpec((1, 1, s, dk), lambda bi, h: (bi, h, 0, 0)),
            pl.BlockSpec((s, LANES), lambda bi, h: (0, 0), pipeline_mode=once),
            pl.BlockSpec((1, 1, nq, DA_V_DIM, tq), lambda bi, h: (bi, h, 0, 0, 0)),
            pl.BlockSpec((1, s, DA_V_DIM), lambda bi, h: (bi, 0, h)),
            pl.BlockSpec((ATTN_LAMBDA_ROWS, LANES), lambda bi, h: (0, 0)),
            pl.BlockSpec((DA_V_DIM, 1), lambda bi, h: (0, 0)),
        ],
        out_specs=pl.BlockSpec((1, s, DA_V_DIM), lambda bi, h: (bi, 0, h)),
        out_shape=jax.ShapeDtypeStruct((b, s, nh * DA_V_DIM), BF16),
        scratch_shapes=(
            [pltpu.VMEM((nq, LANES, w), BF16),
             pltpu.VMEM((LANES, w), BF16)]
            + [pltpu.VMEM((tq, w), F32)] * ATTN_SCORE_SLOTS
            + [pltpu.VMEM((ATTN_STAT_ROWS, w), F32)] * ATTN_SCORE_SLOTS
            + [pltpu.VMEM((tq, w), BF16)] * ATTN_PROB_SLOTS
            + [pltpu.VMEM((ATTN_STAT_ROWS, w), F32)] * ATTN_PROB_SLOTS
            + [pltpu.VMEM((nq, 1, w), F32),
               pltpu.VMEM((nq, DA_V_DIM + ATTN_SUM_ROWS, w), F32),
               pltpu.VMEM((tq, w), F32),
               pltpu.SMEM((nq,), F32),
               pltpu.SMEM((len(qb),), jnp.int32),
               pltpu.SMEM((len(qb),), jnp.int32)]
        ),
        compiler_params=_cparams(("arbitrary", "arbitrary")),
        name="diff_attention",
    )(slopes, jnp.asarray(qb), jnp.asarray(kb), q_norms.reshape(-1), k_norms.reshape(-1), qT, k,
      _alibi_key_features(s), vT, g, lamv, subln.reshape(DA_V_DIM, 1))


def _ret_kernel(lg_ref, q_ref, k_ref, v_ref, g_ref, y_ref, state_ref, decay_ref, qdec_ref,
                kdec_ref, *, blk):
    n = pl.program_id(1)
    dk, dv = RET_QK_DIM, RET_V_DIM

    @pl.when(n == 0)
    def _():
        state_ref[...] = jnp.zeros(state_ref.shape, F32)
        ti = lax.broadcasted_iota(jnp.int32, (blk, blk), 0)
        ui = lax.broadcasted_iota(jnp.int32, (blk, blk), 1)
        dist = jnp.abs(ti - ui).astype(F32)
        allowed = (ui >> CHUNK_SHIFT) <= (ti >> CHUNK_SHIFT)
        t = lax.broadcasted_iota(jnp.int32, (blk, dk), 0).astype(F32)
        for h in range(RET_HEADS):
            decay_ref[h] = jnp.where(allowed, jnp.exp(lg_ref[h] * dist), 0.0) * dk ** -0.5
            qdec_ref[h] = jnp.exp(lg_ref[h] * t)
            kdec_ref[h] = jnp.exp(lg_ref[h] * (blk - t)) * dk ** -0.5

    def block(rows):
        for h in range(RET_HEADS):
            lg = lg_ref[h]
            q = q_ref[0, rows, h * dk:(h + 1) * dk]
            k = k_ref[0, rows, h * dk:(h + 1) * dk]
            v = v_ref[0, rows, h * dv:(h + 1) * dv]
            qd = (q.astype(F32) * qdec_ref[h]).astype(BF16)
            kd = (k.astype(F32) * kdec_ref[h]).astype(BF16)

            a = lax.dot_general(q, k, (((1,), (1,)), ((), ())), preferred_element_type=F32)
            a = (a * decay_ref[h]).astype(BF16)
            out = jnp.dot(a, v, preferred_element_type=F32)
            out = out + jnp.dot(qd, state_ref[h].astype(BF16), preferred_element_type=F32)
            kv = lax.dot_general(kd, v, (((0,), (0,)), ((), ())), preferred_element_type=F32)
            block_decay = jnp.exp(lg * jnp.full((1, 1), float(blk), F32))
            state_ref[h] = block_decay * state_ref[h] + kv

            ss = jnp.sum(out * out, axis=-1, keepdims=True)
            out = out * lax.rsqrt(ss * (1.0 / dv) + HEAD_NORM_EPS)
            g = g_ref[0, rows, h * dv:(h + 1) * dv].astype(F32)
            y_ref[0, rows, h * dv:(h + 1) * dv] = (g * jax.nn.sigmoid(g) * out).astype(BF16)

    for sub in range(q_ref.shape[1] // blk):
        block(slice(sub * blk, (sub + 1) * blk))


def _retention(q, k, v, g, log_gammas, blk, tile):
    b, s, qk_w = q.shape
    v_w = v.shape[-1]
    assert tile % blk == 0 and s % tile == 0
    return pl.pallas_call(
        functools.partial(_ret_kernel, blk=blk),
        grid=(b, s // tile),
        in_specs=[
            pl.BlockSpec(memory_space=pltpu.SMEM),
            pl.BlockSpec((1, tile, qk_w), lambda bi, i: (bi, i, 0)),
            pl.BlockSpec((1, tile, qk_w), lambda bi, i: (bi, i, 0)),
            pl.BlockSpec((1, tile, v_w), lambda bi, i: (bi, i, 0)),
            pl.BlockSpec((1, tile, v_w), lambda bi, i: (bi, i, 0)),
        ],
        out_specs=pl.BlockSpec((1, tile, v_w), lambda bi, i: (bi, i, 0)),
        out_shape=jax.ShapeDtypeStruct((b, s, v_w), BF16),
        scratch_shapes=[
            pltpu.VMEM((RET_HEADS, RET_QK_DIM, RET_V_DIM), F32),
            pltpu.VMEM((RET_HEADS, blk, blk), F32),
            pltpu.VMEM((RET_HEADS, blk, RET_QK_DIM), F32),
            pltpu.VMEM((RET_HEADS, blk, RET_QK_DIM), F32),
        ],
        compiler_params=_cparams(("arbitrary", "arbitrary")),
        name="retention",
    )(log_gammas, q, k, v, g)


def kernel(x, c, ada_w, ada_b, pre_gain, post_gain, da_w_in, da_w_out, da_lambda_q1, da_lambda_k1,
           da_lambda_q2, da_lambda_k2, da_subln_gain, ret_w_in, ret_w_out):
    b, s, d = x.shape
    mod = _modulation(c, ada_w, ada_b)

    qk_w = 2 * DA_HEADS * DA_HEAD_DIM
    v_w = DA_HEADS * DA_V_DIM
    q_scale = LOG2E * DA_HEAD_DIM ** -0.5
    da_plan = (("headT", 0, DA_HEADS, 2 * DA_HEAD_DIM, q_scale, 0),
               ("head", qk_w, DA_HEADS, 2 * DA_HEAD_DIM, 1.0, DA_HEADS),
               ("headT", 2 * qk_w, DA_HEADS, DA_V_DIM, 1.0, None),
               ("tok", 2 * qk_w + v_w, None, v_w, 1.0, None))
    mod0 = mod[0].reshape(b, 1, 3 * d)
    qT, k, vT, g, norms = _in_projection(x, mod0, pre_gain[0:1], da_w_in[0].astype(BF16),
                                         da_plan, tm=ATTN_TILE, slab=INPROJ_SLAB)
    q_norms = jnp.transpose(norms[:, :, 0:DA_HEADS, 0], (0, 2, 1))
    k_norms = jnp.transpose(norms[:, :, DA_HEADS:2 * DA_HEADS, 0], (0, 2, 1))
    lam_init = 0.8 - 0.6 * math.exp(-0.3 * 0)
    slopes = jnp.asarray([2.0 ** (-8.0 * (hh + 1) / DA_HEADS) for hh in range(DA_HEADS)], F32)
    lamv = jnp.stack([da_lambda_q1[0], da_lambda_k1[0], da_lambda_q2[0], da_lambda_k2[0]])
    lamv = jnp.pad(lamv, ((0, ATTN_LAMBDA_ROWS - 4), (0, LANES - DA_HEAD_DIM)))
    y = _diff_attention(qT, k, vT, g, q_norms, k_norms, slopes, lamv, da_subln_gain[0], lam_init)
    x = _out_projection(y, da_w_out[0].astype(BF16), x, mod0, post_gain[0:1], tm=OUTPROJ_ROWS)

    rqk = RET_HEADS * RET_QK_DIM
    rv = RET_HEADS * RET_V_DIM
    ret_plan = (("tok", 0, None, rqk, 1.0, None), ("tok", rqk, None, rqk, 1.0, None),
                ("tok", 2 * rqk, None, rv, 1.0, None), ("tok", 2 * rqk + rv, None, rv, 1.0, None))
    mod1 = mod[1].reshape(b, 1, 3 * d)
    q, k, v, g = _in_projection(x, mod1, pre_gain[1:2], ret_w_in[0].astype(BF16), ret_plan,
                                tm=RET_INPROJ_ROWS, slab=INPROJ_SLAB)
    gammas = 1.0 - 2.0 ** (-5.0 - jnp.arange(RET_HEADS, dtype=F32))
    y = _retention(q, k, v, g, jnp.log(gammas), blk=RET_BLOCK, tile=RET_TILE)
    return _out_projection(y, ret_w_out[0].astype(BF16), x, mod1, post_gain[1:2], tm=OUTPROJ_ROWS)
```

```python
import functools
import math

import jax
import jax.numpy as jnp
import numpy as np
from jax import lax
from jax.experimental import pallas as pl
from jax.experimental.pallas import tpu as pltpu

CHUNK = 64
CHUNK_SHIFT = 6
DA_HEADS = 8
DA_HEAD_DIM = 64
DA_V_DIM = 128
RET_HEADS = 4
RET_QK_DIM = 256
RET_V_DIM = 512
NORM_EPS = 1e-6
HEAD_NORM_EPS = 1e-5
MASK_VALUE = -1e30
LOG2E = 1.4426950408889634

V7X_VMEM_LIMIT_BYTES = 56 * 1024 * 1024
LANES = 128
SUBLANES = 8
BF16_EXACT_INT = 256

ATTN_TILE = 512
BOUNDARY_ROWS = 512
OUTPROJ_ROWS = 1024
RET_BLOCK = 256
RET_TILE = 1024
INPROJ_SLAB = 256

BF16 = jnp.bfloat16
F32 = jnp.float32


def _cparams(sem, vmem=V7X_VMEM_LIMIT_BYTES, flags=None):
    return pltpu.CompilerParams(dimension_semantics=sem, vmem_limit_bytes=vmem, flags=flags)


def _mod_kernel(c_ref, w_ref, b_ref, o_ref):
    c = c_ref[...]
    cond = c * jax.nn.sigmoid(c)
    o_ref[0] = jnp.dot(cond, w_ref[0], preferred_element_type=F32) + b_ref[0]


def _modulation(c, ada_w, ada_b):
    depth, d, d3 = ada_w.shape
    b = c.shape[0]
    n_tiles = d3 // d
    return pl.pallas_call(
        _mod_kernel,
        grid=(depth, n_tiles),
        in_specs=[
            pl.BlockSpec((b, d), lambda l, j: (0, 0)),
            pl.BlockSpec((1, d, d), lambda l, j: (l, 0, j)),
            pl.BlockSpec((1, 1, d), lambda l, j: (l, 0, j)),
        ],
        out_specs=pl.BlockSpec((1, b, d), lambda l, j: (l, 0, j)),
        out_shape=jax.ShapeDtypeStruct((depth, b, d3), F32),
        compiler_params=_cparams(("arbitrary", "arbitrary")),
        name="modulation",
    )(c, ada_w, ada_b.reshape(depth, 1, d3))


def _inproj_kernel(x_ref, mod_ref, gain_ref, w_ref, *refs, d_model, plan, slab, n_norm_rows):
    out_refs = refs[:len(plan)]
    norm_ref = refs[len(plan)] if n_norm_rows else None
    hb_ref = refs[-1]
    x = x_ref[0]
    ms = jnp.mean(x * x, axis=-1, keepdims=True)
    hn = x * lax.rsqrt(ms + NORM_EPS) * gain_ref[...]
    shift = mod_ref[0, :, 0:d_model]
    scale = mod_ref[0, :, d_model:2 * d_model]
    hb_ref[...] = (hn * (1.0 + scale) + shift).astype(BF16)
    for (kind, c0, n_heads, width, out_scale, norm_row), o_ref in zip(plan, out_refs):
        total = width if kind == "tok" else n_heads * width
        for s0 in range(0, total, slab):
            r = jnp.dot(hb_ref[...], w_ref[:, c0 + s0:c0 + s0 + slab],
                        preferred_element_type=F32)
            if out_scale != 1.0:
                r = r * out_scale
            if kind == "tok":
                o_ref[0, :, s0:s0 + slab] = r.astype(BF16)
            elif kind == "head" and width >= slab:
                hh, off = divmod(s0, width)
                o_ref[0, hh, :, off:off + slab] = r.astype(BF16)
            else:
                for p in range(slab // width):
                    head = s0 // width + p
                    piece = r[:, p * width:(p + 1) * width]
                    if kind == "head":
                        o_ref[0, head] = piece.astype(BF16)
                    else:
                        o_ref[0, head, 0] = piece.T.astype(BF16)
                    if norm_row is not None:
                        norm2 = jnp.max(jnp.sum(piece * piece, axis=1, keepdims=True),
                                        axis=0, keepdims=True)
                        norm_ref[0, 0, norm_row + head:norm_row + head + 1, :] = (
                            jnp.broadcast_to(jnp.sqrt(norm2), (1, LANES)))


def _in_projection(x, mod, gain, w_bf16, plan, tm, slab):
    b, s, d = x.shape
    n = w_bf16.shape[1]
    out_shapes, out_specs = [], []
    n_norm_rows = sum(e[2] for e in plan if e[5] is not None)
    for (kind, c0, n_heads, width, _, _) in plan:
        if kind == "tok":
            out_shapes.append(jax.ShapeDtypeStruct((b, s, width), BF16))
            out_specs.append(pl.BlockSpec((1, tm, width), lambda bi, i: (bi, i, 0)))
        elif kind == "head":
            out_shapes.append(jax.ShapeDtypeStruct((b, n_heads, s, width), BF16))
            out_specs.append(pl.BlockSpec((1, n_heads, tm, width), lambda bi, i: (bi, 0, i, 0)))
        else:
            assert slab % width == 0
            out_shapes.append(jax.ShapeDtypeStruct((b, n_heads, s // tm, width, tm), BF16))
            out_specs.append(pl.BlockSpec((1, n_heads, 1, width, tm),
                                          lambda bi, i: (bi, 0, i, 0, 0)))
    if n_norm_rows:
        out_shapes.append(jax.ShapeDtypeStruct((b, s // tm, n_norm_rows, LANES), F32))
        out_specs.append(pl.BlockSpec((1, 1, n_norm_rows, LANES), lambda bi, i: (bi, i, 0, 0)))
    return pl.pallas_call(
        functools.partial(_inproj_kernel, d_model=d, plan=plan, slab=slab,
                          n_norm_rows=n_norm_rows),
        grid=(b, s // tm),
        in_specs=[
            pl.BlockSpec((1, tm, d), lambda bi, i: (bi, i, 0)),
            pl.BlockSpec((1, 1, 3 * d), lambda bi, i: (bi, 0, 0)),
            pl.BlockSpec((1, d), lambda bi, i: (0, 0)),
            pl.BlockSpec((d, n), lambda bi, i: (0, 0), pipeline_mode=pl.Buffered(1)),
        ],
        out_specs=out_specs,
        out_shape=out_shapes,
        scratch_shapes=[pltpu.VMEM((tm, d), BF16)],
        compiler_params=_cparams(("arbitrary", "arbitrary")),
        name="in_projection",
    )(x, mod, gain, w_bf16)


def _outproj_kernel(y_ref, w_ref, x_ref, mod_ref, gain_ref, o_ref, *, d_model):
    t = jnp.dot(y_ref[0], w_ref[...], preferred_element_type=F32)
    ms = jnp.mean(t * t, axis=-1, keepdims=True)
    n = t * lax.rsqrt(ms + NORM_EPS) * gain_ref[...]
    gate = mod_ref[0, :, 2 * d_model:3 * d_model]
    o_ref[0] = x_ref[0] + gate * n


def _out_projection(y, w_bf16, x, mod, gain, tm):
    b, s, d = x.shape
    k = y.shape[-1]
    return pl.pallas_call(
        functools.partial(_outproj_kernel, d_model=d),
        grid=(b, s // tm),
        in_specs=[
            pl.BlockSpec((1, tm, k), lambda bi, i: (bi, i, 0)),
            pl.BlockSpec((k, d), lambda bi, i: (0, 0)),
            pl.BlockSpec((1, tm, d), lambda bi, i: (bi, i, 0)),
            pl.BlockSpec((1, 1, 3 * d), lambda bi, i: (bi, 0, 0)),
            pl.BlockSpec((1, d), lambda bi, i: (0, 0)),
        ],
        out_specs=pl.BlockSpec((1, tm, d), lambda bi, i: (bi, i, 0)),
        out_shape=jax.ShapeDtypeStruct((b, s, d), F32),
        compiler_params=_cparams(("arbitrary", "arbitrary")),
        name="out_projection",
    )(y, w_bf16, x, mod, gain)


def _boundary_kernel(y_ref, wo_ref, x_ref, mod_a_ref, post_gain_ref, mod_b_ref, pre_gain_ref, wi_ref,
                     x_out_ref, *refs, d_model, widths, slab):
    out_refs, hb_ref = refs[:len(widths)], refs[-1]
    t = jnp.dot(y_ref[0], wo_ref[...], preferred_element_type=F32)
    ms = jnp.mean(t * t, axis=-1, keepdims=True)
    n = t * lax.rsqrt(ms + NORM_EPS) * post_gain_ref[...]
    x_new = x_ref[0] + mod_a_ref[0, :, 2 * d_model:3 * d_model] * n
    x_out_ref[0] = x_new
    ms = jnp.mean(x_new * x_new, axis=-1, keepdims=True)
    hn = x_new * lax.rsqrt(ms + NORM_EPS) * pre_gain_ref[...]
    shift = mod_b_ref[0, :, 0:d_model]
    scale = mod_b_ref[0, :, d_model:2 * d_model]
    hb_ref[...] = (hn * (1.0 + scale) + shift).astype(BF16)
    c0 = 0
    for width, o_ref in zip(widths, out_refs):
        for s0 in range(0, width, slab):
            r = jnp.dot(hb_ref[...], wi_ref[:, c0 + s0:c0 + s0 + slab],
                        preferred_element_type=F32)
            o_ref[0, :, s0:s0 + slab] = r.astype(BF16)
        c0 += width


def _layer_boundary(y, wo_bf16, x, mod_a, post_gain, mod_b, pre_gain, wi_bf16, widths, tm, slab):
    b, s, d = x.shape
    k = y.shape[-1]
    n = wi_bf16.shape[1]
    row = lambda bi, i: (bi, i, 0)
    const = lambda bi, i: (0, 0)
    per_batch = lambda bi, i: (bi, 0, 0)
    return pl.pallas_call(
        functools.partial(_boundary_kernel, d_model=d, widths=widths, slab=slab),
        grid=(b, s // tm),
        in_specs=[
            pl.BlockSpec((1, tm, k), row),
            pl.BlockSpec((k, d), const, pipeline_mode=pl.Buffered(1)),
            pl.BlockSpec((1, tm, d), row),
            pl.BlockSpec((1, 1, 3 * d), per_batch),
            pl.BlockSpec((1, d), const),
            pl.BlockSpec((1, 1, 3 * d), per_batch),
            pl.BlockSpec((1, d), const),
            pl.BlockSpec((d, n), const, pipeline_mode=pl.Buffered(1)),
        ],
        out_specs=[pl.BlockSpec((1, tm, d), row)] + [pl.BlockSpec((1, tm, wd), row) for wd in widths],
        out_shape=([jax.ShapeDtypeStruct((b, s, d), F32)]
                   + [jax.ShapeDtypeStruct((b, s, wd), BF16) for wd in widths]),
        scratch_shapes=[pltpu.VMEM((tm, d), BF16)],
        compiler_params=_cparams(("arbitrary", "arbitrary")),
        name="layer_boundary",
    )(y, wo_bf16, x, mod_a, post_gain, mod_b, pre_gain, wi_bf16)


N_ALIBI_FEATURES = 6
ATTN_ROW_CHUNK = 16
ATTN_SUM_ROWS = 16
ATTN_SCORE_SLOTS = 4
ATTN_PROB_SLOTS = 2
ATTN_DIAG_STEPS = 8
ATTN_PAST_STEPS = 8
ATTN_PIPE_DEPTH = 3
ATTN_LAMBDA_ROWS = 32
ATTN_STAT_ROWS = 4
ATTN_ZERO_EXP2 = 150.0
ATTN_BOUND_SLACK = 1.0
ATTN_NORM_SLACK = 1.01


def _alibi_key_features(s):
    pos = jnp.arange(s, dtype=jnp.int32)[:, None]
    col = jnp.arange(LANES, dtype=jnp.int32)[None, :]
    lo = (pos % BF16_EXACT_INT).astype(F32)
    hi = (pos // BF16_EXACT_INT).astype(F32)
    feat = jnp.where(col < 3, lo, jnp.where(col < N_ALIBI_FEATURES, hi, 0.0))
    return feat.astype(BF16)


def _attn_schedule(nq):
    pairs = [(qi, qi) for qi in range(nq)] + [(qi, j) for qi in range(nq) for j in range(qi)]
    qb, kb = zip(*pairs)
    return np.asarray(qb, np.int32), np.asarray(kb, np.int32)


def _attn_kernel(slope_ref, qb_ref, kb_ref, qn_ref, kn_ref, qT_ref, k_ref, kf_ref, vT_ref, g_ref,
                 lamv_ref, subln_ref, y_ref, qs_ref, feat_ref, *scratch, tq, nq, lam_init):
    h = pl.program_id(1)
    tk = tq
    w = 2 * tq
    rc = ATTN_ROW_CHUNK
    n_chunks = tk // rc
    n_pos = nq * (nq + 1) // 2
    scratch = list(scratch)
    s_refs = [scratch.pop(0) for _ in range(ATTN_SCORE_SLOTS)]
    mc_refs = [scratch.pop(0) for _ in range(ATTN_SCORE_SLOTS)]
    p_refs = [scratch.pop(0) for _ in range(ATTN_PROB_SLOTS)]
    al_refs = [scratch.pop(0) for _ in range(ATTN_PROB_SLOTS)]
    m_ref, acc_ref, corr_ref, thr_ref, ql_ref, kl_ref = scratch
    n_past = n_pos - nq
    assert nq % ATTN_DIAG_STEPS == 0 and nq >= ATTN_DIAG_STEPS
    assert ATTN_DIAG_STEPS % ATTN_SCORE_SLOTS == 0 and ATTN_PAST_STEPS % ATTN_SCORE_SLOTS == 0
    assert n_past % ATTN_PAST_STEPS == 0

    slope2 = slope_ref[h] * LOG2E
    fr = lax.broadcasted_iota(jnp.int32, (LANES, w), 0)
    x = jnp.full((LANES, w), slope2, F32)
    hi = x.astype(BF16).astype(F32)
    mid = (x - hi).astype(BF16).astype(F32)
    lo = x - hi - mid
    part = jnp.where((fr == 0) | (fr == 3), hi, jnp.where((fr == 1) | (fr == 4), mid, lo))
    feat = jnp.where(fr < 3, part, jnp.where(fr < N_ALIBI_FEATURES, part * BF16_EXACT_INT, 0.0))
    feat_ref[...] = feat.astype(BF16)
    kk = lax.broadcasted_iota(jnp.int32, (tk, w), 0)
    qq = lax.broadcasted_iota(jnp.int32, (tk, w), 1)
    qq = jnp.where(qq >= tq, qq - tq, qq)
    ahead = (kk - qq).astype(F32)
    same_chunk = (kk >> CHUNK_SHIFT) == (qq >> CHUNK_SHIFT)
    corr_ref[...] = jnp.where(kk <= qq, 0.0,
                              jnp.where(same_chunk, -2.0 * slope2 * ahead, MASK_VALUE))

    def init_query_block(qi, carry):
        qT = qT_ref[0, 0, qi]
        row = lax.broadcasted_iota(jnp.int32, qT.shape, 0)
        zero = jnp.zeros_like(qT)
        qs_ref[qi, :, 0:tq] = jnp.where(row < DA_HEAD_DIM, qT, zero)
        qs_ref[qi, :, tq:w] = jnp.where(row >= DA_HEAD_DIM, qT, zero)
        return carry

    lax.fori_loop(0, nq, init_query_block, 0)
    ones_rows = jnp.ones((ATTN_SUM_ROWS, tk), BF16)

    def qk(pair, t, s_slot, diag):
        qi, kj = pair(t)
        start = kj * tk if isinstance(kj, int) else pl.multiple_of(kj * tk, tk)
        ka = jnp.concatenate([k_ref[0, 0, pl.ds(start, tk), :], kf_ref[pl.ds(start, tk), :]],
                             axis=1)
        qa = jnp.concatenate([qs_ref[qi], feat_ref[...]], axis=0)
        s = jnp.dot(ka, qa, preferred_element_type=F32)
        if diag:
            s = s + corr_ref[...]
        s_refs[s_slot][...] = s
        mc_refs[s_slot][0:1, :] = jnp.max(s, axis=0, keepdims=True)

    def sm(pair, t, s_slot, p_slot, first):
        qi, _ = pair(t)
        if first:
            m_next = mc_refs[s_slot][0:1, :]
        else:
            m_prev = m_ref[qi]
            m_next = jnp.maximum(m_prev, mc_refs[s_slot][0:1, :])
            al_refs[p_slot][0:1, :] = jnp.exp2(m_prev - m_next)
        m_ref[qi] = m_next
        m_rows = jnp.broadcast_to(m_next, (rc, w))
        for c in range(n_chunks):
            p = jnp.exp2(s_refs[s_slot][c * rc:(c + 1) * rc, :] - m_rows)
            p_refs[p_slot][c * rc:(c + 1) * rc, :] = p.astype(BF16)

    def pv(pair, t, p_slot, first):
        qi, kj = pair(t)
        v_aug = jnp.concatenate([vT_ref[0, 0, kj], ones_rows], axis=0)
        upd = jnp.dot(v_aug, p_refs[p_slot][...], preferred_element_type=F32)
        if first:
            acc_ref[qi] = upd
        else:
            acc_ref[qi] = al_refs[p_slot][0:1, :] * acc_ref[qi] + upd

    def run_pipeline(pair, n, steps, diag):
        depth = ATTN_PIPE_DEPTH

        def time_step(tau, u, do_qk=True, do_sm=True, do_pv=True):
            if do_qk:
                qk(pair, tau, u, diag)
            if do_sm:
                sm(pair, tau - (depth - 1), (u - (depth - 1)) % ATTN_SCORE_SLOTS,
                   (u - (depth - 1)) % ATTN_PROB_SLOTS, diag)
            if do_pv:
                pv(pair, tau - depth, (u - depth) % ATTN_PROB_SLOTS, diag)

        for tau in range(depth):
            time_step(tau, tau % ATTN_SCORE_SLOTS, do_sm=tau >= depth - 1, do_pv=False)

        def body(i, carry):
            for u in range(steps):
                time_step(depth + steps * i + u, (depth + u) % ATTN_SCORE_SLOTS)
            return carry

        lax.fori_loop(0, n // steps - 1, body, 0)
        for u in range(steps - depth):
            time_step(n - (steps - depth) + u, (depth + u) % ATTN_SCORE_SLOTS)
        for d in range(depth):
            time_step(n + d, d % ATTN_SCORE_SLOTS, do_qk=False, do_sm=d < depth - 1)

    run_pipeline(lambda t: (t, t), nq, ATTN_DIAG_STEPS, diag=True)

    norm_base = (pl.program_id(0) * pl.num_programs(1) + h) * nq
    m_min = jnp.min(m_ref[...].reshape(nq, w), axis=1, keepdims=True)
    for qi in range(nq):
        thr_ref[qi] = m_min[qi, 0] - (ATTN_ZERO_EXP2 + ATTN_BOUND_SLACK)

    def choose(i, carry):
        count, pad_q, pad_k = carry
        qi, kj = qb_ref[nq + i], kb_ref[nq + i]
        last_key = ((kj + 1) * tk - 1).astype(F32)
        bound = (qn_ref[norm_base + qi] * kn_ref[norm_base + kj] * ATTN_NORM_SLACK
                 + slope2 * last_key)
        keep = bound >= thr_ref[qi]
        ql_ref[count] = qi
        kl_ref[count] = kj
        return (count + keep.astype(jnp.int32), jnp.where(keep, pad_q, qi),
                jnp.where(keep, pad_k, kj))

    n_keep, pad_q, pad_k = lax.fori_loop(0, n_past, choose,
                                         (jnp.int32(0), jnp.int32(0), jnp.int32(0)))
    n_pad = (-n_keep) & (ATTN_PAST_STEPS - 1)
    for r in range(ATTN_PAST_STEPS - 1):
        @pl.when(r < n_pad)
        def _():
            ql_ref[n_keep + r] = pad_q
            kl_ref[n_keep + r] = pad_k

    n_visit = n_keep + n_pad

    @pl.when(n_visit > 0)
    def _():
        run_pipeline(lambda t: (ql_ref[t], kl_ref[t]), n_visit, ATTN_PAST_STEPS, diag=False)

    lv = lamv_ref[...]
    lam = (jnp.exp(jnp.sum(lv[0:1] * lv[1:2], axis=-1, keepdims=True))
           - jnp.exp(jnp.sum(lv[2:3] * lv[3:4], axis=-1, keepdims=True)) + lam_init)

    def finish_query_block(qi, carry):
        o = acc_ref[qi, 0:DA_V_DIM, :] * (1.0 / acc_ref[qi, DA_V_DIM:DA_V_DIM + 1, :])
        out = o[:, 0:tq] - lam * o[:, tq:w]
        ms = jnp.mean(out * out, axis=0, keepdims=True)
        out = out * lax.rsqrt(ms + HEAD_NORM_EPS) * subln_ref[...] * (1.0 - lam_init)
        rows = pl.ds(pl.multiple_of(qi * tq, tq), tq)
        g = g_ref[0, rows, :].astype(F32)
        y_ref[0, rows, :] = (g * jax.nn.sigmoid(g) * out.T).astype(BF16)
        return carry

    def finish_two(i, carry):
        finish_query_block(2 * i, carry)
        return finish_query_block(2 * i + 1, carry)

    lax.fori_loop(0, nq // 2, finish_two, 0)


def _diff_attention(qT, k, vT, g, q_norms, k_norms, slopes, lamv, subln, lam_init):
    b, nh, nq, dk, tq = qT.shape
    s = k.shape[2]
    w = 2 * tq
    qb, kb = _attn_schedule(nq)
    kernel = functools.partial(_attn_kernel, tq=tq, nq=nq, lam_init=lam_init)
    smem = pl.BlockSpec(memory_space=pltpu.SMEM)
    once = pl.Buffered(1)
    return pl.pallas_call(
        kernel,
        grid=(b, nh),
        in_specs=[
            smem, smem, smem, smem, smem,
            pl.BlockSpec((1, 1, nq, dk, tq), lambda bi, h: (bi, h, 0, 0, 0)),
            pl.BlockSpec((1, 1, s, dk), lambda bi, h: (bi, h, 0, 0)),
            pl.BlockSpec((s, LANES), lambda bi, h: (0, 0), pipeline_mode=once),
            pl.BlockSpec((1, 1, nq, DA_V_DIM, tq), lambda bi, h: (bi, h, 0, 0, 0)),
            pl.BlockSpec((1, s, DA_V_DIM), lambda bi, h: (bi, 0, h)),
            pl.BlockSpec((ATTN_LAMBDA_ROWS, LANES), lambda bi, h: (0, 0)),
            pl.BlockSpec((DA_V_DIM, 1), lambda bi, h: (0, 0)),
        ],
        out_specs=pl.BlockSpec((1, s, DA_V_DIM), lambda bi, h: (bi, 0, h)),
        out_shape=jax.ShapeDtypeStruct((b, s, nh * DA_V_DIM), BF16),
        scratch_shapes=(
            [pltpu.VMEM((nq, LANES, w), BF16),
             pltpu.VMEM((LANES, w), BF16)]
            + [pltpu.VMEM((tq, w), F32)] * ATTN_SCORE_SLOTS
            + [pltpu.VMEM((ATTN_STAT_ROWS, w), F32)] * ATTN_SCORE_SLOTS
            + [pltpu.VMEM((tq, w), BF16)] * ATTN_PROB_SLOTS
            + [pltpu.VMEM((ATTN_STAT_ROWS, w), F32)] * ATTN_PROB_SLOTS
            + [pltpu.VMEM((nq, 1, w), F32),
               pltpu.VMEM((nq, DA_V_DIM + ATTN_SUM_ROWS, w), F32),
               pltpu.VMEM((tq, w), F32),
               pltpu.SMEM((nq,), F32),
               pltpu.SMEM((len(qb),), jnp.int32),
               pltpu.SMEM((len(qb),), jnp.int32)]
        ),
        compiler_params=_cparams(("arbitrary", "arbitrary")),
        name="diff_attention",
    )(slopes, jnp.asarray(qb), jnp.asarray(kb), q_norms.reshape(-1), k_norms.reshape(-1), qT, k,
      _alibi_key_features(s), vT, g, lamv, subln.reshape(DA_V_DIM, 1))


def _ret_kernel(lg_ref, q_ref, k_ref, v_ref, g_ref, y_ref, state_ref, decay_ref, qdec_ref,
                kdec_ref, *, blk):
    n = pl.program_id(1)
    dk, dv = RET_QK_DIM, RET_V_DIM

    @pl.when(n == 0)
    def _():
        state_ref[...] = jnp.zeros(state_ref.shape, F32)
        ti = lax.broadcasted_iota(jnp.int32, (blk, blk), 0)
        ui = lax.broadcasted_iota(jnp.int32, (blk, blk), 1)
        dist = jnp.abs(ti - ui).astype(F32)
        allowed = (ui >> CHUNK_SHIFT) <= (ti >> CHUNK_SHIFT)
        t = lax.broadcasted_iota(jnp.int32, (blk, dk), 0).astype(F32)
        for h in range(RET_HEADS):
            decay_ref[h] = jnp.where(allowed, jnp.exp(lg_ref[h] * dist), 0.0) * dk ** -0.5
            qdec_ref[h] = jnp.exp(lg_ref[h] * t)
            kdec_ref[h] = jnp.exp(lg_ref[h] * (blk - t)) * dk ** -0.5

    def block(rows):
        for h in range(RET_HEADS):
            lg = lg_ref[h]
            q = q_ref[0, rows, h * dk:(h + 1) * dk]
            k = k_ref[0, rows, h * dk:(h + 1) * dk]
            v = v_ref[0, rows, h * dv:(h + 1) * dv]
            qd = (q.astype(F32) * qdec_ref[h]).astype(BF16)
            kd = (k.astype(F32) * kdec_ref[h]).astype(BF16)

            a = lax.dot_general(q, k, (((1,), (1,)), ((), ())), preferred_element_type=F32)
            a = (a * decay_ref[h]).astype(BF16)
            out = jnp.dot(a, v, preferred_element_type=F32)
            out = out + jnp.dot(qd, state_ref[h].astype(BF16), preferred_element_type=F32)
            kv = lax.dot_general(kd, v, (((0,), (0,)), ((), ())), preferred_element_type=F32)
            block_decay = jnp.exp(lg * jnp.full((1, 1), float(blk), F32))
            state_ref[h] = block_decay * state_ref[h] + kv

            ss = jnp.sum(out * out, axis=-1, keepdims=True)
            out = out * lax.rsqrt(ss * (1.0 / dv) + HEAD_NORM_EPS)
            g = g_ref[0, rows, h * dv:(h + 1) * dv].astype(F32)
            y_ref[0, rows, h * dv:(h + 1) * dv] = (g * jax.nn.sigmoid(g) * out).astype(BF16)

    for sub in range(q_ref.shape[1] // blk):
        block(slice(sub * blk, (sub + 1) * blk))


def _retention(q, k, v, g, log_gammas, blk, tile):
    b, s, qk_w = q.shape
    v_w = v.shape[-1]
    assert tile % blk == 0 and s % tile == 0
    return pl.pallas_call(
        functools.partial(_ret_kernel, blk=blk),
        grid=(b, s // tile),
        in_specs=[
            pl.BlockSpec(memory_space=pltpu.SMEM),
            pl.BlockSpec((1, tile, qk_w), lambda bi, i: (bi, i, 0)),
            pl.BlockSpec((1, tile, qk_w), lambda bi, i: (bi, i, 0)),
            pl.BlockSpec((1, tile, v_w), lambda bi, i: (bi, i, 0)),
            pl.BlockSpec((1, tile, v_w), lambda bi, i: (bi, i, 0)),
        ],
        out_specs=pl.BlockSpec((1, tile, v_w), lambda bi, i: (bi, i, 0)),
        out_shape=jax.ShapeDtypeStruct((b, s, v_w), BF16),
        scratch_shapes=[
            pltpu.VMEM((RET_HEADS, RET_QK_DIM, RET_V_DIM), F32),
            pltpu.VMEM((RET_HEADS, blk, blk), F32),
            pltpu.VMEM((RET_HEADS, blk, RET_QK_DIM), F32),
            pltpu.VMEM((RET_HEADS, blk, RET_QK_DIM), F32),
        ],
        compiler_params=_cparams(("arbitrary", "arbitrary")),
        name="retention",
    )(log_gammas, q, k, v, g)


def kernel(x, c, ada_w, ada_b, pre_gain, post_gain, da_w_in, da_w_out, da_lambda_q1, da_lambda_k1,
           da_lambda_q2, da_lambda_k2, da_subln_gain, ret_w_in, ret_w_out):
    b, s, d = x.shape
    mod = _modulation(c, ada_w, ada_b)

    qk_w = 2 * DA_HEADS * DA_HEAD_DIM
    v_w = DA_HEADS * DA_V_DIM
    q_scale = LOG2E * DA_HEAD_DIM ** -0.5
    da_plan = (("headT", 0, DA_HEADS, 2 * DA_HEAD_DIM, q_scale, 0),
               ("head", qk_w, DA_HEADS, 2 * DA_HEAD_DIM, 1.0, DA_HEADS),
               ("headT", 2 * qk_w, DA_HEADS, DA_V_DIM, 1.0, None),
               ("tok", 2 * qk_w + v_w, None, v_w, 1.0, None))
    mod0 = mod[0].reshape(b, 1, 3 * d)
    qT, k, vT, g, norms = _in_projection(x, mod0, pre_gain[0:1], da_w_in[0].astype(BF16),
                                         da_plan, tm=ATTN_TILE, slab=INPROJ_SLAB)
    q_norms = jnp.transpose(norms[:, :, 0:DA_HEADS, 0], (0, 2, 1))
    k_norms = jnp.transpose(norms[:, :, DA_HEADS:2 * DA_HEADS, 0], (0, 2, 1))
    lam_init = 0.8 - 0.6 * math.exp(-0.3 * 0)
    slopes = jnp.asarray([2.0 ** (-8.0 * (hh + 1) / DA_HEADS) for hh in range(DA_HEADS)], F32)
    lamv = jnp.stack([da_lambda_q1[0], da_lambda_k1[0], da_lambda_q2[0], da_lambda_k2[0]])
    lamv = jnp.pad(lamv, ((0, ATTN_LAMBDA_ROWS - 4), (0, LANES - DA_HEAD_DIM)))
    y = _diff_attention(qT, k, vT, g, q_norms, k_norms, slopes, lamv, da_subln_gain[0], lam_init)
    rqk = RET_HEADS * RET_QK_DIM
    rv = RET_HEADS * RET_V_DIM
    mod1 = mod[1].reshape(b, 1, 3 * d)
    x, q, k, v, g = _layer_boundary(y, da_w_out[0].astype(BF16), x, mod0, post_gain[0:1], mod1,
                                    pre_gain[1:2], ret_w_in[0].astype(BF16), (rqk, rqk, rv, rv),
                                    tm=BOUNDARY_ROWS, slab=INPROJ_SLAB)
    gammas = 1.0 - 2.0 ** (-5.0 - jnp.arange(RET_HEADS, dtype=F32))
    y = _retention(q, k, v, g, jnp.log(gammas), blk=RET_BLOCK, tile=RET_TILE)
    return _out_projection(y, ret_w_out[0].astype(BF16), x, mod1, post_gain[1:2], tm=OUTPROJ_ROWS)
```

```python
import functools
import math

import jax
import jax.numpy as jnp
import numpy as np
from jax import lax
from jax.experimental import pallas as pl
from jax.experimental.pallas import tpu as pltpu

CHUNK = 64
CHUNK_SHIFT = 6
DA_HEADS = 8
DA_HEAD_DIM = 64
DA_V_DIM = 128
RET_HEADS = 4
RET_QK_DIM = 256
RET_V_DIM = 512
NORM_EPS = 1e-6
HEAD_NORM_EPS = 1e-5
MASK_VALUE = -1e30
LOG2E = 1.4426950408889634

V7X_VMEM_LIMIT_BYTES = 56 * 1024 * 1024
LANES = 128
SUBLANES = 8
BF16_EXACT_INT = 256

ATTN_TILE = 512
BOUNDARY_ROWS = 512
OUTPROJ_ROWS = 1024
RET_BLOCK = 256
RET_TILE = 512
INPROJ_SLAB = 256

BF16 = jnp.bfloat16
F32 = jnp.float32


def _cparams(sem, vmem=V7X_VMEM_LIMIT_BYTES, flags=None):
    return pltpu.CompilerParams(dimension_semantics=sem, vmem_limit_bytes=vmem, flags=flags)


def _mod_kernel(c_ref, w_ref, b_ref, o_ref):
    c = c_ref[...]
    cond = c * jax.nn.sigmoid(c)
    o_ref[0] = jnp.dot(cond, w_ref[0], preferred_element_type=F32) + b_ref[0]


def _modulation(c, ada_w, ada_b):
    depth, d, d3 = ada_w.shape
    b = c.shape[0]
    n_tiles = d3 // d
    return pl.pallas_call(
        _mod_kernel,
        grid=(depth, n_tiles),
        in_specs=[
            pl.BlockSpec((b, d), lambda l, j: (0, 0)),
            pl.BlockSpec((1, d, d), lambda l, j: (l, 0, j)),
            pl.BlockSpec((1, 1, d), lambda l, j: (l, 0, j)),
        ],
        out_specs=pl.BlockSpec((1, b, d), lambda l, j: (l, 0, j)),
        out_shape=jax.ShapeDtypeStruct((depth, b, d3), F32),
        compiler_params=_cparams(("arbitrary", "arbitrary")),
        name="modulation",
    )(c, ada_w, ada_b.reshape(depth, 1, d3))


def _inproj_kernel(x_ref, mod_ref, gain_ref, w_ref, *refs, d_model, plan, slab, n_norm_rows):
    out_refs = refs[:len(plan)]
    norm_ref = refs[len(plan)] if n_norm_rows else None
    hb_ref = refs[-1]
    x = x_ref[0]
    ms = jnp.mean(x * x, axis=-1, keepdims=True)
    hn = x * lax.rsqrt(ms + NORM_EPS) * gain_ref[...]
    shift = mod_ref[0, :, 0:d_model]
    scale = mod_ref[0, :, d_model:2 * d_model]
    hb_ref[...] = (hn * (1.0 + scale) + shift).astype(BF16)
    for (kind, c0, n_heads, width, out_scale, norm_row), o_ref in zip(plan, out_refs):
        total = width if kind == "tok" else n_heads * width
        for s0 in range(0, total, slab):
            r = jnp.dot(hb_ref[...], w_ref[:, c0 + s0:c0 + s0 + slab],
                        preferred_element_type=F32)
            if out_scale != 1.0:
                r = r * out_scale
            if kind == "tok":
                o_ref[0, :, s0:s0 + slab] = r.astype(BF16)
            elif kind == "head" and width >= slab:
                hh, off = divmod(s0, width)
                o_ref[0, hh, :, off:off + slab] = r.astype(BF16)
            else:
                for p in range(slab // width):
                    head = s0 // width + p
                    piece = r[:, p * width:(p + 1) * width]
                    if kind == "head":
                        o_ref[0, head] = piece.astype(BF16)
                    else:
                        o_ref[0, head, 0] = piece.T.astype(BF16)
                    if norm_row is not None:
                        norm2 = jnp.max(jnp.sum(piece * piece, axis=1, keepdims=True),
                                        axis=0, keepdims=True)
                        norm_ref[0, 0, norm_row + head:norm_row + head + 1, :] = (
                            jnp.broadcast_to(jnp.sqrt(norm2), (1, LANES)))


def _in_projection(x, mod, gain, w_bf16, plan, tm, slab):
    b, s, d = x.shape
    n = w_bf16.shape[1]
    out_shapes, out_specs = [], []
    n_norm_rows = sum(e[2] for e in plan if e[5] is not None)
    for (kind, c0, n_heads, width, _, _) in plan:
        if kind == "tok":
            out_shapes.append(jax.ShapeDtypeStruct((b, s, width), BF16))
            out_specs.append(pl.BlockSpec((1, tm, width), lambda bi, i: (bi, i, 0)))
        elif kind == "head":
            out_shapes.append(jax.ShapeDtypeStruct((b, n_heads, s, width), BF16))
            out_specs.append(pl.BlockSpec((1, n_heads, tm, width), lambda bi, i: (bi, 0, i, 0)))
        else:
            assert slab % width == 0
            out_shapes.append(jax.ShapeDtypeStruct((b, n_heads, s // tm, width, tm), BF16))
            out_specs.append(pl.BlockSpec((1, n_heads, 1, width, tm),
                                          lambda bi, i: (bi, 0, i, 0, 0)))
    if n_norm_rows:
        out_shapes.append(jax.ShapeDtypeStruct((b, s // tm, n_norm_rows, LANES), F32))
        out_specs.append(pl.BlockSpec((1, 1, n_norm_rows, LANES), lambda bi, i: (bi, i, 0, 0)))
    return pl.pallas_call(
        functools.partial(_inproj_kernel, d_model=d, plan=plan, slab=slab,
                          n_norm_rows=n_norm_rows),
        grid=(b, s // tm),
        in_specs=[
            pl.BlockSpec((1, tm, d), lambda bi, i: (bi, i, 0)),
            pl.BlockSpec((1, 1, 3 * d), lambda bi, i: (bi, 0, 0)),
            pl.BlockSpec((1, d), lambda bi, i: (0, 0)),
            pl.BlockSpec((d, n), lambda bi, i: (0, 0), pipeline_mode=pl.Buffered(1)),
        ],
        out_specs=out_specs,
        out_shape=out_shapes,
        scratch_shapes=[pltpu.VMEM((tm, d), BF16)],
        compiler_params=_cparams(("arbitrary", "arbitrary")),
        name="in_projection",
    )(x, mod, gain, w_bf16)


def _outproj_kernel(y_ref, w_ref, x_ref, mod_ref, gain_ref, o_ref, *, d_model):
    t = jnp.dot(y_ref[0], w_ref[...], preferred_element_type=F32)
    ms = jnp.mean(t * t, axis=-1, keepdims=True)
    n = t * lax.rsqrt(ms + NORM_EPS) * gain_ref[...]
    gate = mod_ref[0, :, 2 * d_model:3 * d_model]
    o_ref[0] = x_ref[0] + gate * n


def _out_projection(y, w_bf16, x, mod, gain, tm):
    b, s, d = x.shape
    k = y.shape[-1]
    return pl.pallas_call(
        functools.partial(_outproj_kernel, d_model=d),
        grid=(b, s // tm),
        in_specs=[
            pl.BlockSpec((1, tm, k), lambda bi, i: (bi, i, 0)),
            pl.BlockSpec((k, d), lambda bi, i: (0, 0)),
            pl.BlockSpec((1, tm, d), lambda bi, i: (bi, i, 0)),
            pl.BlockSpec((1, 1, 3 * d), lambda bi, i: (bi, 0, 0)),
            pl.BlockSpec((1, d), lambda bi, i: (0, 0)),
        ],
        out_specs=pl.BlockSpec((1, tm, d), lambda bi, i: (bi, i, 0)),
        out_shape=jax.ShapeDtypeStruct((b, s, d), F32),
        compiler_params=_cparams(("arbitrary", "arbitrary")),
        name="out_projection",
    )(y, w_bf16, x, mod, gain)


def _boundary_kernel(y_ref, wo_ref, x_ref, mod_a_ref, post_gain_ref, mod_b_ref, pre_gain_ref, wi_ref,
                     x_out_ref, *refs, d_model, widths, slab):
    out_refs, hb_ref = refs[:len(widths)], refs[-1]
    t = jnp.dot(y_ref[0], wo_ref[...], preferred_element_type=F32)
    ms = jnp.mean(t * t, axis=-1, keepdims=True)
    n = t * lax.rsqrt(ms + NORM_EPS) * post_gain_ref[...]
    x_new = x_ref[0] + mod_a_ref[0, :, 2 * d_model:3 * d_model] * n
    x_out_ref[0] = x_new
    ms = jnp.mean(x_new * x_new, axis=-1, keepdims=True)
    hn = x_new * lax.rsqrt(ms + NORM_EPS) * pre_gain_ref[...]
    shift = mod_b_ref[0, :, 0:d_model]
    scale = mod_b_ref[0, :, d_model:2 * d_model]
    hb_ref[...] = (hn * (1.0 + scale) + shift).astype(BF16)
    c0 = 0
    for width, o_ref in zip(widths, out_refs):
        for s0 in range(0, width, slab):
            r = jnp.dot(hb_ref[...], wi_ref[:, c0 + s0:c0 + s0 + slab],
                        preferred_element_type=F32)
            o_ref[0, :, s0:s0 + slab] = r.astype(BF16)
        c0 += width


def _layer_boundary(y, wo_bf16, x, mod_a, post_gain, mod_b, pre_gain, wi_bf16, widths, tm, slab):
    b, s, d = x.shape
    k = y.shape[-1]
    n = wi_bf16.shape[1]
    row = lambda bi, i: (bi, i, 0)
    const = lambda bi, i: (0, 0)
    per_batch = lambda bi, i: (bi, 0, 0)
    return pl.pallas_call(
        functools.partial(_boundary_kernel, d_model=d, widths=widths, slab=slab),
        grid=(b, s // tm),
        in_specs=[
            pl.BlockSpec((1, tm, k), row),
            pl.BlockSpec((k, d), const, pipeline_mode=pl.Buffered(1)),
            pl.BlockSpec((1, tm, d), row),
            pl.BlockSpec((1, 1, 3 * d), per_batch),
            pl.BlockSpec((1, d), const),
            pl.BlockSpec((1, 1, 3 * d), per_batch),
            pl.BlockSpec((1, d), const),
            pl.BlockSpec((d, n), const, pipeline_mode=pl.Buffered(1)),
        ],
        out_specs=[pl.BlockSpec((1, tm, d), row)] + [pl.BlockSpec((1, tm, wd), row) for wd in widths],
        out_shape=([jax.ShapeDtypeStruct((b, s, d), F32)]
                   + [jax.ShapeDtypeStruct((b, s, wd), BF16) for wd in widths]),
        scratch_shapes=[pltpu.VMEM((tm, d), BF16)],
        compiler_params=_cparams(("arbitrary", "arbitrary")),
        name="layer_boundary",
    )(y, wo_bf16, x, mod_a, post_gain, mod_b, pre_gain, wi_bf16)


N_ALIBI_FEATURES = 6
ATTN_ROW_CHUNK = 16
ATTN_SUM_ROWS = 16
ATTN_SCORE_SLOTS = 4
ATTN_PROB_SLOTS = 2
ATTN_DIAG_STEPS = 8
ATTN_PAST_STEPS = 8
ATTN_PIPE_DEPTH = 3
ATTN_LAMBDA_ROWS = 32
ATTN_STAT_ROWS = 4
ATTN_ZERO_EXP2 = 150.0
ATTN_BOUND_SLACK = 1.0
ATTN_NORM_SLACK = 1.01


def _alibi_key_features(s):
    pos = jnp.arange(s, dtype=jnp.int32)[:, None]
    col = jnp.arange(LANES, dtype=jnp.int32)[None, :]
    lo = (pos % BF16_EXACT_INT).astype(F32)
    hi = (pos // BF16_EXACT_INT).astype(F32)
    feat = jnp.where(col < 3, lo, jnp.where(col < N_ALIBI_FEATURES, hi, 0.0))
    return feat.astype(BF16)


def _attn_schedule(nq):
    pairs = [(qi, qi) for qi in range(nq)] + [(qi, j) for qi in range(nq) for j in range(qi)]
    qb, kb = zip(*pairs)
    return np.asarray(qb, np.int32), np.asarray(kb, np.int32)


def _attn_kernel(slope_ref, qb_ref, kb_ref, qn_ref, kn_ref, qT_ref, k_ref, kf_ref, vT_ref, g_ref,
                 lamv_ref, subln_ref, y_ref, qs_ref, feat_ref, *scratch, tq, nq, lam_init):
    h = pl.program_id(1)
    tk = tq
    w = 2 * tq
    rc = ATTN_ROW_CHUNK
    n_chunks = tk // rc
    n_pos = nq * (nq + 1) // 2
    scratch = list(scratch)
    s_refs = [scratch.pop(0) for _ in range(ATTN_SCORE_SLOTS)]
    mc_refs = [scratch.pop(0) for _ in range(ATTN_SCORE_SLOTS)]
    p_refs = [scratch.pop(0) for _ in range(ATTN_PROB_SLOTS)]
    al_refs = [scratch.pop(0) for _ in range(ATTN_PROB_SLOTS)]
    m_ref, acc_ref, corr_ref, thr_ref, ql_ref, kl_ref = scratch
    n_past = n_pos - nq
    assert nq % ATTN_DIAG_STEPS == 0 and nq >= ATTN_DIAG_STEPS
    assert ATTN_DIAG_STEPS % ATTN_SCORE_SLOTS == 0 and ATTN_PAST_STEPS % ATTN_SCORE_SLOTS == 0
    assert n_past % ATTN_PAST_STEPS == 0

    slope2 = slope_ref[h] * LOG2E
    fr = lax.broadcasted_iota(jnp.int32, (LANES, w), 0)
    x = jnp.full((LANES, w), slope2, F32)
    hi = x.astype(BF16).astype(F32)
    mid = (x - hi).astype(BF16).astype(F32)
    lo = x - hi - mid
    part = jnp.where((fr == 0) | (fr == 3), hi, jnp.where((fr == 1) | (fr == 4), mid, lo))
    feat = jnp.where(fr < 3, part, jnp.where(fr < N_ALIBI_FEATURES, part * BF16_EXACT_INT, 0.0))
    feat_ref[...] = feat.astype(BF16)
    kk = lax.broadcasted_iota(jnp.int32, (tk, w), 0)
    qq = lax.broadcasted_iota(jnp.int32, (tk, w), 1)
    qq = jnp.where(qq >= tq, qq - tq, qq)
    ahead = (kk - qq).astype(F32)
    same_chunk = (kk >> CHUNK_SHIFT) == (qq >> CHUNK_SHIFT)
    corr_ref[...] = jnp.where(kk <= qq, 0.0,
                              jnp.where(same_chunk, -2.0 * slope2 * ahead, MASK_VALUE))

    def init_query_block(qi, carry):
        qT = qT_ref[0, 0, qi]
        row = lax.broadcasted_iota(jnp.int32, qT.shape, 0)
        zero = jnp.zeros_like(qT)
        qs_ref[qi, :, 0:tq] = jnp.where(row < DA_HEAD_DIM, qT, zero)
        qs_ref[qi, :, tq:w] = jnp.where(row >= DA_HEAD_DIM, qT, zero)
        return carry

    lax.fori_loop(0, nq, init_query_block, 0)
    ones_rows = jnp.ones((ATTN_SUM_ROWS, tk), BF16)

    def qk(pair, t, s_slot, diag):
        qi, kj = pair(t)
        start = kj * tk if isinstance(kj, int) else pl.multiple_of(kj * tk, tk)
        ka = jnp.concatenate([k_ref[0, 0, pl.ds(start, tk), :], kf_ref[pl.ds(start, tk), :]],
                             axis=1)
        qa = jnp.concatenate([qs_ref[qi], feat_ref[...]], axis=0)
        s = jnp.dot(ka, qa, preferred_element_type=F32)
        if diag:
            s = s + corr_ref[...]
        s_refs[s_slot][...] = s
        mc_refs[s_slot][0:1, :] = jnp.max(s, axis=0, keepdims=True)

    def sm(pair, t, s_slot, p_slot, first):
        qi, _ = pair(t)
        if first:
            m_next = mc_refs[s_slot][0:1, :]
        else:
            m_prev = m_ref[qi]
            m_next = jnp.maximum(m_prev, mc_refs[s_slot][0:1, :])
            al_refs[p_slot][0:1, :] = jnp.exp2(m_prev - m_next)
        m_ref[qi] = m_next
        m_rows = jnp.broadcast_to(m_next, (rc, w))
        for c in range(n_chunks):
            p = jnp.exp2(s_refs[s_slot][c * rc:(c + 1) * rc, :] - m_rows)
            p_refs[p_slot][c * rc:(c + 1) * rc, :] = p.astype(BF16)

    def pv(pair, t, p_slot, first):
        qi, kj = pair(t)
        v_aug = jnp.concatenate([vT_ref[0, 0, kj], ones_rows], axis=0)
        upd = jnp.dot(v_aug, p_refs[p_slot][...], preferred_element_type=F32)
        if first:
            acc_ref[qi] = upd
        else:
            acc_ref[qi] = al_refs[p_slot][0:1, :] * acc_ref[qi] + upd

    def run_pipeline(pair, n, steps, diag):
        depth = ATTN_PIPE_DEPTH

        def time_step(tau, u, do_qk=True, do_sm=True, do_pv=True):
            if do_qk:
                qk(pair, tau, u, diag)
            if do_sm:
                sm(pair, tau - (depth - 1), (u - (depth - 1)) % ATTN_SCORE_SLOTS,
                   (u - (depth - 1)) % ATTN_PROB_SLOTS, diag)
            if do_pv:
                pv(pair, tau - depth, (u - depth) % ATTN_PROB_SLOTS, diag)

        for tau in range(depth):
            time_step(tau, tau % ATTN_SCORE_SLOTS, do_sm=tau >= depth - 1, do_pv=False)

        def body(i, carry):
            for u in range(steps):
                time_step(depth + steps * i + u, (depth + u) % ATTN_SCORE_SLOTS)
            return carry

        lax.fori_loop(0, n // steps - 1, body, 0)
        for u in range(steps - depth):
            time_step(n - (steps - depth) + u, (depth + u) % ATTN_SCORE_SLOTS)
        for d in range(depth):
            time_step(n + d, d % ATTN_SCORE_SLOTS, do_qk=False, do_sm=d < depth - 1)

    run_pipeline(lambda t: (t, t), nq, ATTN_DIAG_STEPS, diag=True)

    norm_base = (pl.program_id(0) * pl.num_programs(1) + h) * nq
    m_min = jnp.min(m_ref[...].reshape(nq, w), axis=1, keepdims=True)
    for qi in range(nq):
        thr_ref[qi] = m_min[qi, 0] - (ATTN_ZERO_EXP2 + ATTN_BOUND_SLACK)

    def choose(i, carry):
        count, pad_q, pad_k = carry
        qi, kj = qb_ref[nq + i], kb_ref[nq + i]
        last_key = ((kj + 1) * tk - 1).astype(F32)
        bound = (qn_ref[norm_base + qi] * kn_ref[norm_base + kj] * ATTN_NORM_SLACK
                 + slope2 * last_key)
        keep = bound >= thr_ref[qi]
        ql_ref[count] = qi
        kl_ref[count] = kj
        return (count + keep.astype(jnp.int32), jnp.where(keep, pad_q, qi),
                jnp.where(keep, pad_k, kj))

    n_keep, pad_q, pad_k = lax.fori_loop(0, n_past, choose,
                                         (jnp.int32(0), jnp.int32(0), jnp.int32(0)))
    n_pad = (-n_keep) & (ATTN_PAST_STEPS - 1)
    for r in range(ATTN_PAST_STEPS - 1):
        @pl.when(r < n_pad)
        def _():
            ql_ref[n_keep + r] = pad_q
            kl_ref[n_keep + r] = pad_k

    n_visit = n_keep + n_pad

    @pl.when(n_visit > 0)
    def _():
        run_pipeline(lambda t: (ql_ref[t], kl_ref[t]), n_visit, ATTN_PAST_STEPS, diag=False)

    lv = lamv_ref[...]
    lam = (jnp.exp(jnp.sum(lv[0:1] * lv[1:2], axis=-1, keepdims=True))
           - jnp.exp(jnp.sum(lv[2:3] * lv[3:4], axis=-1, keepdims=True)) + lam_init)

    def finish_query_block(qi, carry):
        o = acc_ref[qi, 0:DA_V_DIM, :] * (1.0 / acc_ref[qi, DA_V_DIM:DA_V_DIM + 1, :])
        out = o[:, 0:tq] - lam * o[:, tq:w]
        ms = jnp.mean(out * out, axis=0, keepdims=True)
        out = out * lax.rsqrt(ms + HEAD_NORM_EPS) * subln_ref[...] * (1.0 - lam_init)
        rows = pl.ds(pl.multiple_of(qi * tq, tq), tq)
        g = g_ref[0, rows, :].astype(F32)
        y_ref[0, rows, :] = (g * jax.nn.sigmoid(g) * out.T).astype(BF16)
        return carry

    def finish_two(i, carry):
        finish_query_block(2 * i, carry)
        return finish_query_block(2 * i + 1, carry)

    lax.fori_loop(0, nq // 2, finish_two, 0)


def _diff_attention(qT, k, vT, g, q_norms, k_norms, slopes, lamv, subln, lam_init):
    b, nh, nq, dk, tq = qT.shape
    s = k.shape[2]
    w = 2 * tq
    qb, kb = _attn_schedule(nq)
    kernel = functools.partial(_attn_kernel, tq=tq, nq=nq, lam_init=lam_init)
    smem = pl.BlockSpec(memory_space=pltpu.SMEM)
    once = pl.Buffered(1)
    return pl.pallas_call(
        kernel,
        grid=(b, nh),
        in_specs=[
            smem, smem, smem, smem, smem,
            pl.BlockSpec((1, 1, nq, dk, tq), lambda bi, h: (bi, h, 0, 0, 0)),
            pl.BlockSpec((1, 1, s, dk), lambda bi, h: (bi, h, 0, 0)),
            pl.BlockSpec((s, LANES), lambda bi, h: (0, 0), pipeline_mode=once),
            pl.BlockSpec((1, 1, nq, DA_V_DIM, tq), lambda bi, h: (bi, h, 0, 0, 0)),
            pl.BlockSpec((1, s, DA_V_DIM), lambda bi, h: (bi, 0, h)),
            pl.BlockSpec((ATTN_LAMBDA_ROWS, LANES), lambda bi, h: (0, 0)),
            pl.BlockSpec((DA_V_DIM, 1), lambda bi, h: (0, 0)),
        ],
        out_specs=pl.BlockSpec((1, s, DA_V_DIM), lambda bi, h: (bi, 0, h)),
        out_shape=jax.ShapeDtypeStruct((b, s, nh * DA_V_DIM), BF16),
        scratch_shapes=(
            [pltpu.VMEM((nq, LANES, w), BF16),
             pltpu.VMEM((LANES, w), BF16)]
            + [pltpu.VMEM((tq, w), F32)] * ATTN_SCORE_SLOTS
            + [pltpu.VMEM((ATTN_STAT_ROWS, w), F32)] * ATTN_SCORE_SLOTS
            + [pltpu.VMEM((tq, w), BF16)] * ATTN_PROB_SLOTS
            + [pltpu.VMEM((ATTN_STAT_ROWS, w), F32)] * ATTN_PROB_SLOTS
            + [pltpu.VMEM((nq, 1, w), F32),
               pltpu.VMEM((nq, DA_V_DIM + ATTN_SUM_ROWS, w), F32),
               pltpu.VMEM((tq, w), F32),
               pltpu.SMEM((nq,), F32),
               pltpu.SMEM((len(qb),), jnp.int32),
               pltpu.SMEM((len(qb),), jnp.int32)]
        ),
        compiler_params=_cparams(("arbitrary", "arbitrary")),
        name="diff_attention",
    )(slopes, jnp.asarray(qb), jnp.asarray(kb), q_norms.reshape(-1), k_norms.reshape(-1), qT, k,
      _alibi_key_features(s), vT, g, lamv, subln.reshape(DA_V_DIM, 1))


def _ret_kernel(lg_ref, q_ref, k_ref, v_ref, g_ref, wo_ref, x_ref, mod_ref, post_gain_ref, o_ref,
                y_ref, state_ref, decay_ref, qdec_ref, kdec_ref, *, blk, d_model):
    n = pl.program_id(1)
    dk, dv = RET_QK_DIM, RET_V_DIM

    @pl.when(n == 0)
    def _():
        state_ref[...] = jnp.zeros(state_ref.shape, F32)
        ti = lax.broadcasted_iota(jnp.int32, (blk, blk), 0)
        ui = lax.broadcasted_iota(jnp.int32, (blk, blk), 1)
        dist = jnp.abs(ti - ui).astype(F32)
        allowed = (ui >> CHUNK_SHIFT) <= (ti >> CHUNK_SHIFT)
        t = lax.broadcasted_iota(jnp.int32, (blk, dk), 0).astype(F32)
        for h in range(RET_HEADS):
            decay_ref[h] = jnp.where(allowed, jnp.exp(lg_ref[h] * dist), 0.0) * dk ** -0.5
            qdec_ref[h] = jnp.exp(lg_ref[h] * t)
            kdec_ref[h] = jnp.exp(lg_ref[h] * (blk - t)) * dk ** -0.5

    def block(rows):
        for h in range(RET_HEADS):
            lg = lg_ref[h]
            q = q_ref[0, rows, h * dk:(h + 1) * dk]
            k = k_ref[0, rows, h * dk:(h + 1) * dk]
            v = v_ref[0, rows, h * dv:(h + 1) * dv]
            qd = (q.astype(F32) * qdec_ref[h]).astype(BF16)
            kd = (k.astype(F32) * kdec_ref[h]).astype(BF16)

            a = lax.dot_general(q, k, (((1,), (1,)), ((), ())), preferred_element_type=F32)
            a = (a * decay_ref[h]).astype(BF16)
            out = jnp.dot(a, v, preferred_element_type=F32)
            out = out + jnp.dot(qd, state_ref[h].astype(BF16), preferred_element_type=F32)
            kv = lax.dot_general(kd, v, (((0,), (0,)), ((), ())), preferred_element_type=F32)
            block_decay = jnp.exp(lg * jnp.full((1, 1), float(blk), F32))
            state_ref[h] = block_decay * state_ref[h] + kv

            ss = jnp.sum(out * out, axis=-1, keepdims=True)
            out = out * lax.rsqrt(ss * (1.0 / dv) + HEAD_NORM_EPS)
            g = g_ref[0, rows, h * dv:(h + 1) * dv].astype(F32)
            y_ref[rows, h * dv:(h + 1) * dv] = (g * jax.nn.sigmoid(g) * out).astype(BF16)
        t = jnp.dot(y_ref[rows, :], wo_ref[...], preferred_element_type=F32)
        ms = jnp.mean(t * t, axis=-1, keepdims=True)
        nrm = t * lax.rsqrt(ms + NORM_EPS) * post_gain_ref[...]
        o_ref[0, rows, :] = x_ref[0, rows, :] + mod_ref[0, :, 2 * d_model:3 * d_model] * nrm

    for sub in range(q_ref.shape[1] // blk):
        block(slice(sub * blk, (sub + 1) * blk))


def _retention(q, k, v, g, log_gammas, wo_bf16, x, mod, post_gain, blk, tile):
    b, s, qk_w = q.shape
    v_w = v.shape[-1]
    d = x.shape[-1]
    assert tile % blk == 0 and s % tile == 0
    return pl.pallas_call(
        functools.partial(_ret_kernel, blk=blk, d_model=d),
        grid=(b, s // tile),
        in_specs=[
            pl.BlockSpec(memory_space=pltpu.SMEM),
            pl.BlockSpec((1, tile, qk_w), lambda bi, i: (bi, i, 0)),
            pl.BlockSpec((1, tile, qk_w), lambda bi, i: (bi, i, 0)),
            pl.BlockSpec((1, tile, v_w), lambda bi, i: (bi, i, 0)),
            pl.BlockSpec((1, tile, v_w), lambda bi, i: (bi, i, 0)),
            pl.BlockSpec((v_w, d), lambda bi, i: (0, 0), pipeline_mode=pl.Buffered(1)),
            pl.BlockSpec((1, tile, d), lambda bi, i: (bi, i, 0)),
            pl.BlockSpec((1, 1, 3 * d), lambda bi, i: (bi, 0, 0)),
            pl.BlockSpec((1, d), lambda bi, i: (0, 0)),
        ],
        out_specs=pl.BlockSpec((1, tile, d), lambda bi, i: (bi, i, 0)),
        out_shape=jax.ShapeDtypeStruct((b, s, d), F32),
        scratch_shapes=[
            pltpu.VMEM((tile, v_w), BF16),
            pltpu.VMEM((RET_HEADS, RET_QK_DIM, RET_V_DIM), F32),
            pltpu.VMEM((RET_HEADS, blk, blk), F32),
            pltpu.VMEM((RET_HEADS, blk, RET_QK_DIM), F32),
            pltpu.VMEM((RET_HEADS, blk, RET_QK_DIM), F32),
        ],
        compiler_params=_cparams(("arbitrary", "arbitrary")),
        name="retention",
    )(log_gammas, q, k, v, g, wo_bf16, x, mod, post_gain)


def kernel(x, c, ada_w, ada_b, pre_gain, post_gain, da_w_in, da_w_out, da_lambda_q1, da_lambda_k1,
           da_lambda_q2, da_lambda_k2, da_subln_gain, ret_w_in, ret_w_out):
    b, s, d = x.shape
    mod = _modulation(c, ada_w, ada_b)

    qk_w = 2 * DA_HEADS * DA_HEAD_DIM
    v_w = DA_HEADS * DA_V_DIM
    q_scale = LOG2E * DA_HEAD_DIM ** -0.5
    da_plan = (("headT", 0, DA_HEADS, 2 * DA_HEAD_DIM, q_scale, 0),
               ("head", qk_w, DA_HEADS, 2 * DA_HEAD_DIM, 1.0, DA_HEADS),
               ("headT", 2 * qk_w, DA_HEADS, DA_V_DIM, 1.0, None),
               ("tok", 2 * qk_w + v_w, None, v_w, 1.0, None))
    mod0 = mod[0].reshape(b, 1, 3 * d)
    qT, k, vT, g, norms = _in_projection(x, mod0, pre_gain[0:1], da_w_in[0].astype(BF16),
                                         da_plan, tm=ATTN_TILE, slab=INPROJ_SLAB)
    q_norms = jnp.transpose(norms[:, :, 0:DA_HEADS, 0], (0, 2, 1))
    k_norms = jnp.transpose(norms[:, :, DA_HEADS:2 * DA_HEADS, 0], (0, 2, 1))
    lam_init = 0.8 - 0.6 * math.exp(-0.3 * 0)
    slopes = jnp.asarray([2.0 ** (-8.0 * (hh + 1) / DA_HEADS) for hh in range(DA_HEADS)], F32)
    lamv = jnp.stack([da_lambda_q1[0], da_lambda_k1[0], da_lambda_q2[0], da_lambda_k2[0]])
    lamv = jnp.pad(lamv, ((0, ATTN_LAMBDA_ROWS - 4), (0, LANES - DA_HEAD_DIM)))
    y = _diff_attention(qT, k, vT, g, q_norms, k_norms, slopes, lamv, da_subln_gain[0], lam_init)
    rqk = RET_HEADS * RET_QK_DIM
    rv = RET_HEADS * RET_V_DIM
    mod1 = mod[1].reshape(b, 1, 3 * d)
    x, q, k, v, g = _layer_boundary(y, da_w_out[0].astype(BF16), x, mod0, post_gain[0:1], mod1,
                                    pre_gain[1:2], ret_w_in[0].astype(BF16), (rqk, rqk, rv, rv),
                                    tm=BOUNDARY_ROWS, slab=INPROJ_SLAB)
    gammas = 1.0 - 2.0 ** (-5.0 - jnp.arange(RET_HEADS, dtype=F32))
    return _retention(q, k, v, g, jnp.log(gammas), ret_w_out[0].astype(BF16), x, mod1,
                      post_gain[1:2], blk=RET_BLOCK, tile=RET_TILE)
```
